```python
import math
import jax
import jax.numpy as jnp
from jax import lax
import numpy as np

D_MODEL = 1024
BATCH = 16
SEQ = 256
DEPTH = 2
DEC_BATCH = 2
DEC_SEQ = 2048
PAST_LEN = 512

GRID_W = 64
N_DIR = 2
D_S5 = D_MODEL // 2
S5_GROUP = 16
G_S5 = D_S5 // S5_GROUP
N_S5 = 64
D_RWKV = D_MODEL // 2
RWKV_HEAD = 64
H_RWKV = D_RWKV // RWKV_HEAD
LORA_W = 64
LORA_A = 64
LORA_G = 128
RWKV_COLS = 3 * D_RWKV + LORA_W + LORA_A + LORA_G
RWKV_SPLITS = (D_RWKV, 2 * D_RWKV, 3 * D_RWKV, 3 * D_RWKV + LORA_W, 3 * D_RWKV + LORA_W + LORA_A)
N_IN = D_S5 + RWKV_COLS + 2 * D_MODEL
N_EXPERTS = 16
N_GROUPS = 4
EXPERTS_PER_GROUP = N_EXPERTS // N_GROUPS
TOP_K = 2
D_EXPERT = D_MODEL // 2
ALPHA = (2 * DEPTH) ** 0.25
BETA = (8 * DEPTH) ** -0.25
LN_EPS = 1e-5
GN_EPS = 64e-5

kernel_name = 'hybrid_s5_rwkv7_moe_diffusion_step'


def layer_norm(x, g, b):
    x = x.astype(jnp.float32)
    mu = jnp.mean(x, -1, keepdims=True)
    var = jnp.mean(jnp.square(x - mu), -1, keepdims=True)
    return (x - mu) * lax.rsqrt(var + LN_EPS) * g + b


def shift_seq(z):
    bn, t, ch = z.shape
    z2 = z.reshape(bn, t, ch // 2, 2)
    prev = jnp.pad(z2[:, :-1, :, 0], ((0, 0), (1, 0), (0, 0)))
    nxt = jnp.pad(z2[:, 1:, :, 1], ((0, 0), (0, 1), (0, 0)))
    return jnp.stack([prev, nxt], -1).reshape(bn, t, ch)


def shift_grid(z):
    bn, t, ch = z.shape
    rows = t // GRID_W
    g = z.reshape(bn, rows, GRID_W, ch // 4, 4)
    left = jnp.pad(g[:, :, :-1, :, 0], ((0, 0), (0, 0), (1, 0), (0, 0)))
    right = jnp.pad(g[:, :, 1:, :, 1], ((0, 0), (0, 0), (0, 1), (0, 0)))
    up = jnp.pad(g[:, :-1, :, :, 2], ((0, 0), (1, 0), (0, 0), (0, 0)))
    down = jnp.pad(g[:, 1:, :, :, 3], ((0, 0), (0, 1), (0, 0), (0, 0)))
    return jnp.stack([left, right, up, down], -1).reshape(bn, t, ch)


def s5_discretize(a_re, a_im, log_dt, b_re, b_im):
    a_re = jnp.minimum(a_re.astype(jnp.float32), -1e-4)
    a_im = a_im.astype(jnp.float32)
    dt = jnp.exp(log_dt.astype(jnp.float32))[:, None]
    mag = jnp.exp(a_re * dt)
    ab_re = mag * jnp.cos(a_im * dt)
    ab_im = mag * jnp.sin(a_im * dt)
    den = a_re * a_re + a_im * a_im
    nr = ab_re - 1.0
    q_re = ((nr * a_re + ab_im * a_im) / den)[..., None]
    q_im = ((ab_im * a_re - nr * a_im) / den)[..., None]
    bb_re = q_re * b_re - q_im * b_im
    bb_im = q_re * b_im + q_im * b_re
    return ab_re, ab_im, bb_re, bb_im


def _affine_combine(e1, e2):
    a1r, a1i, b1r, b1i = e1
    a2r, a2i, b2r, b2i = e2
    return (a1r * a2r - a1i * a2i, a1r * a2i + a1i * a2r,
            a2r * b1r - a2i * b1i + b2r, a2r * b1i + a2i * b1r + b2i)


def s5_scan(ab_re, ab_im, bu_re, bu_im, h0_re, h0_im, reverse):
    first = -1 if reverse else 0
    bu_re = bu_re.at[:, first].add(ab_re * h0_re - ab_im * h0_im)
    bu_im = bu_im.at[:, first].add(ab_re * h0_im + ab_im * h0_re)
    a_re = jnp.broadcast_to(ab_re, bu_re.shape)
    a_im = jnp.broadcast_to(ab_im, bu_im.shape)
    _, _, h_re, h_im = lax.associative_scan(_affine_combine, (a_re, a_im, bu_re, bu_im),
                                            reverse=reverse, axis=1)
    end = 0 if reverse else -1
    return h_re, h_im, jnp.stack([h_re[:, end], h_im[:, end]], 1)


def s5_branch(u, p, h0):
    bn, t, _ = u.shape
    ug = u.reshape(bn, t, G_S5, S5_GROUP)
    y = ug * p['s5_d'].reshape(G_S5, S5_GROUP)
    finals = []
    for d in range(N_DIR):
        ab_re, ab_im, bb_re, bb_im = s5_discretize(p['s5_a_re'][d], p['s5_a_im'][d], p['s5_log_dt'][d],
                                                   p['s5_b_re'][d], p['s5_b_im'][d])
        bu_re = jnp.einsum('btgp,gnp->btgn', ug, bb_re)
        bu_im = jnp.einsum('btgp,gnp->btgn', ug, bb_im)
        h_re, h_im, fin = s5_scan(ab_re, ab_im, bu_re, bu_im,
                                  h0[:, d, 0].astype(jnp.float32), h0[:, d, 1].astype(jnp.float32),
                                  reverse=(d == 1))
        y = y + jnp.einsum('btgn,gpn->btgp', h_re, p['s5_c_re'][d]) \
              - jnp.einsum('btgn,gpn->btgp', h_im, p['s5_c_im'][d])
        finals.append(fin)
    y = jax.nn.gelu(y.reshape(bn, t, D_S5))
    y = y * jax.nn.sigmoid(y @ p['s5_w_glu'])
    return y, jnp.stack(finals, 1)


def rwkv7_scan(r, w, k, v, a, b, s0, reverse):
    def step(s, inp):
        r_t, w_t, k_t, v_t, a_t, b_t = inp
        sa = jnp.einsum('bhij,bhj->bhi', s, a_t)
        s = s * w_t[:, :, None, :] + sa[..., None] * b_t[:, :, None, :] + v_t[..., None] * k_t[:, :, None, :]
        return s, jnp.einsum('bhij,bhj->bhi', s, r_t)
    xs = tuple(jnp.moveaxis(z, 1, 0) for z in (r, w, k, v, a, b))
    s_fin, ys = lax.scan(step, s0, xs, reverse=reverse)
    return jnp.moveaxis(ys, 0, 1), s_fin


def rwkv_branch(zr, p, s0, shift_fn):
    bn, t, _ = zr.shape
    zr = zr + (shift_fn(zr) - zr) * p['rw_mu']
    r, k, v, xw, xa, xg = jnp.split(zr, RWKV_SPLITS, axis=-1)
    heads = lambda z: z.reshape(bn, t, H_RWKV, RWKV_HEAD)
    g = jax.nn.sigmoid(xg) @ p['rw_g2']
    kk = heads(k * p['rw_k_k'])
    kk = kk / jnp.maximum(jnp.linalg.norm(kk, axis=-1, keepdims=True), 1e-12)
    ys, finals = [], []
    for d in range(N_DIR):
        w_log = -jax.nn.softplus(-(p['rw_w0'][d] + jnp.tanh(xw) @ p['rw_w2'][d])) - 0.5
        decay = jnp.exp(-jnp.exp(w_log))
        a = jax.nn.sigmoid(p['rw_a0'][d] + xa @ p['rw_a2'][d])
        k_d = k * (1.0 + (a - 1.0) * p['rw_k_a'])
        y_d, s_d = rwkv7_scan(heads(r), heads(decay), heads(k_d), heads(v), -kk, kk * heads(a),
                              s0[:, d].astype(jnp.float32), reverse=(d == 1))
        ys.append(y_d)
        finals.append(s_d)
    y = ys[0] + ys[1]
    mu = jnp.mean(y, -1, keepdims=True)
    var = jnp.mean(jnp.square(y - mu), -1, keepdims=True)
    y = ((y - mu) * lax.rsqrt(var + GN_EPS)).reshape(bn, t, D_RWKV) * p['rw_lnx_w'] + p['rw_lnx_b']
    bonus = jnp.sum(heads(r) * heads(k) * p['rw_r_k'], -1, keepdims=True) * heads(v)
    y = (y + bonus.reshape(bn, t, D_RWKV)) * g
    return y, jnp.stack(finals, 1)


def parallel_mixer(h, p, s5_init, rw_init, shift_fn):
    z = (h @ p['w_in']).astype(jnp.float32)
    u = z[..., :D_S5]
    zr = z[..., D_S5:D_S5 + RWKV_COLS]
    gate_s5 = z[..., D_S5 + RWKV_COLS:D_S5 + RWKV_COLS + D_MODEL]
    gate_rw = z[..., D_S5 + RWKV_COLS + D_MODEL:]
    y_s5, s5_fin = s5_branch(u, p, s5_init)
    y_rw, rw_fin = rwkv_branch(zr, p, rw_init, shift_fn)
    merged = jax.nn.sigmoid(gate_s5) * (y_s5 @ p['w_up_s5']) + jax.nn.sigmoid(gate_rw) * (y_rw @ p['w_up_rwkv'])
    return merged @ p['w_out'], s5_fin, rw_fin


def grouped_moe(h, w_router, b_router, w_gate, w_up, w_down):
    bn, t, d = h.shape
    hf = h.reshape(bn * t, d)
    probs = jax.nn.softmax((hf @ w_router + b_router).astype(jnp.float32), axis=-1)
    pg = probs.reshape(-1, N_GROUPS, EXPERTS_PER_GROUP)
    group_score = jnp.sum(lax.top_k(pg, TOP_K)[0], -1)
    in_group = jnp.argmax(group_score, -1)[:, None] == jnp.arange(N_GROUPS)[None, :]
    masked = jnp.where(in_group[:, :, None], pg, -1.0).reshape(-1, N_EXPERTS)
    top_vals, top_idx = lax.top_k(masked, TOP_K)
    top_w = top_vals / jnp.sum(top_vals, -1, keepdims=True)
    combine = jnp.sum(jax.nn.one_hot(top_idx, N_EXPERTS, dtype=jnp.float32) * top_w[..., None], axis=1)
    hid = jax.nn.silu(jnp.einsum('nd,edf->nef', hf, w_gate)) * jnp.einsum('nd,edf->nef', hf, w_up)
    out = jnp.einsum('nef,efd->nd', hid * combine[..., None], w_down)
    return out.reshape(bn, t, d)


def run_trunk(x, cond, s5_init, rw_init, shift_fn, layers, w_router, b_router):
    s5_out, rw_out = [], []
    for l in range(DEPTH):
        p = layers[l]
        mod = (jax.nn.silu(cond) @ p['w_ada'] + p['b_ada']).reshape(cond.shape[0], 6, 1, D_MODEL)
        sh1, sc1, g1, sh2, sc2, g2 = (mod[:, i] for i in range(6))
        h = x * (1.0 + sc1) + sh1
        mix, s5_fin, rw_fin = parallel_mixer(h, p, s5_init[:, l], rw_init[:, l], shift_fn)
        x = layer_norm(ALPHA * x + g1 * mix, p['ln1_g'], p['ln1_b'])
        h = x * (1.0 + sc2) + sh2
        ffn = grouped_moe(h, w_router, b_router, p['w_exp_gate'], p['w_exp_up'], p['w_exp_down'])
        x = layer_norm(ALPHA * x + g2 * ffn, p['ln2_g'], p['ln2_b'])
        s5_out.append(s5_fin)
        rw_out.append(rw_fin)
    return x, jnp.stack(s5_out, 1), jnp.stack(rw_out, 1)


def setup_inputs(seed: int = 0) -> dict:
    key = jax.random.key(seed)
    ks = iter(jax.random.split(key, 48))
    nrm = lambda shape, scale: jax.random.normal(next(ks), shape, jnp.float32) * scale
    uni = lambda shape, lo, hi: jax.random.uniform(next(ks), shape, jnp.float32, lo, hi)
    L, D = DEPTH, D_MODEL
    n_idx = jnp.arange(N_S5, dtype=jnp.float32)
    return {
        'x_prompt': nrm((BATCH, SEQ, D), 1.0),
        'x_sample': nrm((DEC_BATCH, DEC_SEQ, D), 1.0),
        'state_s5': nrm((DEC_BATCH, L, N_DIR, 2, G_S5, N_S5), 0.5),
        'state_rwkv': nrm((DEC_BATCH, L, N_DIR, H_RWKV, RWKV_HEAD, RWKV_HEAD), 0.3),
        'c': nrm((DEC_BATCH, D), 1.0),
        'c_ctx': nrm((D,), 1.0),
        'w_ada': nrm((L, D, 6 * D), 0.5 * D ** -0.5),
        'b_ada': nrm((L, 6 * D), 0.02),
        'w_in': nrm((L, D, N_IN), D ** -0.5),
        's5_a_re': -0.5 + nrm((L, N_DIR, G_S5, N_S5), 0.01),
        's5_a_im': math.pi * n_idx + nrm((L, N_DIR, G_S5, N_S5), 0.01),
        's5_log_dt': uni((L, N_DIR, G_S5), math.log(1e-3), math.log(1e-1)),
        's5_b_re': nrm((L, N_DIR, G_S5, N_S5, S5_GROUP), (2 * S5_GROUP) ** -0.5),
        's5_b_im': nrm((L, N_DIR, G_S5, N_S5, S5_GROUP), (2 * S5_GROUP) ** -0.5),
        's5_c_re': nrm((L, N_DIR, G_S5, S5_GROUP, N_S5), 0.5 * N_S5 ** -0.5),
        's5_c_im': nrm((L, N_DIR, G_S5, S5_GROUP, N_S5), 0.5 * N_S5 ** -0.5),
        's5_d': nrm((L, D_S5), 1.0),
        's5_w_glu': nrm((L, D_S5, D_S5), D_S5 ** -0.5),
        'rw_mu': uni((L, RWKV_COLS), 0.0, 1.0),
        'rw_w0': uni((L, N_DIR, D_RWKV), -6.5, -1.5),
        'rw_w2': nrm((L, N_DIR, LORA_W, D_RWKV), 0.1 * LORA_W ** -0.5),
        'rw_a0': nrm((L, N_DIR, D_RWKV), 0.1),
        'rw_a2': nrm((L, N_DIR, LORA_A, D_RWKV), 0.1 * LORA_A ** -0.5),
        'rw_g2': nrm((L, LORA_G, D_RWKV), LORA_G ** -0.5),
        'rw_k_k': 0.85 + nrm((L, D_RWKV), 0.02),
        'rw_k_a': 1.0 + nrm((L, D_RWKV), 0.02),
        'rw_r_k': nrm((L, H_RWKV, RWKV_HEAD), 0.1),
        'rw_lnx_w': 1.0 + nrm((L, D_RWKV), 0.02),
        'rw_lnx_b': nrm((L, D_RWKV), 0.02),
        'w_up_s5': nrm((L, D_S5, D), D_S5 ** -0.5),
        'w_up_rwkv': nrm((L, D_RWKV, D), D_RWKV ** -0.5),
        'w_out': nrm((L, D, D), BETA * D ** -0.5),
        'ln1_g': 1.0 + nrm((L, D), 0.02),
        'ln1_b': nrm((L, D), 0.02),
        'ln2_g': 1.0 + nrm((L, D), 0.02),
        'ln2_b': nrm((L, D), 0.02),
        'w_router': nrm((D, N_EXPERTS), D ** -0.5),
        'b_router': nrm((N_EXPERTS,), 0.01),
        'w_exp_gate': nrm((L, N_EXPERTS, D, D_EXPERT), D ** -0.5),
        'w_exp_up': nrm((L, N_EXPERTS, D, D_EXPERT), D ** -0.5),
        'w_exp_down': nrm((L, N_EXPERTS, D_EXPERT, D), BETA * D_EXPERT ** -0.5),
    }


def reference(x_prompt, x_sample, state_s5, state_rwkv, c, c_ctx, w_ada, b_ada, w_in,
              s5_a_re, s5_a_im, s5_log_dt, s5_b_re, s5_b_im, s5_c_re, s5_c_im, s5_d, s5_w_glu,
              rw_mu, rw_w0, rw_w2, rw_a0, rw_a2, rw_g2, rw_k_k, rw_k_a, rw_r_k, rw_lnx_w, rw_lnx_b,
              w_up_s5, w_up_rwkv, w_out, ln1_g, ln1_b, ln2_g, ln2_b, w_router, b_router,
              w_exp_gate, w_exp_up, w_exp_down):
    stacked = dict(w_ada=w_ada, b_ada=b_ada, w_in=w_in,
                   s5_a_re=s5_a_re, s5_a_im=s5_a_im, s5_log_dt=s5_log_dt, s5_b_re=s5_b_re,
                   s5_b_im=s5_b_im, s5_c_re=s5_c_re, s5_c_im=s5_c_im, s5_d=s5_d, s5_w_glu=s5_w_glu,
                   rw_mu=rw_mu, rw_w0=rw_w0, rw_w2=rw_w2, rw_a0=rw_a0, rw_a2=rw_a2, rw_g2=rw_g2,
                   rw_k_k=rw_k_k, rw_k_a=rw_k_a, rw_r_k=rw_r_k, rw_lnx_w=rw_lnx_w, rw_lnx_b=rw_lnx_b,
                   w_up_s5=w_up_s5, w_up_rwkv=w_up_rwkv, w_out=w_out,
                   ln1_g=ln1_g, ln1_b=ln1_b, ln2_g=ln2_g, ln2_b=ln2_b,
                   w_exp_gate=w_exp_gate, w_exp_up=w_exp_up, w_exp_down=w_exp_down)
    layers = [{name: arr[l] for name, arr in stacked.items()} for l in range(DEPTH)]

    nb = x_prompt.shape[0]
    zero_s5 = jnp.zeros((nb, DEPTH, N_DIR, 2, G_S5, N_S5), jnp.float32)
    zero_rw = jnp.zeros((nb, DEPTH, N_DIR, H_RWKV, RWKV_HEAD, RWKV_HEAD), jnp.float32)
    y_prompt, new_state_s5, new_state_rwkv = run_trunk(x_prompt, c_ctx[None, :], zero_s5, zero_rw,
                                                       shift_seq, layers, w_router, b_router)

    y_sample, _, _ = run_trunk(x_sample, c, state_s5, state_rwkv, shift_grid, layers, w_router, b_router)
    return (y_prompt, y_sample, new_state_s5, new_state_rwkv)
```

```python
import functools
import math

import jax
import jax.numpy as jnp
from jax import lax
from jax.experimental import pallas as pl
from jax.experimental.pallas import tpu as pltpu

D_MODEL = 1024
DEPTH = 2
GRID_W = 64
D_S5 = 512
S5_GROUP = 16
G_S5 = 32
N_S5 = 64
S5_STATE = G_S5 * N_S5
D_RWKV = 512
RWKV_HEAD = 64
H_RWKV = 8
LORA_W = 64
LORA_A = 64
LORA_G = 128
RWKV_COLS = 3 * D_RWKV + LORA_W + LORA_A + LORA_G
N_EXPERTS = 16
N_GROUPS = 4
EXPERTS_PER_GROUP = 4
D_EXPERT = 512
ALPHA = (2 * DEPTH) ** 0.25
LN_EPS = 1e-5
GN_EPS = 64e-5

SUBLANES = 8
LANES = 128
VMEM_LIMIT = 56 * 1024 * 1024

F32 = jnp.float32
BF16 = jnp.bfloat16


def _params(sem):
    return pltpu.CompilerParams(dimension_semantics=sem, vmem_limit_bytes=VMEM_LIMIT)


def _dot(a, b):
    return jnp.dot(a.astype(BF16), b.astype(BF16), preferred_element_type=F32)


def _dot_split(x, w_exact):
    hi = x.astype(BF16)
    lo = (x - hi.astype(F32)).astype(BF16)
    return (jnp.dot(hi, w_exact, preferred_element_type=F32)
            + jnp.dot(lo, w_exact, preferred_element_type=F32))


def _sigmoid(x):
    return 1.0 / (1.0 + jnp.exp(-x))


def _silu(x):
    return x * _sigmoid(x)


def _layer_norm(x, g, b):
    mu = jnp.mean(x, -1, keepdims=True)
    xc = x - mu
    var = jnp.mean(xc * xc, -1, keepdims=True)
    return xc * lax.rsqrt(var + LN_EPS) * g + b


def _modulate(x, shift8, scale8):
    rows, d = x.shape
    x3 = x.reshape(rows // SUBLANES, SUBLANES, d)
    return (x3 * (1.0 + scale8)[None] + shift8[None]).reshape(rows, d)


def _ada_kernel(cond_ref, w_ref, b_ref, o_ref):
    c = cond_ref[...]
    o_ref[0] = _dot(_silu(c), w_ref[0]) + b_ref[0]


def _ada_mod(cond8, w_ada, b_ada):
    nl = w_ada.shape[0]
    d = D_MODEL
    return pl.pallas_call(
        _ada_kernel,
        grid=(nl, 6),
        in_specs=[pl.BlockSpec((SUBLANES, d), lambda l, k: (0, 0)),
                  pl.BlockSpec((1, d, d), lambda l, k: (l, 0, k)),
                  pl.BlockSpec((1, 1, d), lambda l, k: (l, 0, k))],
        out_specs=pl.BlockSpec((1, SUBLANES, d), lambda l, k: (l, 0, k)),
        out_shape=jax.ShapeDtypeStruct((nl, SUBLANES, 6 * d), F32),
        compiler_params=_params(("arbitrary", "arbitrary")),
        name="ada_mod",
    )(cond8, w_ada, b_ada.reshape(nl, 1, 6 * d))


def _inproj_kernel(x_ref, pat_ref, wu_ref, wzr_ref, wgs_ref, wgr_ref, u_ref, zr_ref, gs_ref, gr_ref):
    h = _modulate(x_ref[...], pat_ref[0, 0], pat_ref[0, 1]).astype(BF16)
    u_ref[...] = jnp.dot(h, wu_ref[...], preferred_element_type=F32)
    zr_ref[...] = jnp.dot(h, wzr_ref[...], preferred_element_type=F32)
    gs_ref[...] = jnp.dot(h, wgs_ref[...], preferred_element_type=F32)
    gr_ref[...] = jnp.dot(h, wgr_ref[...], preferred_element_type=F32)


def _inproj(x, pat, w_in_l, rows_per_trunk, tm=256):
    rows = x.shape[0]
    d = D_MODEL
    tpt = rows_per_trunk // tm
    wu = w_in_l[:, :D_S5].astype(BF16)
    wzr = w_in_l[:, D_S5:D_S5 + RWKV_COLS].astype(BF16)
    wgs = w_in_l[:, D_S5 + RWKV_COLS:D_S5 + RWKV_COLS + d].astype(BF16)
    wgr = w_in_l[:, D_S5 + RWKV_COLS + d:].astype(BF16)
    full = lambda n: pl.BlockSpec((d, n), lambda i: (0, 0))
    rowblk = lambda n: pl.BlockSpec((tm, n), lambda i: (i, 0))
    return pl.pallas_call(
        _inproj_kernel,
        grid=(rows // tm,),
        in_specs=[rowblk(d),
                  pl.BlockSpec((1, 6, SUBLANES, d), lambda i: (i // tpt, 0, 0, 0)),
                  full(D_S5), full(RWKV_COLS), full(d), full(d)],
        out_specs=[rowblk(D_S5), rowblk(RWKV_COLS), rowblk(d), rowblk(d)],
        out_shape=[jax.ShapeDtypeStruct((rows, n), F32) for n in (D_S5, RWKV_COLS, d, d)],
        compiler_params=_params(("arbitrary",)),
        name="in_proj",
    )(x, pat, wu, wzr, wgs, wgr)


def _s5_disc_kernel(are_ref, aim_ref, ldt_ref, bre_ref, bim_ref, abre_ref, abim_ref, bbre_ref, bbim_ref):
    a_re = jnp.minimum(are_ref[...], -1e-4)
    a_im = aim_ref[...]
    dt = jnp.exp(ldt_ref[...])
    mag = jnp.exp(a_re * dt)
    ab_re = mag * jnp.cos(a_im * dt)
    ab_im = mag * jnp.sin(a_im * dt)
    den = a_re * a_re + a_im * a_im
    nr = ab_re - 1.0
    q_re = (nr * a_re + ab_im * a_im) / den
    q_im = (ab_im * a_re - nr * a_im) / den
    abre_ref[...] = ab_re
    abim_ref[...] = ab_im
    b_re = bre_ref[...]
    b_im = bim_ref[...]
    bbre_ref[...] = q_re[:, None, :] * b_re - q_im[:, None, :] * b_im
    bbim_ref[...] = q_re[:, None, :] * b_im + q_im[:, None, :] * b_re


def _s5_discretize(a_re, a_im, log_dt, b_re, b_im):
    m, n = a_re.shape
    p = b_re.shape[1]
    return pl.pallas_call(
        _s5_disc_kernel,
        out_shape=[jax.ShapeDtypeStruct((m, n), F32), jax.ShapeDtypeStruct((m, n), F32),
                   jax.ShapeDtypeStruct((m, p, n), F32), jax.ShapeDtypeStruct((m, p, n), F32)],
        name="s5_discretize",
    )(a_re, a_im, log_dt, b_re, b_im)


def _s5_scan_kernel(u_ref, wbre_ref, wbim_ref, abre_ref, abim_ref, wc_ref, h0_ref,
                    y_ref, hfin_ref, bure_ref, buim_ref, hre_ref, him_ref, *, nb, lane_w):
    d = pl.program_id(0)
    c = pl.program_id(1)
    nc = pl.num_programs(1)
    rows = u_ref.shape[0]

    @pl.when(c == 0)
    def _():
        hre_ref[...] = h0_ref[0, 0]
        him_ref[...] = h0_ref[0, 1]

    ub = u_ref[...].astype(BF16)
    bure_ref[...] = jnp.dot(ub, wbre_ref[0], preferred_element_type=F32)
    buim_ref[...] = jnp.dot(ub, wbim_ref[0], preferred_element_type=F32)

    fwd = d == 0
    if nb >= SUBLANES:
        steps = rows // nb
        for lc in range(S5_STATE // lane_w):
            ls = slice(lc * lane_w, (lc + 1) * lane_w)
            ar = jnp.broadcast_to(abre_ref[0, :, ls], (nb, lane_w))
            ai = jnp.broadcast_to(abim_ref[0, :, ls], (nb, lane_w))

            def body(s, carry, ls=ls, ar=ar, ai=ai):
                hr, hi = carry
                t = jnp.where(fwd, s, steps - 1 - s)
                r0 = pl.multiple_of(t * nb, nb)
                br = bure_ref[pl.ds(r0, nb), ls]
                bi = buim_ref[pl.ds(r0, nb), ls]
                nr = ar * hr - ai * hi + br
                ni = ar * hi + ai * hr + bi
                bure_ref[pl.ds(r0, nb), ls] = nr
                buim_ref[pl.ds(r0, nb), ls] = ni
                return nr, ni

            hr, hi = lax.fori_loop(0, steps, body, (hre_ref[:, ls], him_ref[:, ls]))
            hre_ref[:, ls] = hr
            him_ref[:, ls] = hi
    else:
        per = SUBLANES // nb
        groups = rows // SUBLANES
        row_id = lax.broadcasted_iota(jnp.int32, (SUBLANES, lane_w), 0) // nb
        for lc in range(S5_STATE // lane_w):
            ls = slice(lc * lane_w, (lc + 1) * lane_w)
            ar = jnp.broadcast_to(abre_ref[0, :, ls], (SUBLANES, lane_w))
            ai = jnp.broadcast_to(abim_ref[0, :, ls], (SUBLANES, lane_w))
            h0r = hre_ref[:, ls]
            h0i = him_ref[:, ls]

            def body(gidx, carry, ls=ls, ar=ar, ai=ai):
                tr, ti = carry
                g = jnp.where(fwd, gidx, groups - 1 - gidx)
                r0 = pl.multiple_of(g * SUBLANES, SUBLANES)
                br = bure_ref[pl.ds(r0, SUBLANES), ls]
                bi = buim_ref[pl.ds(r0, SUBLANES), ls]
                for k in range(per):
                    pr_f = pltpu.roll(tr, nb, 0)
                    pi_f = pltpu.roll(ti, nb, 0)
                    pr_b = pltpu.roll(tr, SUBLANES - nb, 0)
                    pi_b = pltpu.roll(ti, SUBLANES - nb, 0)
                    pr = jnp.where(fwd, pr_f, pr_b)
                    pi = jnp.where(fwd, pi_f, pi_b)
                    nr = ar * pr - ai * pi + br
                    ni = ar * pi + ai * pr + bi
                    tgt = jnp.where(fwd, k, per - 1 - k)
                    sel = row_id == tgt
                    tr = jnp.where(sel, nr, tr)
                    ti = jnp.where(sel, ni, ti)
                bure_ref[pl.ds(r0, SUBLANES), ls] = tr
                buim_ref[pl.ds(r0, SUBLANES), ls] = ti
                return tr, ti

            tr, ti = lax.fori_loop(0, groups, body, (h0r, h0i))
            hre_ref[:, ls] = tr
            him_ref[:, ls] = ti

    hcat = jnp.concatenate([bure_ref[...].astype(BF16), buim_ref[...].astype(BF16)], axis=1)
    y_ref[0] = jnp.dot(hcat, wc_ref[0], preferred_element_type=F32)

    @pl.when(c == nc - 1)
    def _():
        hfin_ref[0, 0] = hre_ref[...]
        hfin_ref[0, 1] = him_ref[...]


def _s5_scan(u, wb_re, wb_im, ab_re, ab_im, wc, h0, nb, chunk_rows=512):
    rows = u.shape[0]
    if nb < SUBLANES:
        h0 = jnp.tile(h0, (1, 1, SUBLANES // nb, 1))
    srows = max(nb, SUBLANES)
    cr = min(chunk_rows, rows)
    nc = rows // cr
    s = S5_STATE
    lane_w = 256 if nb >= SUBLANES else 512
    chunk = lambda d, c: c + d * (nc - 1 - 2 * c)
    kern = functools.partial(_s5_scan_kernel, nb=nb, lane_w=lane_w)
    y, hfin = pl.pallas_call(
        kern,
        grid=(2, nc),
        in_specs=[pl.BlockSpec((cr, D_S5), lambda d, c: (chunk(d, c), 0)),
                  pl.BlockSpec((1, D_S5, s), lambda d, c: (d, 0, 0)),
                  pl.BlockSpec((1, D_S5, s), lambda d, c: (d, 0, 0)),
                  pl.BlockSpec((1, 1, s), lambda d, c: (d, 0, 0)),
                  pl.BlockSpec((1, 1, s), lambda d, c: (d, 0, 0)),
                  pl.BlockSpec((1, 2 * s, D_S5), lambda d, c: (d, 0, 0)),
                  pl.BlockSpec((1, 2, srows, s), lambda d, c: (d, 0, 0, 0))],
        out_specs=[pl.BlockSpec((1, cr, D_S5), lambda d, c: (d, chunk(d, c), 0)),
                   pl.BlockSpec((1, 2, srows, s), lambda d, c: (d, 0, 0, 0))],
        out_shape=[jax.ShapeDtypeStruct((2, rows, D_S5), F32),
                   jax.ShapeDtypeStruct((2, 2, srows, s), F32)],
        scratch_shapes=[pltpu.VMEM((cr, s), F32), pltpu.VMEM((cr, s), F32),
                        pltpu.VMEM((srows, s), F32), pltpu.VMEM((srows, s), F32)],
        compiler_params=_params(("arbitrary", "arbitrary")),
        name="s5_scan",
    )(u, wb_re, wb_im, ab_re, ab_im, wc, h0)
    if nb < SUBLANES:
        hfin = jnp.stack([hfin[0, :, srows - nb:], hfin[1, :, :nb]])
    return y, hfin


def _rwkv_prep_kernel(prev_ref, cur_ref, next_ref, mu_ref, ones_ref, g2_ref, kk_ref, ka_ref, rk_ref,
                      w0_ref, w2_ref, a0_ref, a2_ref,
                      r_ref, v_ref, nkk_ref, bonus_ref, g_ref, w_ref, kd_ref, b_ref, *, nb, seq, grid_shift):
    tm = cur_ref.shape[0]
    i = pl.program_id(0)
    cur = cur_ref[...]
    prv = prev_ref[...]
    nxt = next_ref[...]
    row = lax.broadcasted_iota(jnp.int32, (tm, RWKV_COLS), 0)
    lane = lax.broadcasted_iota(jnp.int32, (tm, RWKV_COLS), 1)
    t = (i * tm + row) // nb

    def rows_before(s):
        if s == tm:
            return prv
        return jnp.where(row < s, pltpu.roll(prv, s, 0), pltpu.roll(cur, s, 0))

    def rows_after(s):
        if s == tm:
            return nxt
        return jnp.where(row >= tm - s, pltpu.roll(nxt, tm - s, 0), pltpu.roll(cur, tm - s, 0))

    if grid_shift:
        tw = t % GRID_W
        left = jnp.where(tw == 0, 0.0, rows_before(nb))
        right = jnp.where(tw == GRID_W - 1, 0.0, rows_after(nb))
        up = jnp.where(t < GRID_W, 0.0, rows_before(nb * GRID_W))
        down = jnp.where(t >= seq - GRID_W, 0.0, rows_after(nb * GRID_W))
        m4 = lane % 4
        sh = jnp.where(m4 == 0, left, jnp.where(m4 == 1, right, jnp.where(m4 == 2, up, down)))
    else:
        before = jnp.where(t == 0, 0.0, rows_before(nb))
        after = jnp.where(t == seq - 1, 0.0, rows_after(nb))
        sh = jnp.where(lane % 2 == 0, before, after)

    z = cur + (sh - cur) * mu_ref[...]
    r = z[:, 0:D_RWKV]
    k = z[:, D_RWKV:2 * D_RWKV]
    v = z[:, 2 * D_RWKV:3 * D_RWKV]
    xw = z[:, 3 * D_RWKV:3 * D_RWKV + LORA_W]
    xa = z[:, 3 * D_RWKV + LORA_W:3 * D_RWKV + LORA_W + LORA_A]
    xg = z[:, 3 * D_RWKV + LORA_W + LORA_A:]
    ones = ones_ref[...]

    g_ref[...] = _dot(_sigmoid(xg), g2_ref[...])
    kk = k * kk_ref[...]
    nrm = jnp.sqrt(_dot_split(kk * kk, ones))
    kk = kk / jnp.maximum(nrm, 1e-12)
    r_ref[...] = r
    v_ref[...] = v
    nkk_ref[...] = -kk
    bonus_ref[...] = _dot_split(r * k * rk_ref[...], ones) * v
    txw = jnp.tanh(xw)
    for d in range(2):
        zw = -(w0_ref[d] + _dot(txw, w2_ref[d]))
        softplus = jnp.maximum(zw, 0.0) + jnp.log(1.0 + jnp.exp(-jnp.abs(zw)))
        w_log = -softplus - 0.5
        w_ref[d] = jnp.exp(-jnp.exp(w_log))
        a = _sigmoid(a0_ref[d] + _dot(xa, a2_ref[d]))
        kd_ref[d] = k * (1.0 + (a - 1.0) * ka_ref[...])
        b_ref[d] = kk * a


def _rwkv_prep(zr, p, nb, seq, grid_shift, tm=128):
    rows = zr.shape[0]
    nt = rows // tm
    if grid_shift:
        assert tm == nb * GRID_W, "one tile must be one grid row of the latent grid"
    else:
        assert tm % nb == 0 and tm >= nb
    kern = functools.partial(_rwkv_prep_kernel, nb=nb, seq=seq, grid_shift=grid_shift)
    c = D_RWKV
    vec = lambda n: pl.BlockSpec((1, n), lambda i: (0, 0))
    mat = lambda a, b: pl.BlockSpec((a, b), lambda i: (0, 0))
    row_out = pl.BlockSpec((tm, c), lambda i: (i, 0))
    dir_out = pl.BlockSpec((2, tm, c), lambda i: (0, i, 0))
    head_ones = jnp.kron(jnp.eye(H_RWKV, dtype=F32), jnp.ones((RWKV_HEAD, RWKV_HEAD), F32)).astype(BF16)
    return pl.pallas_call(
        kern,
        grid=(nt,),
        in_specs=[pl.BlockSpec((tm, RWKV_COLS), lambda i: (jnp.maximum(i - 1, 0), 0)),
                  pl.BlockSpec((tm, RWKV_COLS), lambda i: (i, 0)),
                  pl.BlockSpec((tm, RWKV_COLS), lambda i: (jnp.minimum(i + 1, nt - 1), 0)),
                  vec(RWKV_COLS), mat(c, c), mat(LORA_G, c), vec(c), vec(c), vec(c),
                  pl.BlockSpec((2, 1, c), lambda i: (0, 0, 0)),
                  pl.BlockSpec((2, LORA_W, c), lambda i: (0, 0, 0)),
                  pl.BlockSpec((2, 1, c), lambda i: (0, 0, 0)),
                  pl.BlockSpec((2, LORA_A, c), lambda i: (0, 0, 0))],
        out_specs=[row_out] * 5 + [dir_out] * 3,
        out_shape=[jax.ShapeDtypeStruct((rows, c), F32)] * 5 + [jax.ShapeDtypeStruct((2, rows, c), F32)] * 3,
        compiler_params=_params(("arbitrary",)),
        name="rwkv_prep",
    )(zr, zr, zr, p['rw_mu'].reshape(1, -1), head_ones, p['rw_g2'], p['rw_k_k'].reshape(1, c),
      p['rw_k_a'].reshape(1, c), p['rw_r_k'].reshape(1, c), p['rw_w0'].reshape(2, 1, c), p['rw_w2'],
      p['rw_a0'].reshape(2, 1, c), p['rw_a2'])


def _rwkv_scan_kernel(a_ref, w_ref, b_ref, k_ref, r_ref, v_ref, s0_ref, y_ref, sfin_ref, s_ref, *, nacc):
    c = pl.program_id(1)
    nc = pl.num_programs(1)
    tc = a_ref.shape[1]
    nj = RWKV_HEAD

    @pl.when(c == 0)
    def _():
        s_ref[...] = s0_ref[0]

    def step(t, carry):
        acc = [None] * nacc
        for j in range(nj):
            term = s_ref[j] * a_ref[0, t, pl.ds(j, 1), :]
            acc[j % nacc] = term if acc[j % nacc] is None else acc[j % nacc] + term
        sa = functools.reduce(lambda x, y: x + y, acc)
        vv = v_ref[0, t]
        acc = [None] * nacc
        for j in range(nj):
            s_new = (s_ref[j] * w_ref[0, t, pl.ds(j, 1), :]
                     + sa * b_ref[0, t, pl.ds(j, 1), :]
                     + vv * k_ref[0, t, pl.ds(j, 1), :])
            s_ref[j] = s_new
            term = s_new * r_ref[0, t, pl.ds(j, 1), :]
            acc[j % nacc] = term if acc[j % nacc] is None else acc[j % nacc] + term
        y_ref[0, t] = functools.reduce(lambda x, y: x + y, acc)
        return carry

    lax.fori_loop(0, tc, step, 0)

    @pl.when(c == nc - 1)
    def _():
        sfin_ref[0] = s_ref[...]


def _rwkv_scan(a, w, b, k, r, v, s0, tc):
    nd, t, _, nl = a.shape
    iv = v.shape[2]
    nacc = 2 if iv >= 64 else 4
    kspec = pl.BlockSpec((1, tc, RWKV_HEAD, nl), lambda d, c: (d, c, 0, 0))
    vspec = pl.BlockSpec((1, tc, iv, nl), lambda d, c: (d, c, 0, 0))
    sspec = pl.BlockSpec((1, RWKV_HEAD, iv, nl), lambda d, c: (d, 0, 0, 0))
    return pl.pallas_call(
        functools.partial(_rwkv_scan_kernel, nacc=nacc),
        grid=(nd, t // tc),
        in_specs=[kspec] * 5 + [vspec, sspec],
        out_specs=[vspec, sspec],
        out_shape=[jax.ShapeDtypeStruct((nd, t, iv, nl), F32),
                   jax.ShapeDtypeStruct((nd, RWKV_HEAD, iv, nl), F32)],
        scratch_shapes=[pltpu.VMEM((RWKV_HEAD, iv, nl), F32)],
        compiler_params=_params(("arbitrary", "arbitrary")),
        name="rwkv_scan",
    )(a, w, b, k, r, v, s0)


def _rwkv_mix_ctx(prep, nb, seq):
    r, v, nkk, _, _, w, kd, bb = prep
    h = H_RWKV

    def to_lanes(x):
        return x.reshape(seq, nb, h, RWKV_HEAD).transpose(0, 3, 1, 2).reshape(seq, RWKV_HEAD, nb * h)

    def both(shared=None, per_dir=None):
        if shared is not None:
            x = to_lanes(shared)
            return jnp.stack([x, x[::-1]])
        return jnp.stack([to_lanes(per_dir[0]), to_lanes(per_dir[1])[::-1]])

    return (both(shared=nkk), both(per_dir=w), both(per_dir=bb), both(per_dir=kd), both(shared=r),
            both(shared=v))


def _rwkv_unmix_ctx(y, nb, seq):
    y = y[0] + y[1][::-1]
    return y.reshape(seq, RWKV_HEAD, nb, H_RWKV).transpose(0, 2, 3, 1).reshape(seq * nb, D_RWKV)


SMP_REP = 4


def _rwkv_mix_smp(prep, nb, seq):
    r, v, nkk, _, _, w, kd, bb = prep
    h = H_RWKV
    ni = 2 * nb * h
    assert ni * SMP_REP == LANES

    def inst(x0, x1):
        x0 = x0.reshape(seq, nb, h, RWKV_HEAD)
        x1 = x1.reshape(seq, nb, h, RWKV_HEAD)[::-1]
        return jnp.stack([x0, x1], axis=1).reshape(seq, ni, RWKV_HEAD)

    def key(x0, x1):
        x = inst(x0, x1).transpose(0, 2, 1)
        return jnp.tile(x, (1, 1, SMP_REP))[None]

    vi = inst(v, v).reshape(seq, ni, SMP_REP, RWKV_HEAD // SMP_REP)
    vi = vi.transpose(0, 3, 2, 1).reshape(seq, RWKV_HEAD // SMP_REP, LANES)[None]
    return (key(nkk, nkk), key(w[0], w[1]), key(bb[0], bb[1]), key(kd[0], kd[1]), key(r, r), vi)


def _rwkv_unmix_smp(y, nb, seq):
    h = H_RWKV
    y = y[0].reshape(seq, RWKV_HEAD // SMP_REP, SMP_REP, 2, nb, h)
    y = y.transpose(3, 0, 4, 5, 2, 1).reshape(2, seq, nb, h, RWKV_HEAD)
    y = y[0] + y[1][::-1]
    return y.reshape(seq * nb, D_RWKV)


def _rwkv_state_smp(state_l, nb):
    s = state_l.reshape(nb, 2, H_RWKV, SMP_REP, RWKV_HEAD // SMP_REP, RWKV_HEAD)
    s = s.transpose(5, 4, 3, 1, 0, 2)
    return s.reshape(1, RWKV_HEAD, RWKV_HEAD // SMP_REP, LANES)


def _rwkv_state_out_ctx(sfin, nb):
    s = sfin.reshape(2, RWKV_HEAD, RWKV_HEAD, nb, H_RWKV)
    return s.transpose(3, 0, 4, 2, 1)


def _mix_out_kernel(x_ref, pat_ref, u_ref, y5_ref, gs_ref, gr_ref, yrw_ref, bonus_ref, g_ref,
                    d_ref, wglu_ref, ones_ref, lnxw_ref, lnxb_ref, wups_ref, wupr_ref, wout_ref,
                    ln1g_ref, ln1b_ref, o_ref):
    y5 = u_ref[...] * d_ref[...] + y5_ref[0] + y5_ref[1]
    y5 = jax.nn.gelu(y5)
    y5 = y5 * _sigmoid(_dot(y5, wglu_ref[...]))
    ones = ones_ref[...]
    yr = yrw_ref[...]
    inv_n = 1.0 / RWKV_HEAD
    mu = _dot_split(yr, ones) * inv_n
    yc = yr - mu
    var = _dot_split(yc * yc, ones) * inv_n
    yr = yc * lax.rsqrt(var + GN_EPS) * lnxw_ref[...] + lnxb_ref[...]
    yr = (yr + bonus_ref[...]) * g_ref[...]
    merged = (_sigmoid(gs_ref[...]) * _dot(y5, wups_ref[...])
              + _sigmoid(gr_ref[...]) * _dot(yr, wupr_ref[...]))
    mix = _dot(merged, wout_ref[...])
    x3 = _modulate(mix, jnp.zeros_like(pat_ref[0, 2]), pat_ref[0, 2] - 1.0)
    o_ref[...] = _layer_norm(ALPHA * x_ref[...] + x3, ln1g_ref[...], ln1b_ref[...])


def _mix_out(x, pat, u, y5, gs, gr, yrw, bonus, g, p, rows_per_trunk, tm=256):
    rows = x.shape[0]
    d = D_MODEL
    c = D_RWKV
    tpt = rows_per_trunk // tm
    rowblk = lambda n: pl.BlockSpec((tm, n), lambda i: (i, 0))
    vec = lambda n: pl.BlockSpec((1, n), lambda i: (0, 0))
    mat = lambda a, b: pl.BlockSpec((a, b), lambda i: (0, 0))
    head_ones = jnp.kron(jnp.eye(H_RWKV, dtype=F32), jnp.ones((RWKV_HEAD, RWKV_HEAD), F32)).astype(BF16)
    return pl.pallas_call(
        _mix_out_kernel,
        grid=(rows // tm,),
        in_specs=[rowblk(d),
                  pl.BlockSpec((1, 6, SUBLANES, d), lambda i: (i // tpt, 0, 0, 0)),
                  rowblk(c), pl.BlockSpec((2, tm, c), lambda i: (0, i, 0)),
                  rowblk(d), rowblk(d), rowblk(c), rowblk(c), rowblk(c),
                  vec(c), mat(c, c), mat(c, c), vec(c), vec(c), mat(c, d), mat(c, d), mat(d, d),
                  vec(d), vec(d)],
        out_specs=rowblk(d),
        out_shape=jax.ShapeDtypeStruct((rows, d), F32),
        compiler_params=_params(("arbitrary",)),
        name="mix_out",
    )(x, pat, u, y5, gs, gr, yrw, bonus, g,
      p['s5_d'].reshape(1, c), p['s5_w_glu'].astype(BF16), head_ones,
      p['rw_lnx_w'].reshape(1, c), p['rw_lnx_b'].reshape(1, c),
      p['w_up_s5'].astype(BF16), p['w_up_rwkv'].astype(BF16), p['w_out'].astype(BF16),
      p['ln1_g'].reshape(1, d), p['ln1_b'].reshape(1, d))


def _first_max(x, lane, valid):
    xm = jnp.where(valid, x, -jnp.inf)
    m = jnp.max(xm, -1, keepdims=True)
    idx = jnp.min(jnp.where(xm == m, lane, float(N_EXPERTS)), -1, keepdims=True)
    return m, idx


def _moe_kernel(x_ref, pat_ref, wr_ref, br_ref, wg_ref, wu_ref, wd_ref, ln2g_ref, ln2b_ref, o_ref,
                hb_ref, comb_ref, acc_ref):
    e = pl.program_id(1)
    ne = pl.num_programs(1)
    tm = x_ref.shape[0]

    @pl.when(e == 0)
    def _():
        h = _modulate(x_ref[...], pat_ref[0, 3], pat_ref[0, 4])
        hb_ref[...] = h.astype(BF16)
        wr = wr_ref[...]
        w_hi = wr.astype(BF16)
        w_lo = (wr - w_hi.astype(F32)).astype(BF16)
        h_hi = h.astype(BF16)
        h_lo = (h - h_hi.astype(F32)).astype(BF16)
        logits = (jnp.dot(h_hi, w_hi, preferred_element_type=F32)
                  + jnp.dot(h_hi, w_lo, preferred_element_type=F32)
                  + jnp.dot(h_lo, w_hi, preferred_element_type=F32)) + br_ref[...]
        logits = logits - jnp.max(logits, -1, keepdims=True)
        ex = jnp.exp(logits)
        probs = ex / jnp.sum(ex, -1, keepdims=True)
        lane_i = lax.broadcasted_iota(jnp.int32, (tm, N_EXPERTS), 1)
        lane = lane_i.astype(F32)
        grp = (lane_i // EXPERTS_PER_GROUP).astype(F32)
        best_score = jnp.full((tm, 1), -jnp.inf, F32)
        best_grp = jnp.zeros((tm, 1), F32)
        for gi in range(N_GROUPS):
            in_g = grp == float(gi)
            m1, i1 = _first_max(probs, lane, in_g)
            m2, _ = _first_max(probs, lane, in_g & (lane != i1))
            score = m1 + m2
            better = score > best_score
            best_score = jnp.where(better, score, best_score)
            best_grp = jnp.where(better, float(gi), best_grp)
        in_best = grp == best_grp
        m1, i1 = _first_max(probs, lane, in_best)
        m2, i2 = _first_max(probs, lane, in_best & (lane != i1))
        tot = m1 + m2
        comb_ref[...] = jnp.where(lane == i1, m1 / tot, 0.0) + jnp.where(lane == i2, m2 / tot, 0.0)
        acc_ref[...] = jnp.zeros_like(acc_ref)

    hb = hb_ref[...]
    lane = lax.broadcasted_iota(jnp.int32, (tm, N_EXPERTS), 1)
    ce = jnp.sum(jnp.where(lane == e, comb_ref[...], 0.0), -1, keepdims=True)
    hid = _silu(jnp.dot(hb, wg_ref[0], preferred_element_type=F32)) * jnp.dot(hb, wu_ref[0],
                                                                             preferred_element_type=F32)
    acc_ref[...] += jnp.dot((hid * ce).astype(BF16), wd_ref[0], preferred_element_type=F32)

    @pl.when(e == ne - 1)
    def _():
        ffn = _modulate(acc_ref[...], jnp.zeros_like(pat_ref[0, 5]), pat_ref[0, 5] - 1.0)
        o_ref[...] = _layer_norm(ALPHA * x_ref[...] + ffn, ln2g_ref[...], ln2b_ref[...])


def _moe(x, pat, w_router, b_router, wg, wu, wd, ln2_g, ln2_b, rows_per_trunk, tm=512):
    rows = x.shape[0]
    d = D_MODEL
    tpt = rows_per_trunk // tm
    rowblk = pl.BlockSpec((tm, d), lambda i, e: (i, 0))
    vec = lambda n: pl.BlockSpec((1, n), lambda i, e: (0, 0))
    return pl.pallas_call(
        _moe_kernel,
        grid=(rows // tm, N_EXPERTS),
        in_specs=[rowblk,
                  pl.BlockSpec((1, 6, SUBLANES, d), lambda i, e: (i // tpt, 0, 0, 0)),
                  pl.BlockSpec((d, N_EXPERTS), lambda i, e: (0, 0)), vec(N_EXPERTS),
                  pl.BlockSpec((1, d, D_EXPERT), lambda i, e: (e, 0, 0)),
                  pl.BlockSpec((1, d, D_EXPERT), lambda i, e: (e, 0, 0)),
                  pl.BlockSpec((1, D_EXPERT, d), lambda i, e: (e, 0, 0)),
                  vec(d), vec(d)],
        out_specs=rowblk,
        out_shape=jax.ShapeDtypeStruct((rows, d), F32),
        scratch_shapes=[pltpu.VMEM((tm, d), BF16), pltpu.VMEM((tm, N_EXPERTS), F32),
                        pltpu.VMEM((tm, d), F32)],
        compiler_params=_params(("arbitrary", "arbitrary")),
        name="moe",
    )(x, pat, w_router, b_router.reshape(1, N_EXPERTS), wg, wu, wd,
      ln2_g.reshape(1, d), ln2_b.reshape(1, d))


def _block_diag_in(bb):
    nd, g, p, n = bb.shape
    eye = jnp.eye(g, dtype=bb.dtype)
    return jnp.einsum('dgpn,gh->dgphn', bb, eye).reshape(nd, g * p, g * n)


def _block_diag_out(cc):
    nd, g, p, n = cc.shape
    eye = jnp.eye(g, dtype=cc.dtype)
    return jnp.einsum('dgpn,gh->dgnhp', cc, eye).reshape(nd, g * n, g * p)


def kernel(x_prompt, x_sample, state_s5, state_rwkv, c, c_ctx, w_ada, b_ada, w_in, s5_a_re, s5_a_im, s5_log_dt, s5_b_re, s5_b_im, s5_c_re, s5_c_im, s5_d, s5_w_glu, rw_mu, rw_w0, rw_w2, rw_a0, rw_a2, rw_g2, rw_k_k, rw_k_a, rw_r_k, rw_lnx_w, rw_lnx_b, w_up_s5, w_up_rwkv, w_out, ln1_g, ln1_b, ln2_g, ln2_b, w_router, b_router, w_exp_gate, w_exp_up, w_exp_down):
    nbc, tc_len, d = x_prompt.shape
    nbs, ts_len, _ = x_sample.shape
    nl = w_ada.shape[0]
    rc = nbc * tc_len
    rs = nbs * ts_len
    assert rc == rs, "both trunks are processed as equal halves of one row-major token matrix"
    assert SUBLANES % nbs == 0 and nbc % SUBLANES == 0

    x = jnp.concatenate([x_prompt.transpose(1, 0, 2).reshape(rc, d),
                         x_sample.transpose(1, 0, 2).reshape(rs, d)], axis=0)

    cond = jnp.concatenate([c_ctx[None], c], axis=0)
    cond8 = jnp.zeros((SUBLANES, d), F32).at[:cond.shape[0]].set(cond)
    mod = _ada_mod(cond8, w_ada, b_ada).reshape(nl, SUBLANES, 6, d)
    ctx_rows = jnp.zeros((SUBLANES,), jnp.int32)
    smp_rows = 1 + jnp.arange(SUBLANES, dtype=jnp.int32) % nbs
    pat_idx = jnp.stack([ctx_rows, smp_rows])
    pats = mod[:, pat_idx]
    pats = pats.transpose(0, 1, 3, 2, 4)

    m = nl * 2 * G_S5
    ab_re, ab_im, bb_re, bb_im = _s5_discretize(
        s5_a_re.reshape(m, N_S5), s5_a_im.reshape(m, N_S5), s5_log_dt.reshape(m, 1),
        s5_b_re.reshape(m, N_S5, S5_GROUP).transpose(0, 2, 1),
        s5_b_im.reshape(m, N_S5, S5_GROUP).transpose(0, 2, 1))
    ab_re = ab_re.reshape(nl, 2, 1, S5_STATE)
    ab_im = ab_im.reshape(nl, 2, 1, S5_STATE)
    bb_re = bb_re.reshape(nl, 2, G_S5, S5_GROUP, N_S5)
    bb_im = bb_im.reshape(nl, 2, G_S5, S5_GROUP, N_S5)

    zero_s5 = jnp.zeros((2, 2, nbc, S5_STATE), F32)
    zero_rw = jnp.zeros((2, RWKV_HEAD, RWKV_HEAD, nbc * H_RWKV), F32)
    s5_out, rw_out = [], []
    for l in range(nl):
        p = dict(s5_d=s5_d[l], s5_w_glu=s5_w_glu[l], rw_mu=rw_mu[l], rw_w0=rw_w0[l], rw_w2=rw_w2[l],
                 rw_a0=rw_a0[l], rw_a2=rw_a2[l], rw_g2=rw_g2[l], rw_k_k=rw_k_k[l], rw_k_a=rw_k_a[l],
                 rw_r_k=rw_r_k[l], rw_lnx_w=rw_lnx_w[l], rw_lnx_b=rw_lnx_b[l], w_up_s5=w_up_s5[l],
                 w_up_rwkv=w_up_rwkv[l], w_out=w_out[l], ln1_g=ln1_g[l], ln1_b=ln1_b[l])
        pat = pats[l]
        u, zr, gs, gr = _inproj(x, pat, w_in[l], rc)

        wb_re = _block_diag_in(bb_re[l]).astype(BF16)
        wb_im = _block_diag_in(bb_im[l]).astype(BF16)
        wc = jnp.concatenate([_block_diag_out(s5_c_re[l]), -_block_diag_out(s5_c_im[l])], axis=1).astype(BF16)
        y5c, hfin = _s5_scan(u[:rc], wb_re, wb_im, ab_re[l], ab_im[l], wc, zero_s5, nbc)
        h0s = state_s5[:, l].reshape(nbs, 2, 2, S5_STATE).transpose(1, 2, 0, 3)
        y5s, _ = _s5_scan(u[rc:], wb_re, wb_im, ab_re[l], ab_im[l], wc, h0s, nbs)
        y5 = jnp.concatenate([y5c, y5s], axis=1)
        s5_out.append(hfin.transpose(2, 0, 1, 3).reshape(nbc, 2, 2, G_S5, N_S5))

        prep_c = _rwkv_prep(zr[:rc], p, nbc, tc_len, grid_shift=False)
        prep_s = _rwkv_prep(zr[rc:], p, nbs, ts_len, grid_shift=True)
        yc, sfin = _rwkv_scan(*_rwkv_mix_ctx(prep_c, nbc, tc_len), zero_rw, tc=min(32, tc_len))
        ys, _ = _rwkv_scan(*_rwkv_mix_smp(prep_s, nbs, ts_len), _rwkv_state_smp(state_rwkv[:, l], nbs),
                           tc=min(64, ts_len))
        yrw = jnp.concatenate([_rwkv_unmix_ctx(yc, nbc, tc_len), _rwkv_unmix_smp(ys, nbs, ts_len)], axis=0)
        bonus = jnp.concatenate([prep_c[3], prep_s[3]], axis=0)
        g = jnp.concatenate([prep_c[4], prep_s[4]], axis=0)
        rw_out.append(_rwkv_state_out_ctx(sfin, nbc))

        x = _mix_out(x, pat, u, y5, gs, gr, yrw, bonus, g, p, rc)
        x = _moe(x, pat, w_router, b_router, w_exp_gate[l].astype(BF16), w_exp_up[l].astype(BF16),
                 w_exp_down[l].astype(BF16), ln2_g[l], ln2_b[l], rc)

    y_prompt = x[:rc].reshape(tc_len, nbc, d).transpose(1, 0, 2)
    y_sample = x[rc:].reshape(ts_len, nbs, d).transpose(1, 0, 2)
    return (y_prompt, y_sample, jnp.stack(s5_out, 1), jnp.stack(rw_out, 1))
```

```python
import functools
import math

import jax
import jax.numpy as jnp
from jax import lax
from jax.experimental import pallas as pl
from jax.experimental.pallas import tpu as pltpu

D_MODEL = 1024
DEPTH = 2
GRID_W = 64
D_S5 = 512
S5_GROUP = 16
G_S5 = 32
N_S5 = 64
S5_STATE = G_S5 * N_S5
D_RWKV = 512
RWKV_HEAD = 64
H_RWKV = 8
LORA_W = 64
LORA_A = 64
LORA_G = 128
RWKV_COLS = 3 * D_RWKV + LORA_W + LORA_A + LORA_G
N_EXPERTS = 16
N_GROUPS = 4
EXPERTS_PER_GROUP = 4
D_EXPERT = 512
ALPHA = (2 * DEPTH) ** 0.25
LN_EPS = 1e-5
GN_EPS = 64e-5

SUBLANES = 8
LANES = 128
VMEM_LIMIT = 56 * 1024 * 1024

F32 = jnp.float32
BF16 = jnp.bfloat16


def _params(sem):
    return pltpu.CompilerParams(dimension_semantics=sem, vmem_limit_bytes=VMEM_LIMIT)


def _dot(a, b):
    return jnp.dot(a.astype(BF16), b.astype(BF16), preferred_element_type=F32)


def _dot_split(x, w_exact):
    hi = x.astype(BF16)
    lo = (x - hi.astype(F32)).astype(BF16)
    return (jnp.dot(hi, w_exact, preferred_element_type=F32)
            + jnp.dot(lo, w_exact, preferred_element_type=F32))


def _sigmoid(x):
    return 1.0 / (1.0 + jnp.exp(-x))


def _silu(x):
    return x * _sigmoid(x)


def _layer_norm(x, g, b):
    mu = jnp.mean(x, -1, keepdims=True)
    xc = x - mu
    var = jnp.mean(xc * xc, -1, keepdims=True)
    return xc * lax.rsqrt(var + LN_EPS) * g + b


def _modulate(x, shift8, scale8):
    rows, d = x.shape
    x3 = x.reshape(rows // SUBLANES, SUBLANES, d)
    return (x3 * (1.0 + scale8)[None] + shift8[None]).reshape(rows, d)


def _ada_kernel(cond_ref, w_ref, b_ref, o_ref):
    c = cond_ref[...]
    o_ref[0] = _dot(_silu(c), w_ref[0]) + b_ref[0]


def _ada_mod(cond8, w_ada, b_ada):
    nl = w_ada.shape[0]
    d = D_MODEL
    return pl.pallas_call(
        _ada_kernel,
        grid=(nl, 6),
        in_specs=[pl.BlockSpec((SUBLANES, d), lambda l, k: (0, 0)),
                  pl.BlockSpec((1, d, d), lambda l, k: (l, 0, k)),
                  pl.BlockSpec((1, 1, d), lambda l, k: (l, 0, k))],
        out_specs=pl.BlockSpec((1, SUBLANES, d), lambda l, k: (l, 0, k)),
        out_shape=jax.ShapeDtypeStruct((nl, SUBLANES, 6 * d), F32),
        compiler_params=_params(("arbitrary", "arbitrary")),
        name="ada_mod",
    )(cond8, w_ada, b_ada.reshape(nl, 1, 6 * d))


def _inproj_kernel(x_ref, pat_ref, wu_ref, wzr_ref, wgs_ref, wgr_ref, u_ref, zr_ref, gs_ref, gr_ref):
    h = _modulate(x_ref[...], pat_ref[0, 0], pat_ref[0, 1]).astype(BF16)
    u_ref[...] = jnp.dot(h, wu_ref[...], preferred_element_type=F32)
    zr_ref[...] = jnp.dot(h, wzr_ref[...], preferred_element_type=F32)
    gs_ref[...] = jnp.dot(h, wgs_ref[...], preferred_element_type=F32)
    gr_ref[...] = jnp.dot(h, wgr_ref[...], preferred_element_type=F32)


def _inproj(x, pat, w_in_l, rows_per_trunk, tm=256):
    rows = x.shape[0]
    d = D_MODEL
    tpt = rows_per_trunk // tm
    wu = w_in_l[:, :D_S5].astype(BF16)
    wzr = w_in_l[:, D_S5:D_S5 + RWKV_COLS].astype(BF16)
    wgs = w_in_l[:, D_S5 + RWKV_COLS:D_S5 + RWKV_COLS + d].astype(BF16)
    wgr = w_in_l[:, D_S5 + RWKV_COLS + d:].astype(BF16)
    full = lambda n: pl.BlockSpec((d, n), lambda i: (0, 0))
    rowblk = lambda n: pl.BlockSpec((tm, n), lambda i: (i, 0))
    return pl.pallas_call(
        _inproj_kernel,
        grid=(rows // tm,),
        in_specs=[rowblk(d),
                  pl.BlockSpec((1, 6, SUBLANES, d), lambda i: (i // tpt, 0, 0, 0)),
                  full(D_S5), full(RWKV_COLS), full(d), full(d)],
        out_specs=[rowblk(D_S5), rowblk(RWKV_COLS), rowblk(d), rowblk(d)],
        out_shape=[jax.ShapeDtypeStruct((rows, n), F32) for n in (D_S5, RWKV_COLS, d, d)],
        compiler_params=_params(("arbitrary",)),
        name="in_proj",
    )(x, pat, wu, wzr, wgs, wgr)


def _s5_disc_kernel(are_ref, aim_ref, ldt_ref, bre_ref, bim_ref, abre_ref, abim_ref, bbre_ref, bbim_ref):
    a_re = jnp.minimum(are_ref[...], -1e-4)
    a_im = aim_ref[...]
    dt = jnp.exp(ldt_ref[...])
    mag = jnp.exp(a_re * dt)
    ab_re = mag * jnp.cos(a_im * dt)
    ab_im = mag * jnp.sin(a_im * dt)
    den = a_re * a_re + a_im * a_im
    nr = ab_re - 1.0
    q_re = (nr * a_re + ab_im * a_im) / den
    q_im = (ab_im * a_re - nr * a_im) / den
    abre_ref[...] = ab_re
    abim_ref[...] = ab_im
    b_re = bre_ref[...]
    b_im = bim_ref[...]
    bbre_ref[...] = q_re[:, None, :] * b_re - q_im[:, None, :] * b_im
    bbim_ref[...] = q_re[:, None, :] * b_im + q_im[:, None, :] * b_re


def _s5_discretize(a_re, a_im, log_dt, b_re, b_im):
    m, n = a_re.shape
    p = b_re.shape[1]
    return pl.pallas_call(
        _s5_disc_kernel,
        out_shape=[jax.ShapeDtypeStruct((m, n), F32), jax.ShapeDtypeStruct((m, n), F32),
                   jax.ShapeDtypeStruct((m, p, n), F32), jax.ShapeDtypeStruct((m, p, n), F32)],
        name="s5_discretize",
    )(a_re, a_im, log_dt, b_re, b_im)


def _s5_scan_kernel(u_ref, wbre_ref, wbim_ref, abre_ref, abim_ref, wc_ref, h0_ref,
                    y_ref, hfin_ref, bure_ref, buim_ref, hre_ref, him_ref, *, nb, lane_w):
    d = pl.program_id(0)
    c = pl.program_id(1)
    nc = pl.num_programs(1)
    rows = u_ref.shape[0]

    @pl.when(c == 0)
    def _():
        hre_ref[...] = h0_ref[0, 0]
        him_ref[...] = h0_ref[0, 1]

    ub = u_ref[...].astype(BF16)
    bure_ref[...] = jnp.dot(ub, wbre_ref[0], preferred_element_type=F32)
    buim_ref[...] = jnp.dot(ub, wbim_ref[0], preferred_element_type=F32)

    fwd = d == 0
    if nb >= SUBLANES:
        steps = rows // nb
        for lc in range(S5_STATE // lane_w):
            ls = slice(lc * lane_w, (lc + 1) * lane_w)
            ar = jnp.broadcast_to(abre_ref[0, :, ls], (nb, lane_w))
            ai = jnp.broadcast_to(abim_ref[0, :, ls], (nb, lane_w))

            def body(s, carry, ls=ls, ar=ar, ai=ai):
                hr, hi = carry
                t = jnp.where(fwd, s, steps - 1 - s)
                r0 = pl.multiple_of(t * nb, nb)
                br = bure_ref[pl.ds(r0, nb), ls]
                bi = buim_ref[pl.ds(r0, nb), ls]
                nr = ar * hr - ai * hi + br
                ni = ar * hi + ai * hr + bi
                bure_ref[pl.ds(r0, nb), ls] = nr
                buim_ref[pl.ds(r0, nb), ls] = ni
                return nr, ni

            hr, hi = lax.fori_loop(0, steps, body, (hre_ref[:, ls], him_ref[:, ls]))
            hre_ref[:, ls] = hr
            him_ref[:, ls] = hi
    else:
        per = SUBLANES // nb
        groups = rows // SUBLANES
        row_id = lax.broadcasted_iota(jnp.int32, (SUBLANES, lane_w), 0) // nb
        for lc in range(S5_STATE // lane_w):
            ls = slice(lc * lane_w, (lc + 1) * lane_w)
            ar = jnp.broadcast_to(abre_ref[0, :, ls], (SUBLANES, lane_w))
            ai = jnp.broadcast_to(abim_ref[0, :, ls], (SUBLANES, lane_w))
            h0r = hre_ref[:, ls]
            h0i = him_ref[:, ls]

            def body(gidx, carry, ls=ls, ar=ar, ai=ai):
                tr, ti = carry
                g = jnp.where(fwd, gidx, groups - 1 - gidx)
                r0 = pl.multiple_of(g * SUBLANES, SUBLANES)
                br = bure_ref[pl.ds(r0, SUBLANES), ls]
                bi = buim_ref[pl.ds(r0, SUBLANES), ls]
                for k in range(per):
                    pr_f = pltpu.roll(tr, nb, 0)
                    pi_f = pltpu.roll(ti, nb, 0)
                    pr_b = pltpu.roll(tr, SUBLANES - nb, 0)
                    pi_b = pltpu.roll(ti, SUBLANES - nb, 0)
                    pr = jnp.where(fwd, pr_f, pr_b)
                    pi = jnp.where(fwd, pi_f, pi_b)
                    nr = ar * pr - ai * pi + br
                    ni = ar * pi + ai * pr + bi
                    tgt = jnp.where(fwd, k, per - 1 - k)
                    sel = row_id == tgt
                    tr = jnp.where(sel, nr, tr)
                    ti = jnp.where(sel, ni, ti)
                bure_ref[pl.ds(r0, SUBLANES), ls] = tr
                buim_ref[pl.ds(r0, SUBLANES), ls] = ti
                return tr, ti

            tr, ti = lax.fori_loop(0, groups, body, (h0r, h0i))
            hre_ref[:, ls] = tr
            him_ref[:, ls] = ti

    hcat = jnp.concatenate([bure_ref[...].astype(BF16), buim_ref[...].astype(BF16)], axis=1)
    y_ref[0] = jnp.dot(hcat, wc_ref[0], preferred_element_type=F32)

    @pl.when(c == nc - 1)
    def _():
        hfin_ref[0, 0] = hre_ref[...]
        hfin_ref[0, 1] = him_ref[...]


def _s5_scan(u, wb_re, wb_im, ab_re, ab_im, wc, h0, nb, row0, rows, prev=None, chunk_rows=512):
    if nb < SUBLANES:
        h0 = jnp.tile(h0, (1, 1, SUBLANES // nb, 1))
    srows = max(nb, SUBLANES)
    cr = min(chunk_rows, rows)
    nc = rows // cr
    c0 = row0 // cr
    s = S5_STATE
    lane_w = 256 if nb >= SUBLANES else 512
    chunk = lambda d, c: c0 + c + d * (nc - 1 - 2 * c)
    kern = functools.partial(_s5_scan_kernel, nb=nb, lane_w=lane_w)
    in_specs = [pl.BlockSpec((cr, D_S5), lambda d, c: (chunk(d, c), 0)),
                pl.BlockSpec((1, D_S5, s), lambda d, c: (d, 0, 0)),
                pl.BlockSpec((1, D_S5, s), lambda d, c: (d, 0, 0)),
                pl.BlockSpec((1, 1, s), lambda d, c: (d, 0, 0)),
                pl.BlockSpec((1, 1, s), lambda d, c: (d, 0, 0)),
                pl.BlockSpec((1, 2 * s, D_S5), lambda d, c: (d, 0, 0)),
                pl.BlockSpec((1, 2, srows, s), lambda d, c: (d, 0, 0, 0))]
    args = [u, wb_re, wb_im, ab_re, ab_im, wc, h0]
    aliases = {}
    if prev is not None:
        in_specs.append(pl.BlockSpec(memory_space=pl.ANY))
        args.append(prev)
        aliases = {len(args) - 1: 0}
        kern = functools.partial(_ignore_last_input, kern, len(args))
    y, hfin = pl.pallas_call(
        kern,
        grid=(2, nc),
        in_specs=in_specs,
        out_specs=[pl.BlockSpec((1, cr, D_S5), lambda d, c: (d, chunk(d, c), 0)),
                   pl.BlockSpec((1, 2, srows, s), lambda d, c: (d, 0, 0, 0))],
        out_shape=[jax.ShapeDtypeStruct((2, u.shape[0], D_S5), F32),
                   jax.ShapeDtypeStruct((2, 2, srows, s), F32)],
        scratch_shapes=[pltpu.VMEM((cr, s), F32), pltpu.VMEM((cr, s), F32),
                        pltpu.VMEM((srows, s), F32), pltpu.VMEM((srows, s), F32)],
        input_output_aliases=aliases,
        compiler_params=_params(("arbitrary", "arbitrary")),
        name="s5_scan",
    )(*args)
    if nb < SUBLANES:
        hfin = jnp.stack([hfin[0, :, srows - nb:], hfin[1, :, :nb]])
    return y, hfin


def _ignore_last_input(kern, n_in, *refs):
    return kern(*refs[:n_in - 1], *refs[n_in:])


def _rwkv_prep_kernel(prev_ref, cur_ref, next_ref, mu_ref, ones_ref, g2_ref, kk_ref, ka_ref, rk_ref,
                      w0_ref, w2_ref, a0_ref, a2_ref,
                      r_ref, v_ref, nkk_ref, bonus_ref, g_ref, w_ref, kd_ref, b_ref, *, nb, seq, grid_shift):
    tm = cur_ref.shape[0]
    i = pl.program_id(0)
    cur = cur_ref[...]
    prv = prev_ref[...]
    nxt = next_ref[...]
    row = lax.broadcasted_iota(jnp.int32, (tm, RWKV_COLS), 0)
    lane = lax.broadcasted_iota(jnp.int32, (tm, RWKV_COLS), 1)
    t = (i * tm + row) // nb

    def rows_before(s):
        if s == tm:
            return prv
        return jnp.where(row < s, pltpu.roll(prv, s, 0), pltpu.roll(cur, s, 0))

    def rows_after(s):
        if s == tm:
            return nxt
        return jnp.where(row >= tm - s, pltpu.roll(nxt, tm - s, 0), pltpu.roll(cur, tm - s, 0))

    if grid_shift:
        tw = t % GRID_W
        left = jnp.where(tw == 0, 0.0, rows_before(nb))
        right = jnp.where(tw == GRID_W - 1, 0.0, rows_after(nb))
        up = jnp.where(t < GRID_W, 0.0, rows_before(nb * GRID_W))
        down = jnp.where(t >= seq - GRID_W, 0.0, rows_after(nb * GRID_W))
        m4 = lane % 4
        sh = jnp.where(m4 == 0, left, jnp.where(m4 == 1, right, jnp.where(m4 == 2, up, down)))
    else:
        before = jnp.where(t == 0, 0.0, rows_before(nb))
        after = jnp.where(t == seq - 1, 0.0, rows_after(nb))
        sh = jnp.where(lane % 2 == 0, before, after)

    z = cur + (sh - cur) * mu_ref[...]
    r = z[:, 0:D_RWKV]
    k = z[:, D_RWKV:2 * D_RWKV]
    v = z[:, 2 * D_RWKV:3 * D_RWKV]
    xw = z[:, 3 * D_RWKV:3 * D_RWKV + LORA_W]
    xa = z[:, 3 * D_RWKV + LORA_W:3 * D_RWKV + LORA_W + LORA_A]
    xg = z[:, 3 * D_RWKV + LORA_W + LORA_A:]
    ones = ones_ref[...]

    g_ref[...] = _dot(_sigmoid(xg), g2_ref[...])
    kk = k * kk_ref[...]
    nrm = jnp.sqrt(_dot_split(kk * kk, ones))
    kk = kk / jnp.maximum(nrm, 1e-12)
    r_ref[...] = r
    v_ref[...] = v
    nkk_ref[...] = -kk
    bonus_ref[...] = _dot_split(r * k * rk_ref[...], ones) * v
    txw = jnp.tanh(xw)
    for d in range(2):
        zw = -(w0_ref[d] + _dot(txw, w2_ref[d]))
        softplus = jnp.maximum(zw, 0.0) + jnp.log(1.0 + jnp.exp(-jnp.abs(zw)))
        w_log = -softplus - 0.5
        w_ref[d] = jnp.exp(-jnp.exp(w_log))
        a = _sigmoid(a0_ref[d] + _dot(xa, a2_ref[d]))
        kd_ref[d] = k * (1.0 + (a - 1.0) * ka_ref[...])
        b_ref[d] = kk * a


def _rwkv_prep(zr, p, nb, seq, grid_shift, row0, prev=None, tm=128):
    rows = seq * nb
    nt = rows // tm
    t0 = row0 // tm
    if grid_shift:
        assert tm == nb * GRID_W, "one tile must be one grid row of the latent grid"
    else:
        assert tm % nb == 0 and tm >= nb
    kern = functools.partial(_rwkv_prep_kernel, nb=nb, seq=seq, grid_shift=grid_shift)
    c = D_RWKV
    vec = lambda n: pl.BlockSpec((1, n), lambda i: (0, 0))
    mat = lambda a, b: pl.BlockSpec((a, b), lambda i: (0, 0))
    row_out = pl.BlockSpec((tm, c), lambda i: (i, 0))
    full_out = pl.BlockSpec((tm, c), lambda i: (t0 + i, 0))
    dir_out = pl.BlockSpec((2, tm, c), lambda i: (0, i, 0))
    head_ones = jnp.kron(jnp.eye(H_RWKV, dtype=F32), jnp.ones((RWKV_HEAD, RWKV_HEAD), F32)).astype(BF16)
    in_specs = [pl.BlockSpec((tm, RWKV_COLS), lambda i: (t0 + jnp.maximum(i - 1, 0), 0)),
                pl.BlockSpec((tm, RWKV_COLS), lambda i: (t0 + i, 0)),
                pl.BlockSpec((tm, RWKV_COLS), lambda i: (t0 + jnp.minimum(i + 1, nt - 1), 0)),
                vec(RWKV_COLS), mat(c, c), mat(LORA_G, c), vec(c), vec(c), vec(c),
                pl.BlockSpec((2, 1, c), lambda i: (0, 0, 0)),
                pl.BlockSpec((2, LORA_W, c), lambda i: (0, 0, 0)),
                pl.BlockSpec((2, 1, c), lambda i: (0, 0, 0)),
                pl.BlockSpec((2, LORA_A, c), lambda i: (0, 0, 0))]
    args = [zr, zr, zr, p['rw_mu'].reshape(1, -1), head_ones, p['rw_g2'], p['rw_k_k'].reshape(1, c),
            p['rw_k_a'].reshape(1, c), p['rw_r_k'].reshape(1, c), p['rw_w0'].reshape(2, 1, c), p['rw_w2'],
            p['rw_a0'].reshape(2, 1, c), p['rw_a2']]
    aliases = {}
    if prev is not None:
        n_in = len(args)
        in_specs += [pl.BlockSpec(memory_space=pl.ANY)] * 2
        args += list(prev)
        aliases = {n_in: 3, n_in + 1: 4}
        kern = functools.partial(_ignore_last_input, functools.partial(_ignore_last_input, kern, n_in + 1),
                                 n_in + 2)
    full = jax.ShapeDtypeStruct((zr.shape[0], c), F32)
    part = jax.ShapeDtypeStruct((rows, c), F32)
    return pl.pallas_call(
        kern,
        grid=(nt,),
        in_specs=in_specs,
        out_specs=[row_out] * 3 + [full_out] * 2 + [dir_out] * 3,
        out_shape=[part] * 3 + [full] * 2 + [jax.ShapeDtypeStruct((2, rows, c), F32)] * 3,
        input_output_aliases=aliases,
        compiler_params=_params(("arbitrary",)),
        name="rwkv_prep",
    )(*args)


def _rwkv_scan_kernel(a_ref, w_ref, b_ref, k_ref, r_ref, v_ref, s0_ref, y_ref, sfin_ref, s_ref, *, nacc, folded):
    d = pl.program_id(0)
    c = pl.program_id(1)
    nc = pl.num_programs(1)
    tc = a_ref.shape[1]
    nj = RWKV_HEAD
    rev = (c >= nc // 2) if folded else (d == 1)

    @pl.when(c == 0)
    def _():
        s_ref[...] = s0_ref[0]

    def step(s, carry):
        t = jnp.where(rev, tc - 1 - s, s)
        acc = [None] * nacc
        for j in range(nj):
            term = s_ref[j] * a_ref[0, t, pl.ds(j, 1), :]
            acc[j % nacc] = term if acc[j % nacc] is None else acc[j % nacc] + term
        sa = functools.reduce(lambda x, y: x + y, acc)
        vv = v_ref[0, t]
        acc = [None] * nacc
        for j in range(nj):
            s_new = (s_ref[j] * w_ref[0, t, pl.ds(j, 1), :]
                     + sa * b_ref[0, t, pl.ds(j, 1), :]
                     + vv * k_ref[0, t, pl.ds(j, 1), :])
            s_ref[j] = s_new
            term = s_new * r_ref[0, t, pl.ds(j, 1), :]
            acc[j % nacc] = term if acc[j % nacc] is None else acc[j % nacc] + term
        y_ref[0, t] = functools.reduce(lambda x, y: x + y, acc)
        return carry

    lax.fori_loop(0, tc, step, 0)

    @pl.when(c == nc - 1)
    def _():
        sfin_ref[0] = s_ref[...]


def _rwkv_scan(a, w, b, k, r, v, s0, tc, folded):
    _, tp, _, nl = w.shape
    iv = v.shape[2]
    nacc = 2 if iv >= 64 else 4
    nc_half = tp // tc
    if folded:
        nd, nc = 1, 2 * nc_half
        tmap = lambda d, c: (c // nc_half, jnp.where(c < nc_half, c, nc - 1 - c), 0, 0)
        shared = per_dir = tmap
    else:
        nd, nc = 2, nc_half
        chunk = lambda d, c: c + d * (nc - 1 - 2 * c)
        shared = lambda d, c: (0, chunk(d, c), 0, 0)
        per_dir = lambda d, c: (d, chunk(d, c), 0, 0)
    key = lambda m: pl.BlockSpec((1, tc, RWKV_HEAD, nl), m)
    val = lambda m: pl.BlockSpec((1, tc, iv, nl), m)
    sspec = pl.BlockSpec((1, RWKV_HEAD, iv, nl), lambda d, c: (d, 0, 0, 0))
    return pl.pallas_call(
        functools.partial(_rwkv_scan_kernel, nacc=nacc, folded=folded),
        grid=(nd, nc),
        in_specs=[key(shared), key(per_dir), key(per_dir), key(per_dir), key(shared), val(shared), sspec],
        out_specs=[val(per_dir), sspec],
        out_shape=[jax.ShapeDtypeStruct((2, tp, iv, nl), F32),
                   jax.ShapeDtypeStruct((nd, RWKV_HEAD, iv, nl), F32)],
        scratch_shapes=[pltpu.VMEM((RWKV_HEAD, iv, nl), F32)],
        compiler_params=_params(("arbitrary", "arbitrary")),
        name="rwkv_scan",
    )(a, w, b, k, r, v, s0)


SMP_REP = 4
SMP_ROWS = RWKV_HEAD // SMP_REP


def _block_transpose(tiles, bs):
    tiles = list(tiles)
    lane = lax.broadcasted_iota(jnp.int32, tiles[0].shape, 1)
    for kbit in range(3):
        sft = bs << kbit
        bit = (lane // sft) % 2 == 1
        for r in range(8):
            if (r >> kbit) & 1:
                continue
            r2 = r | (1 << kbit)
            lo, hi = tiles[r], tiles[r2]
            tiles[r] = jnp.where(bit, pltpu.roll(hi, sft, 1), lo)
            tiles[r2] = jnp.where(bit, hi, pltpu.roll(lo, LANES - sft, 1))
    return tiles


def _head_tiles(x):
    xt = x.T
    return [xt[h * RWKV_HEAD:(h + 1) * RWKV_HEAD, :] for h in range(H_RWKV)]


def _from_head_tiles(tiles):
    return jnp.concatenate(tiles, axis=0).T


def _relayout_ctx_kernel(r_ref, v_ref, nkk_ref, w_ref, kd_ref, b_ref, a_o, r_o, v_o, w_o, b_o, k_o, *, nb):
    steps = a_o.shape[1]
    for src, dst in ((nkk_ref, a_o), (r_ref, r_o), (v_ref, v_o)):
        tiles = _block_transpose(_head_tiles(src[...]), nb)
        for t in range(steps):
            dst[0, t] = tiles[t]
    for src, dst in ((w_ref, w_o), (b_ref, b_o), (kd_ref, k_o)):
        for d in range(2):
            tiles = _block_transpose(_head_tiles(src[d]), nb)
            for t in range(steps):
                dst[d, t] = tiles[t]


def _relayout_ctx(prep, nb, seq):
    r, v, nkk, _, _, w, kd, bb = prep
    steps = LANES // nb
    assert steps == H_RWKV and nb * H_RWKV == LANES
    tm = steps * nb
    nt = seq // steps
    c = D_RWKV
    nat = pl.BlockSpec((tm, c), lambda i: (i, 0))
    nat2 = pl.BlockSpec((2, tm, c), lambda i: (0, i, 0))
    out1 = pl.BlockSpec((1, steps, RWKV_HEAD, LANES), lambda i: (0, i, 0, 0))
    out2 = pl.BlockSpec((2, steps, RWKV_HEAD, LANES), lambda i: (0, i, 0, 0))
    s1 = jax.ShapeDtypeStruct((1, seq, RWKV_HEAD, LANES), F32)
    s2 = jax.ShapeDtypeStruct((2, seq, RWKV_HEAD, LANES), F32)
    a, r_, v_, w_, b_, k_ = pl.pallas_call(
        functools.partial(_relayout_ctx_kernel, nb=nb),
        grid=(nt,),
        in_specs=[nat, nat, nat, nat2, nat2, nat2],
        out_specs=[out1, out1, out1, out2, out2, out2],
        out_shape=[s1, s1, s1, s2, s2, s2],
        compiler_params=_params(("arbitrary",)),
        name="rwkv_relayout_ctx",
    )(r, v, nkk, w, kd, bb)
    return a, w_, b_, k_, r_, v_


def _unmix_ctx_kernel(y_ref, o_ref, *, nb):
    steps = y_ref.shape[1]
    tiles = [y_ref[0, t] + y_ref[1, t] for t in range(steps)]
    o_ref[...] = _from_head_tiles(_block_transpose(tiles, nb))


def _unmix_ctx(y, nb, seq, rows_total):
    steps = LANES // nb
    tm = steps * nb
    return pl.pallas_call(
        functools.partial(_unmix_ctx_kernel, nb=nb),
        grid=(seq // steps,),
        in_specs=[pl.BlockSpec((2, steps, RWKV_HEAD, LANES), lambda i: (0, i, 0, 0))],
        out_specs=pl.BlockSpec((tm, D_RWKV), lambda i: (i, 0)),
        out_shape=jax.ShapeDtypeStruct((rows_total, D_RWKV), F32),
        compiler_params=_params(("arbitrary",)),
        name="rwkv_unmix_ctx",
    )(y)


def _relayout_smp_kernel(ra, rb, va, vb, na, nb_, wa, wb, ka, kb, ba, bb, a_o, r_o, v_o, w_o, b_o, k_o):
    steps = a_o.shape[1]
    lane = lax.broadcasted_iota(jnp.int32, (RWKV_HEAD, LANES), 1)
    lane_v = lax.broadcasted_iota(jnp.int32, (SMP_ROWS, LANES), 1)

    def conv(xa, xb):
        return _block_transpose(_head_tiles(jnp.concatenate([xa, xb], axis=0)), 2)

    def sources(o0, o1, half, s):
        s_hi, s_lo = divmod(s, 8)
        m_hi, m_lo = divmod(steps - 1 - s, 8)
        la = 16 * s_hi
        lb = 64 + 16 * m_hi
        if half == 0:
            return o0[s_lo], la, o1[m_lo], lb
        return o0[m_lo], lb, o1[s_lo], la

    def roll_to(x, src, dst):
        sh = (dst - src) % LANES
        return x if sh == 0 else pltpu.roll(x, sh, 1)

    def key_tile(o0, o1, half, s):
        x0, l0, x1, l1 = sources(o0, o1, half, s)
        m = jnp.where(lane % 32 < 16, roll_to(x0, l0, 0), roll_to(x1, l1, 16))
        m = jnp.where(lane < 32, m, pltpu.roll(m, 32, 1))
        return jnp.where(lane < 64, m, pltpu.roll(m, 64, 1))

    def val_tile(o0, o1, half, s):
        x0, l0, x1, l1 = sources(o0, o1, half, s)
        acc = jnp.zeros((SMP_ROWS, LANES), F32)
        for blk in range(SMP_REP):
            rows = slice(blk * SMP_ROWS, (blk + 1) * SMP_ROWS)
            p0 = roll_to(x0[rows], l0, 32 * blk)
            p1 = roll_to(x1[rows], l1, 32 * blk + 16)
            grp = lane_v // 16
            acc = jnp.where(grp == 2 * blk, p0, jnp.where(grp == 2 * blk + 1, p1, acc))
        return acc

    o_n = conv(na[...], nb_[...])
    o_r = conv(ra[...], rb[...])
    o_v = conv(va[...], vb[...])
    for half in range(2):
        for s in range(steps):
            a_o[half, s] = key_tile(o_n, o_n, half, s)
            r_o[half, s] = key_tile(o_r, o_r, half, s)
            v_o[half, s] = val_tile(o_v, o_v, half, s)
    for srca, srcb, dst in ((wa, wb, w_o), (ba, bb, b_o), (ka, kb, k_o)):
        o0 = conv(srca[0], srcb[0])
        o1 = conv(srca[1], srcb[1])
        for half in range(2):
            for s in range(steps):
                dst[half, s] = key_tile(o0, o1, half, s)


def _relayout_smp(prep, nb, seq, row0):
    r, v, nkk, _, _, w, kd, bb = prep
    assert 2 * nb * H_RWKV * SMP_REP == LANES
    steps = 32
    tm = steps * nb
    nt = seq // steps
    half_t = seq // 2
    c = D_RWKV
    t0 = row0 // tm
    blk_a = pl.BlockSpec((tm, c), lambda g: (t0 + g, 0))
    blk_b = pl.BlockSpec((tm, c), lambda g: (t0 + nt - 1 - g, 0))
    blk2_a = pl.BlockSpec((2, tm, c), lambda g: (0, t0 + g, 0))
    blk2_b = pl.BlockSpec((2, tm, c), lambda g: (0, t0 + nt - 1 - g, 0))
    okey = pl.BlockSpec((2, steps, RWKV_HEAD, LANES), lambda g: (0, g, 0, 0))
    oval = pl.BlockSpec((2, steps, SMP_ROWS, LANES), lambda g: (0, g, 0, 0))
    skey = jax.ShapeDtypeStruct((2, half_t, RWKV_HEAD, LANES), F32)
    sval = jax.ShapeDtypeStruct((2, half_t, SMP_ROWS, LANES), F32)
    a, r_, v_, w_, b_, k_ = pl.pallas_call(
        _relayout_smp_kernel,
        grid=(nt // 2,),
        in_specs=[blk_a, blk_b, blk_a, blk_b, blk_a, blk_b, blk2_a, blk2_b, blk2_a, blk2_b, blk2_a, blk2_b],
        out_specs=[okey, okey, oval, okey, okey, okey],
        out_shape=[skey, skey, sval, skey, skey, skey],
        compiler_params=_params(("arbitrary",)),
        name="rwkv_relayout_smp",
    )(r, r, v, v, nkk, nkk, w, w, kd, kd, bb, bb)
    return a, w_, b_, k_, r_, v_


def _unmix_smp_kernel(y_ref, prev_ref, o_ref, *, half_tiles):
    del prev_ref
    q = pl.program_id(0)
    steps = y_ref.shape[1]
    low = q < half_tiles
    h0 = jnp.where(low, 0, 1)
    lane = lax.broadcasted_iota(jnp.int32, (SMP_ROWS, LANES), 1)
    tiles = []
    for s_lo in range(8):
        blocks = []
        for blk in range(SMP_REP):
            acc = jnp.zeros((SMP_ROWS, LANES), F32)
            for s_hi in range(steps // 8):
                s = 8 * s_hi + s_lo
                u = jnp.where(low, s, steps - 1 - s)
                d0 = y_ref[h0, u]
                d1 = y_ref[1 - h0, u]
                sh0 = (16 * s_hi - 32 * blk) % LANES
                sh1 = (16 * s_hi - 32 * blk - 16) % LANES
                t0 = d0 if sh0 == 0 else pltpu.roll(d0, sh0, 1)
                t1 = d1 if sh1 == 0 else pltpu.roll(d1, sh1, 1)
                acc = jnp.where(lane // 16 == s_hi, t0 + t1, acc)
            blocks.append(acc)
        tiles.append(jnp.concatenate(blocks, axis=0))
    o_ref[...] = _from_head_tiles(_block_transpose(tiles, 2))


def _unmix_smp(y, prev, nb, seq, row0):
    steps = 64
    tm = steps * nb
    assert tm == LANES
    nq = seq // steps
    half_tiles = nq // 2
    t0 = row0 // tm
    return pl.pallas_call(
        functools.partial(_unmix_smp_kernel, half_tiles=half_tiles),
        grid=(nq,),
        in_specs=[pl.BlockSpec((2, steps, SMP_ROWS, LANES),
                               lambda q: (0, jnp.where(q < half_tiles, q, nq - 1 - q), 0, 0)),
                  pl.BlockSpec(memory_space=pl.ANY)],
        out_specs=pl.BlockSpec((tm, D_RWKV), lambda q: (t0 + q, 0)),
        out_shape=jax.ShapeDtypeStruct(prev.shape, F32),
        input_output_aliases={1: 0},
        compiler_params=_params(("arbitrary",)),
        name="rwkv_unmix_smp",
    )(y, prev)


def _rwkv_state_smp(state_l, nb):
    s = state_l.reshape(nb, 2, H_RWKV, SMP_REP, SMP_ROWS, RWKV_HEAD)
    s = s.transpose(5, 4, 3, 1, 2, 0)
    return s.reshape(1, RWKV_HEAD, SMP_ROWS, LANES)


def _rwkv_state_out_ctx(sfin, nb):
    s = sfin.reshape(2, RWKV_HEAD, RWKV_HEAD, H_RWKV, nb)
    return s.transpose(4, 0, 3, 2, 1)


def _mix_out_kernel(x_ref, pat_ref, u_ref, y5_ref, gs_ref, gr_ref, yrw_ref, bonus_ref, g_ref,
                    d_ref, wglu_ref, ones_ref, lnxw_ref, lnxb_ref, wups_ref, wupr_ref, wout_ref,
                    ln1g_ref, ln1b_ref, o_ref):
    y5 = u_ref[...] * d_ref[...] + y5_ref[0] + y5_ref[1]
    y5 = jax.nn.gelu(y5)
    y5 = y5 * _sigmoid(_dot(y5, wglu_ref[...]))
    ones = ones_ref[...]
    yr = yrw_ref[...]
    inv_n = 1.0 / RWKV_HEAD
    mu = _dot_split(yr, ones) * inv_n
    yc = yr - mu
    var = _dot_split(yc * yc, ones) * inv_n
    yr = yc * lax.rsqrt(var + GN_EPS) * lnxw_ref[...] + lnxb_ref[...]
    yr = (yr + bonus_ref[...]) * g_ref[...]
    merged = (_sigmoid(gs_ref[...]) * _dot(y5, wups_ref[...])
              + _sigmoid(gr_ref[...]) * _dot(yr, wupr_ref[...]))
    mix = _dot(merged, wout_ref[...])
    x3 = _modulate(mix, jnp.zeros_like(pat_ref[0, 2]), pat_ref[0, 2] - 1.0)
    o_ref[...] = _layer_norm(ALPHA * x_ref[...] + x3, ln1g_ref[...], ln1b_ref[...])


def _mix_out(x, pat, u, y5, gs, gr, yrw, bonus, g, p, rows_per_trunk, tm=256):
    rows = x.shape[0]
    d = D_MODEL
    c = D_RWKV
    tpt = rows_per_trunk // tm
    rowblk = lambda n: pl.BlockSpec((tm, n), lambda i: (i, 0))
    vec = lambda n: pl.BlockSpec((1, n), lambda i: (0, 0))
    mat = lambda a, b: pl.BlockSpec((a, b), lambda i: (0, 0))
    head_ones = jnp.kron(jnp.eye(H_RWKV, dtype=F32), jnp.ones((RWKV_HEAD, RWKV_HEAD), F32)).astype(BF16)
    return pl.pallas_call(
        _mix_out_kernel,
        grid=(rows // tm,),
        in_specs=[rowblk(d),
                  pl.BlockSpec((1, 6, SUBLANES, d), lambda i: (i // tpt, 0, 0, 0)),
                  rowblk(c), pl.BlockSpec((2, tm, c), lambda i: (0, i, 0)),
                  rowblk(d), rowblk(d), rowblk(c), rowblk(c), rowblk(c),
                  vec(c), mat(c, c), mat(c, c), vec(c), vec(c), mat(c, d), mat(c, d), mat(d, d),
                  vec(d), vec(d)],
        out_specs=rowblk(d),
        out_shape=jax.ShapeDtypeStruct((rows, d), F32),
        compiler_params=_params(("arbitrary",)),
        name="mix_out",
    )(x, pat, u, y5, gs, gr, yrw, bonus, g,
      p['s5_d'].reshape(1, c), p['s5_w_glu'].astype(BF16), head_ones,
      p['rw_lnx_w'].reshape(1, c), p['rw_lnx_b'].reshape(1, c),
      p['w_up_s5'].astype(BF16), p['w_up_rwkv'].astype(BF16), p['w_out'].astype(BF16),
      p['ln1_g'].reshape(1, d), p['ln1_b'].reshape(1, d))


def _first_max(x, lane, valid):
    xm = jnp.where(valid, x, -jnp.inf)
    m = jnp.max(xm, -1, keepdims=True)
    idx = jnp.min(jnp.where(xm == m, lane, float(N_EXPERTS)), -1, keepdims=True)
    return m, idx


def _moe_kernel(x_ref, pat_ref, wr_ref, br_ref, wg_ref, wu_ref, wd_ref, ln2g_ref, ln2b_ref, o_ref,
                hb_ref, comb_ref, acc_ref):
    e = pl.program_id(1)
    ne = pl.num_programs(1)
    tm = x_ref.shape[0]

    @pl.when(e == 0)
    def _():
        h = _modulate(x_ref[...], pat_ref[0, 3], pat_ref[0, 4])
        hb_ref[...] = h.astype(BF16)
        wr = wr_ref[...]
        w_hi = wr.astype(BF16)
        w_lo = (wr - w_hi.astype(F32)).astype(BF16)
        h_hi = h.astype(BF16)
        h_lo = (h - h_hi.astype(F32)).astype(BF16)
        logits = (jnp.dot(h_hi, w_hi, preferred_element_type=F32)
                  + jnp.dot(h_hi, w_lo, preferred_element_type=F32)
                  + jnp.dot(h_lo, w_hi, preferred_element_type=F32)) + br_ref[...]
        logits = logits - jnp.max(logits, -1, keepdims=True)
        ex = jnp.exp(logits)
        probs = ex / jnp.sum(ex, -1, keepdims=True)
        lane_i = lax.broadcasted_iota(jnp.int32, (tm, N_EXPERTS), 1)
        lane = lane_i.astype(F32)
        grp = (lane_i // EXPERTS_PER_GROUP).astype(F32)
        best_score = jnp.full((tm, 1), -jnp.inf, F32)
        best_grp = jnp.zeros((tm, 1), F32)
        for gi in range(N_GROUPS):
            in_g = grp == float(gi)
            m1, i1 = _first_max(probs, lane, in_g)
            m2, _ = _first_max(probs, lane, in_g & (lane != i1))
            score = m1 + m2
            better = score > best_score
            best_score = jnp.where(better, score, best_score)
            best_grp = jnp.where(better, float(gi), best_grp)
        in_best = grp == best_grp
        m1, i1 = _first_max(probs, lane, in_best)
        m2, i2 = _first_max(probs, lane, in_best & (lane != i1))
        tot = m1 + m2
        comb_ref[...] = jnp.where(lane == i1, m1 / tot, 0.0) + jnp.where(lane == i2, m2 / tot, 0.0)
        acc_ref[...] = jnp.zeros_like(acc_ref)

    hb = hb_ref[...]
    lane = lax.broadcasted_iota(jnp.int32, (tm, N_EXPERTS), 1)
    ce = jnp.sum(jnp.where(lane == e, comb_ref[...], 0.0), -1, keepdims=True)
    hid = _silu(_dot(hb, wg_ref[0])) * _dot(hb, wu_ref[0])
    acc_ref[...] += _dot(hid * ce, wd_ref[0])

    @pl.when(e == ne - 1)
    def _():
        ffn = _modulate(acc_ref[...], jnp.zeros_like(pat_ref[0, 5]), pat_ref[0, 5] - 1.0)
        o_ref[...] = _layer_norm(ALPHA * x_ref[...] + ffn, ln2g_ref[...], ln2b_ref[...])


def _moe(x, pat, w_router, b_router, wg, wu, wd, ln2_g, ln2_b, rows_per_trunk, tm=1024):
    rows = x.shape[0]
    tm = min(tm, rows_per_trunk)
    d = D_MODEL
    tpt = rows_per_trunk // tm
    rowblk = pl.BlockSpec((tm, d), lambda i, e: (i, 0))
    vec = lambda n: pl.BlockSpec((1, n), lambda i, e: (0, 0))
    return pl.pallas_call(
        _moe_kernel,
        grid=(rows // tm, N_EXPERTS),
        in_specs=[rowblk,
                  pl.BlockSpec((1, 6, SUBLANES, d), lambda i, e: (i // tpt, 0, 0, 0)),
                  pl.BlockSpec((d, N_EXPERTS), lambda i, e: (0, 0)), vec(N_EXPERTS),
                  pl.BlockSpec((1, d, D_EXPERT), lambda i, e: (e, 0, 0)),
                  pl.BlockSpec((1, d, D_EXPERT), lambda i, e: (e, 0, 0)),
                  pl.BlockSpec((1, D_EXPERT, d), lambda i, e: (e, 0, 0)),
                  vec(d), vec(d)],
        out_specs=rowblk,
        out_shape=jax.ShapeDtypeStruct((rows, d), F32),
        scratch_shapes=[pltpu.VMEM((tm, d), BF16), pltpu.VMEM((tm, N_EXPERTS), F32),
                        pltpu.VMEM((tm, d), F32)],
        compiler_params=_params(("arbitrary", "arbitrary")),
        name="moe",
    )(x, pat, w_router, b_router.reshape(1, N_EXPERTS), wg, wu, wd,
      ln2_g.reshape(1, d), ln2_b.reshape(1, d))


def _block_diag_in(bb):
    nd, g, p, n = bb.shape
    eye = jnp.eye(g, dtype=bb.dtype)
    return jnp.einsum('dgpn,gh->dgphn', bb, eye).reshape(nd, g * p, g * n)


def _block_diag_out(cc):
    nd, g, p, n = cc.shape
    eye = jnp.eye(g, dtype=cc.dtype)
    return jnp.einsum('dgpn,gh->dgnhp', cc, eye).reshape(nd, g * n, g * p)


def kernel(x_prompt, x_sample, state_s5, state_rwkv, c, c_ctx, w_ada, b_ada, w_in, s5_a_re, s5_a_im, s5_log_dt, s5_b_re, s5_b_im, s5_c_re, s5_c_im, s5_d, s5_w_glu, rw_mu, rw_w0, rw_w2, rw_a0, rw_a2, rw_g2, rw_k_k, rw_k_a, rw_r_k, rw_lnx_w, rw_lnx_b, w_up_s5, w_up_rwkv, w_out, ln1_g, ln1_b, ln2_g, ln2_b, w_router, b_router, w_exp_gate, w_exp_up, w_exp_down):
    nbc, tc_len, d = x_prompt.shape
    nbs, ts_len, _ = x_sample.shape
    nl = w_ada.shape[0]
    rc = nbc * tc_len
    rs = nbs * ts_len
    assert rc == rs, "both trunks are processed as equal halves of one row-major token matrix"
    assert SUBLANES % nbs == 0 and nbc % SUBLANES == 0

    x = jnp.concatenate([x_prompt.transpose(1, 0, 2).reshape(rc, d),
                         x_sample.transpose(1, 0, 2).reshape(rs, d)], axis=0)

    cond = jnp.concatenate([c_ctx[None], c], axis=0)
    cond8 = jnp.zeros((SUBLANES, d), F32).at[:cond.shape[0]].set(cond)
    mod = _ada_mod(cond8, w_ada, b_ada).reshape(nl, SUBLANES, 6, d)
    ctx_rows = jnp.zeros((SUBLANES,), jnp.int32)
    smp_rows = 1 + jnp.arange(SUBLANES, dtype=jnp.int32) % nbs
    pat_idx = jnp.stack([ctx_rows, smp_rows])
    pats = mod[:, pat_idx]
    pats = pats.transpose(0, 1, 3, 2, 4)

    m = nl * 2 * G_S5
    ab_re, ab_im, bb_re, bb_im = _s5_discretize(
        s5_a_re.reshape(m, N_S5), s5_a_im.reshape(m, N_S5), s5_log_dt.reshape(m, 1),
        s5_b_re.reshape(m, N_S5, S5_GROUP).transpose(0, 2, 1),
        s5_b_im.reshape(m, N_S5, S5_GROUP).transpose(0, 2, 1))
    ab_re = ab_re.reshape(nl, 2, 1, S5_STATE)
    ab_im = ab_im.reshape(nl, 2, 1, S5_STATE)
    bb_re = bb_re.reshape(nl, 2, G_S5, S5_GROUP, N_S5)
    bb_im = bb_im.reshape(nl, 2, G_S5, S5_GROUP, N_S5)

    zero_s5 = jnp.zeros((2, 2, nbc, S5_STATE), F32)
    zero_rw = jnp.zeros((2, RWKV_HEAD, RWKV_HEAD, nbc * H_RWKV), F32)
    s5_out, rw_out = [], []
    for l in range(nl):
        p = dict(s5_d=s5_d[l], s5_w_glu=s5_w_glu[l], rw_mu=rw_mu[l], rw_w0=rw_w0[l], rw_w2=rw_w2[l],
                 rw_a0=rw_a0[l], rw_a2=rw_a2[l], rw_g2=rw_g2[l], rw_k_k=rw_k_k[l], rw_k_a=rw_k_a[l],
                 rw_r_k=rw_r_k[l], rw_lnx_w=rw_lnx_w[l], rw_lnx_b=rw_lnx_b[l], w_up_s5=w_up_s5[l],
                 w_up_rwkv=w_up_rwkv[l], w_out=w_out[l], ln1_g=ln1_g[l], ln1_b=ln1_b[l])
        pat = pats[l]
        u, zr, gs, gr = _inproj(x, pat, w_in[l], rc)

        wb_re = _block_diag_in(bb_re[l]).astype(BF16)
        wb_im = _block_diag_in(bb_im[l]).astype(BF16)
        wc = jnp.concatenate([_block_diag_out(s5_c_re[l]), -_block_diag_out(s5_c_im[l])], axis=1).astype(BF16)
        y5, hfin = _s5_scan(u, wb_re, wb_im, ab_re[l], ab_im[l], wc, zero_s5, nbc, 0, rc)
        h0s = state_s5[:, l].reshape(nbs, 2, 2, S5_STATE).transpose(1, 2, 0, 3)
        y5, _ = _s5_scan(u, wb_re, wb_im, ab_re[l], ab_im[l], wc, h0s, nbs, rc, rs, prev=y5)
        s5_out.append(hfin.transpose(2, 0, 1, 3).reshape(nbc, 2, 2, G_S5, N_S5))

        prep_c = _rwkv_prep(zr, p, nbc, tc_len, False, 0)
        prep_s = _rwkv_prep(zr, p, nbs, ts_len, True, rc, prev=(prep_c[3], prep_c[4]))
        bonus, g = prep_s[3], prep_s[4]
        yc, sfin = _rwkv_scan(*_relayout_ctx(prep_c, nbc, tc_len), zero_rw, tc=min(32, tc_len), folded=False)
        ys, _ = _rwkv_scan(*_relayout_smp(prep_s, nbs, ts_len, 0), _rwkv_state_smp(state_rwkv[:, l], nbs),
                           tc=min(64, ts_len // 2), folded=True)
        yrw = _unmix_smp(ys, _unmix_ctx(yc, nbc, tc_len, rc + rs), nbs, ts_len, rc)
        rw_out.append(_rwkv_state_out_ctx(sfin, nbc))

        x = _mix_out(x, pat, u, y5, gs, gr, yrw, bonus, g, p, rc)
        x = _moe(x, pat, w_router, b_router, w_exp_gate[l], w_exp_up[l], w_exp_down[l], ln2_g[l], ln2_b[l], rc)

    y_prompt = x[:rc].reshape(tc_len, nbc, d).transpose(1, 0, 2)
    y_sample = x[rc:].reshape(ts_len, nbs, d).transpose(1, 0, 2)
    return (y_prompt, y_sample, jnp.stack(s5_out, 1), jnp.stack(rw_out, 1))
```

```python
import functools

import numpy as np
import jax
import jax.numpy as jnp
from jax import lax
from jax.experimental import pallas as pl
from jax.experimental.pallas import tpu as pltpu

D_MODEL = 1024
DEPTH = 2
GRID_W = 64
D_S5 = 512
S5_GROUP = 16
G_S5 = 32
N_S5 = 64
S5_STATE = G_S5 * N_S5
D_RWKV = 512
RWKV_HEAD = 64
H_RWKV = 8
LORA_W = 64
LORA_A = 64
LORA_G = 128
RWKV_COLS = 3 * D_RWKV + LORA_W + LORA_A + LORA_G
N_EXPERTS = 16
N_GROUPS = 4
EXPERTS_PER_GROUP = 4
D_EXPERT = 512
ALPHA = (2 * DEPTH) ** 0.25
LN_EPS = 1e-5
GN_EPS = 64e-5

SUBLANES = 8
LANES = 128
VMEM_LIMIT = 56 * 1024 * 1024

F32 = jnp.float32
BF16 = jnp.bfloat16


def _params(sem):
    return pltpu.CompilerParams(dimension_semantics=sem, vmem_limit_bytes=VMEM_LIMIT)


def _dot(a, b):
    return jnp.dot(a.astype(BF16), b.astype(BF16), preferred_element_type=F32)


def _dot_split(x, w_exact):
    hi = x.astype(BF16)
    lo = (x - hi.astype(F32)).astype(BF16)
    return (jnp.dot(hi, w_exact, preferred_element_type=F32)
            + jnp.dot(lo, w_exact, preferred_element_type=F32))


def _sigmoid(x):
    return 1.0 / (1.0 + jnp.exp(-x))


def _silu(x):
    return x * _sigmoid(x)


def _layer_norm(x, g, b):
    mu = jnp.mean(x, -1, keepdims=True)
    xc = x - mu
    var = jnp.mean(xc * xc, -1, keepdims=True)
    return xc * lax.rsqrt(var + LN_EPS) * g + b


def _modulate(x, shift8, scale8):
    rows, d = x.shape
    x3 = x.reshape(rows // SUBLANES, SUBLANES, d)
    return (x3 * (1.0 + scale8)[None] + shift8[None]).reshape(rows, d)


def _ada_kernel(cond_ref, w_ref, b_ref, o_ref):
    c = cond_ref[...]
    o_ref[0] = _dot(_silu(c), w_ref[0]) + b_ref[0]


def _ada_mod(cond8, w_ada, b_ada):
    nl = w_ada.shape[0]
    d = D_MODEL
    return pl.pallas_call(
        _ada_kernel,
        grid=(nl, 6),
        in_specs=[pl.BlockSpec((SUBLANES, d), lambda l, k: (0, 0)),
                  pl.BlockSpec((1, d, d), lambda l, k: (l, 0, k)),
                  pl.BlockSpec((1, 1, d), lambda l, k: (l, 0, k))],
        out_specs=pl.BlockSpec((1, SUBLANES, d), lambda l, k: (l, 0, k)),
        out_shape=jax.ShapeDtypeStruct((nl, SUBLANES, 6 * d), F32),
        compiler_params=_params(("arbitrary", "arbitrary")),
        name="ada_mod",
    )(cond8, w_ada, b_ada.reshape(nl, 1, 6 * d))


def _inproj_kernel(x_ref, pat_ref, wu_ref, wzr_ref, wgs_ref, wgr_ref, u_ref, zr_ref, gs_ref, gr_ref):
    h = _modulate(x_ref[...], pat_ref[0, 0], pat_ref[0, 1]).astype(BF16)
    u_ref[...] = jnp.dot(h, wu_ref[...], preferred_element_type=F32)
    zr_ref[...] = jnp.dot(h, wzr_ref[...], preferred_element_type=F32)
    gs_ref[...] = jnp.dot(h, wgs_ref[...], preferred_element_type=F32)
    gr_ref[...] = jnp.dot(h, wgr_ref[...], preferred_element_type=F32)


def _inproj(x, pat, w_in_l, rows_per_trunk, tm=256):
    rows = x.shape[0]
    d = D_MODEL
    tpt = rows_per_trunk // tm
    wu = w_in_l[:, :D_S5].astype(BF16)
    wzr = w_in_l[:, D_S5:D_S5 + RWKV_COLS].astype(BF16)
    wgs = w_in_l[:, D_S5 + RWKV_COLS:D_S5 + RWKV_COLS + d].astype(BF16)
    wgr = w_in_l[:, D_S5 + RWKV_COLS + d:].astype(BF16)
    full = lambda n: pl.BlockSpec((d, n), lambda i: (0, 0))
    rowblk = lambda n: pl.BlockSpec((tm, n), lambda i: (i, 0))
    return pl.pallas_call(
        _inproj_kernel,
        grid=(rows // tm,),
        in_specs=[rowblk(d),
                  pl.BlockSpec((1, 6, SUBLANES, d), lambda i: (i // tpt, 0, 0, 0)),
                  full(D_S5), full(RWKV_COLS), full(d), full(d)],
        out_specs=[rowblk(D_S5), rowblk(RWKV_COLS), rowblk(d), rowblk(d)],
        out_shape=[jax.ShapeDtypeStruct((rows, n), F32) for n in (D_S5, RWKV_COLS, d, d)],
        compiler_params=_params(("arbitrary",)),
        name="in_proj",
    )(x, pat, wu, wzr, wgs, wgr)


def _s5_disc_kernel(are_ref, aim_ref, ldt_ref, bre_ref, bim_ref, abre_ref, abim_ref, bbre_ref, bbim_ref):
    a_re = jnp.minimum(are_ref[...], -1e-4)
    a_im = aim_ref[...]
    dt = jnp.exp(ldt_ref[...])
    mag = jnp.exp(a_re * dt)
    ab_re = mag * jnp.cos(a_im * dt)
    ab_im = mag * jnp.sin(a_im * dt)
    den = a_re * a_re + a_im * a_im
    nr = ab_re - 1.0
    q_re = (nr * a_re + ab_im * a_im) / den
    q_im = (ab_im * a_re - nr * a_im) / den
    abre_ref[...] = ab_re
    abim_ref[...] = ab_im
    b_re = bre_ref[...]
    b_im = bim_ref[...]
    bbre_ref[...] = q_re[:, None, :] * b_re - q_im[:, None, :] * b_im
    bbim_ref[...] = q_re[:, None, :] * b_im + q_im[:, None, :] * b_re


def _s5_discretize(a_re, a_im, log_dt, b_re, b_im):
    m, n = a_re.shape
    p = b_re.shape[1]
    return pl.pallas_call(
        _s5_disc_kernel,
        out_shape=[jax.ShapeDtypeStruct((m, n), F32), jax.ShapeDtypeStruct((m, n), F32),
                   jax.ShapeDtypeStruct((m, p, n), F32), jax.ShapeDtypeStruct((m, p, n), F32)],
        name="s5_discretize",
    )(a_re, a_im, log_dt, b_re, b_im)


def _s5_scan_kernel(u_ref, wbre_ref, wbim_ref, abre_ref, abim_ref, wc_ref, h0_ref,
                    y_ref, hfin_ref, bure_ref, buim_ref, hre_ref, him_ref, *, nb, lane_w):
    d = pl.program_id(0)
    c = pl.program_id(1)
    nc = pl.num_programs(1)
    rows = u_ref.shape[0]

    @pl.when(c == 0)
    def _():
        hre_ref[...] = h0_ref[0, 0]
        him_ref[...] = h0_ref[0, 1]

    ub = u_ref[...].astype(BF16)
    bure_ref[...] = jnp.dot(ub, wbre_ref[0], preferred_element_type=F32)
    buim_ref[...] = jnp.dot(ub, wbim_ref[0], preferred_element_type=F32)

    fwd = d == 0
    if nb >= SUBLANES:
        steps = rows // nb
        for lc in range(S5_STATE // lane_w):
            ls = slice(lc * lane_w, (lc + 1) * lane_w)
            ar = jnp.broadcast_to(abre_ref[0, :, ls], (nb, lane_w))
            ai = jnp.broadcast_to(abim_ref[0, :, ls], (nb, lane_w))

            def body(s, carry, ls=ls, ar=ar, ai=ai):
                hr, hi = carry
                t = jnp.where(fwd, s, steps - 1 - s)
                r0 = pl.multiple_of(t * nb, nb)
                br = bure_ref[pl.ds(r0, nb), ls]
                bi = buim_ref[pl.ds(r0, nb), ls]
                nr = ar * hr - ai * hi + br
                ni = ar * hi + ai * hr + bi
                bure_ref[pl.ds(r0, nb), ls] = nr
                buim_ref[pl.ds(r0, nb), ls] = ni
                return nr, ni

            hr, hi = lax.fori_loop(0, steps, body, (hre_ref[:, ls], him_ref[:, ls]))
            hre_ref[:, ls] = hr
            him_ref[:, ls] = hi
    else:
        per = SUBLANES // nb
        groups = rows // SUBLANES
        row_id = lax.broadcasted_iota(jnp.int32, (SUBLANES, lane_w), 0) // nb
        shift = jnp.where(fwd, nb, SUBLANES - nb)
        for lc in range(S5_STATE // lane_w):
            ls = slice(lc * lane_w, (lc + 1) * lane_w)
            ar = jnp.broadcast_to(abre_ref[0, :, ls], (SUBLANES, lane_w))
            ai = jnp.broadcast_to(abim_ref[0, :, ls], (SUBLANES, lane_w))
            h0r = hre_ref[:, ls]
            h0i = him_ref[:, ls]

            def body(gidx, carry, ls=ls, ar=ar, ai=ai):
                tr, ti = carry
                g = jnp.where(fwd, gidx, groups - 1 - gidx)
                r0 = pl.multiple_of(g * SUBLANES, SUBLANES)
                br = bure_ref[pl.ds(r0, SUBLANES), ls]
                bi = buim_ref[pl.ds(r0, SUBLANES), ls]
                for k in range(per):
                    pr = pltpu.roll(tr, shift, 0)
                    pi = pltpu.roll(ti, shift, 0)
                    nr = ar * pr - ai * pi + br
                    ni = ar * pi + ai * pr + bi
                    tgt = jnp.where(fwd, k, per - 1 - k)
                    sel = row_id == tgt
                    tr = jnp.where(sel, nr, tr)
                    ti = jnp.where(sel, ni, ti)
                bure_ref[pl.ds(r0, SUBLANES), ls] = tr
                buim_ref[pl.ds(r0, SUBLANES), ls] = ti
                return tr, ti

            tr, ti = lax.fori_loop(0, groups, body, (h0r, h0i))
            hre_ref[:, ls] = tr
            him_ref[:, ls] = ti

    hcat = jnp.concatenate([bure_ref[...].astype(BF16), buim_ref[...].astype(BF16)], axis=1)
    y_ref[0] = jnp.dot(hcat, wc_ref[0], preferred_element_type=F32)

    @pl.when(c == nc - 1)
    def _():
        hfin_ref[0, 0] = hre_ref[...]
        hfin_ref[0, 1] = him_ref[...]


def _s5_scan(u, wb_re, wb_im, ab_re, ab_im, wc, h0, nb, row0, rows, prev=None, chunk_rows=512):
    if nb < SUBLANES:
        h0 = jnp.tile(h0, (1, 1, SUBLANES // nb, 1))
    srows = max(nb, SUBLANES)
    cr = min(chunk_rows, rows)
    nc = rows // cr
    c0 = row0 // cr
    s = S5_STATE
    lane_w = 256 if nb >= SUBLANES else 512
    chunk = lambda d, c: c0 + c + d * (nc - 1 - 2 * c)
    kern = functools.partial(_s5_scan_kernel, nb=nb, lane_w=lane_w)
    in_specs = [pl.BlockSpec((cr, D_S5), lambda d, c: (chunk(d, c), 0)),
                pl.BlockSpec((1, D_S5, s), lambda d, c: (d, 0, 0)),
                pl.BlockSpec((1, D_S5, s), lambda d, c: (d, 0, 0)),
                pl.BlockSpec((1, 1, s), lambda d, c: (d, 0, 0)),
                pl.BlockSpec((1, 1, s), lambda d, c: (d, 0, 0)),
                pl.BlockSpec((1, 2 * s, D_S5), lambda d, c: (d, 0, 0)),
                pl.BlockSpec((1, 2, srows, s), lambda d, c: (d, 0, 0, 0))]
    args = [u, wb_re, wb_im, ab_re, ab_im, wc, h0]
    aliases = {}
    if prev is not None:
        in_specs.append(pl.BlockSpec(memory_space=pl.ANY))
        args.append(prev)
        aliases = {len(args) - 1: 0}
        kern = functools.partial(_ignore_last_input, kern, len(args))
    y, hfin = pl.pallas_call(
        kern,
        grid=(2, nc),
        in_specs=in_specs,
        out_specs=[pl.BlockSpec((1, cr, D_S5), lambda d, c: (d, chunk(d, c), 0)),
                   pl.BlockSpec((1, 2, srows, s), lambda d, c: (d, 0, 0, 0))],
        out_shape=[jax.ShapeDtypeStruct((2, u.shape[0], D_S5), F32),
                   jax.ShapeDtypeStruct((2, 2, srows, s), F32)],
        scratch_shapes=[pltpu.VMEM((cr, s), F32), pltpu.VMEM((cr, s), F32),
                        pltpu.VMEM((srows, s), F32), pltpu.VMEM((srows, s), F32)],
        input_output_aliases=aliases,
        compiler_params=_params(("arbitrary", "arbitrary")),
        name="s5_scan",
    )(*args)
    if nb < SUBLANES:
        hfin = jnp.stack([hfin[0, :, srows - nb:], hfin[1, :, :nb]])
    return y, hfin


def _ignore_last_input(kern, n_in, *refs):
    return kern(*refs[:n_in - 1], *refs[n_in:])


def _rwkv_prep_kernel(prev_ref, cur_ref, next_ref, mu_ref, ones_ref, g2_ref, kk_ref, ka_ref, rk_ref,
                      w0_ref, w2_ref, a0_ref, a2_ref,
                      r_ref, v_ref, nkk_ref, bonus_ref, g_ref, w_ref, kd_ref, b_ref, *, nb, seq, grid_shift):
    tm = cur_ref.shape[0]
    i = pl.program_id(0)
    cur = cur_ref[...]
    prv = prev_ref[...]
    nxt = next_ref[...]
    row = lax.broadcasted_iota(jnp.int32, (tm, RWKV_COLS), 0)
    lane = lax.broadcasted_iota(jnp.int32, (tm, RWKV_COLS), 1)
    t = (i * tm + row) // nb

    def rows_before(s):
        if s == tm:
            return prv
        return jnp.where(row < s, pltpu.roll(prv, s, 0), pltpu.roll(cur, s, 0))

    def rows_after(s):
        if s == tm:
            return nxt
        return jnp.where(row >= tm - s, pltpu.roll(nxt, tm - s, 0), pltpu.roll(cur, tm - s, 0))

    if grid_shift:
        tw = t % GRID_W
        left = jnp.where(tw == 0, 0.0, rows_before(nb))
        right = jnp.where(tw == GRID_W - 1, 0.0, rows_after(nb))
        up = jnp.where(t < GRID_W, 0.0, rows_before(nb * GRID_W))
        down = jnp.where(t >= seq - GRID_W, 0.0, rows_after(nb * GRID_W))
        m4 = lane % 4
        sh = jnp.where(m4 == 0, left, jnp.where(m4 == 1, right, jnp.where(m4 == 2, up, down)))
    else:
        before = jnp.where(t == 0, 0.0, rows_before(nb))
        after = jnp.where(t == seq - 1, 0.0, rows_after(nb))
        sh = jnp.where(lane % 2 == 0, before, after)

    z = cur + (sh - cur) * mu_ref[...]
    r = z[:, 0:D_RWKV]
    k = z[:, D_RWKV:2 * D_RWKV]
    v = z[:, 2 * D_RWKV:3 * D_RWKV]
    xw = z[:, 3 * D_RWKV:3 * D_RWKV + LORA_W]
    xa = z[:, 3 * D_RWKV + LORA_W:3 * D_RWKV + LORA_W + LORA_A]
    xg = z[:, 3 * D_RWKV + LORA_W + LORA_A:]
    ones = ones_ref[...]

    g_ref[...] = _dot(_sigmoid(xg), g2_ref[...])
    kk = k * kk_ref[...]
    nrm = jnp.sqrt(_dot_split(kk * kk, ones))
    kk = kk / jnp.maximum(nrm, 1e-12)
    r_ref[...] = r
    v_ref[...] = v
    nkk_ref[...] = -kk
    bonus_ref[...] = _dot_split(r * k * rk_ref[...], ones) * v
    txw = jnp.tanh(xw)
    for d in range(2):
        zw = -(w0_ref[d] + _dot(txw, w2_ref[d]))
        softplus = jnp.maximum(zw, 0.0) + jnp.log(1.0 + jnp.exp(-jnp.abs(zw)))
        w_log = -softplus - 0.5
        w_ref[d] = jnp.exp(-jnp.exp(w_log))
        a = _sigmoid(a0_ref[d] + _dot(xa, a2_ref[d]))
        kd_ref[d] = k * (1.0 + (a - 1.0) * ka_ref[...])
        b_ref[d] = kk * a


def _rwkv_prep(zr, p, nb, seq, grid_shift, row0, prev=None, tm=128):
    rows = seq * nb
    nt = rows // tm
    t0 = row0 // tm
    if grid_shift:
        assert tm == nb * GRID_W, "one tile must be one grid row of the latent grid"
    else:
        assert tm % nb == 0 and tm >= nb
    kern = functools.partial(_rwkv_prep_kernel, nb=nb, seq=seq, grid_shift=grid_shift)
    c = D_RWKV
    vec = lambda n: pl.BlockSpec((1, n), lambda i: (0, 0))
    mat = lambda a, b: pl.BlockSpec((a, b), lambda i: (0, 0))
    row_out = pl.BlockSpec((tm, c), lambda i: (i, 0))
    full_out = pl.BlockSpec((tm, c), lambda i: (t0 + i, 0))
    dir_out = pl.BlockSpec((2, tm, c), lambda i: (0, i, 0))
    head_ones = jnp.kron(jnp.eye(H_RWKV, dtype=F32), jnp.ones((RWKV_HEAD, RWKV_HEAD), F32)).astype(BF16)
    in_specs = [pl.BlockSpec((tm, RWKV_COLS), lambda i: (t0 + jnp.maximum(i - 1, 0), 0)),
                pl.BlockSpec((tm, RWKV_COLS), lambda i: (t0 + i, 0)),
                pl.BlockSpec((tm, RWKV_COLS), lambda i: (t0 + jnp.minimum(i + 1, nt - 1), 0)),
                vec(RWKV_COLS), mat(c, c), mat(LORA_G, c), vec(c), vec(c), vec(c),
                pl.BlockSpec((2, 1, c), lambda i: (0, 0, 0)),
                pl.BlockSpec((2, LORA_W, c), lambda i: (0, 0, 0)),
                pl.BlockSpec((2, 1, c), lambda i: (0, 0, 0)),
                pl.BlockSpec((2, LORA_A, c), lambda i: (0, 0, 0))]
    args = [zr, zr, zr, p['rw_mu'].reshape(1, -1), head_ones, p['rw_g2'], p['rw_k_k'].reshape(1, c),
            p['rw_k_a'].reshape(1, c), p['rw_r_k'].reshape(1, c), p['rw_w0'].reshape(2, 1, c), p['rw_w2'],
            p['rw_a0'].reshape(2, 1, c), p['rw_a2']]
    aliases = {}
    if prev is not None:
        n_in = len(args)
        in_specs += [pl.BlockSpec(memory_space=pl.ANY)] * 2
        args += list(prev)
        aliases = {n_in: 3, n_in + 1: 4}
        kern = functools.partial(_ignore_last_input, functools.partial(_ignore_last_input, kern, n_in + 1),
                                 n_in + 2)
    full = jax.ShapeDtypeStruct((zr.shape[0], c), F32)
    part = jax.ShapeDtypeStruct((rows, c), F32)
    return pl.pallas_call(
        kern,
        grid=(nt,),
        in_specs=in_specs,
        out_specs=[row_out] * 3 + [full_out] * 2 + [dir_out] * 3,
        out_shape=[part] * 3 + [full] * 2 + [jax.ShapeDtypeStruct((2, rows, c), F32)] * 3,
        input_output_aliases=aliases,
        compiler_params=_params(("arbitrary",)),
        name="rwkv_prep",
    )(*args)


def _rwkv_scan_kernel(a_ref, w_ref, b_ref, k_ref, r_ref, v_ref, s0_ref, y_ref, sfin_ref, s_ref, *, nacc, folded):
    d = pl.program_id(0)
    c = pl.program_id(1)
    nc = pl.num_programs(1)
    tc = a_ref.shape[1]
    nj = RWKV_HEAD
    rev = (c >= nc // 2) if folded else (d == 1)

    @pl.when(c == 0)
    def _():
        s_ref[...] = s0_ref[0]

    def step(s, carry):
        t = jnp.where(rev, tc - 1 - s, s)
        acc = [None] * nacc
        for j in range(nj):
            term = s_ref[j] * a_ref[0, t, pl.ds(j, 1), :]
            acc[j % nacc] = term if acc[j % nacc] is None else acc[j % nacc] + term
        sa = functools.reduce(lambda x, y: x + y, acc)
        vv = v_ref[0, t]
        acc = [None] * nacc
        for j in range(nj):
            s_new = (s_ref[j] * w_ref[0, t, pl.ds(j, 1), :]
                     + sa * b_ref[0, t, pl.ds(j, 1), :]
                     + vv * k_ref[0, t, pl.ds(j, 1), :])
            s_ref[j] = s_new
            term = s_new * r_ref[0, t, pl.ds(j, 1), :]
            acc[j % nacc] = term if acc[j % nacc] is None else acc[j % nacc] + term
        y_ref[0, t] = functools.reduce(lambda x, y: x + y, acc)
        return carry

    lax.fori_loop(0, tc, step, 0)

    @pl.when(c == nc - 1)
    def _():
        sfin_ref[0] = s_ref[...]


def _rwkv_scan(a, w, b, k, r, v, s0, tc, folded):
    _, tp, _, nl = w.shape
    iv = v.shape[2]
    nacc = 2 if iv >= 64 else 4
    nc_half = tp // tc
    if folded:
        nd, nc = 1, 2 * nc_half
        tmap = lambda d, c: (c // nc_half, jnp.where(c < nc_half, c, nc - 1 - c), 0, 0)
        shared = per_dir = tmap
    else:
        nd, nc = 2, nc_half
        chunk = lambda d, c: c + d * (nc - 1 - 2 * c)
        shared = lambda d, c: (0, chunk(d, c), 0, 0)
        per_dir = lambda d, c: (d, chunk(d, c), 0, 0)
    key = lambda m: pl.BlockSpec((1, tc, RWKV_HEAD, nl), m)
    val = lambda m: pl.BlockSpec((1, tc, iv, nl), m)
    sspec = pl.BlockSpec((1, RWKV_HEAD, iv, nl), lambda d, c: (d, 0, 0, 0))
    return pl.pallas_call(
        functools.partial(_rwkv_scan_kernel, nacc=nacc, folded=folded),
        grid=(nd, nc),
        in_specs=[key(shared), key(per_dir), key(per_dir), key(per_dir), key(shared), val(shared), sspec],
        out_specs=[val(per_dir), sspec],
        out_shape=[jax.ShapeDtypeStruct((2, tp, iv, nl), F32),
                   jax.ShapeDtypeStruct((nd, RWKV_HEAD, iv, nl), F32)],
        scratch_shapes=[pltpu.VMEM((RWKV_HEAD, iv, nl), F32)],
        compiler_params=_params(("arbitrary", "arbitrary")),
        name="rwkv_scan",
    )(a, w, b, k, r, v, s0)


SMP_REP = 4
SMP_ROWS = RWKV_HEAD // SMP_REP


def _block_transpose(tiles, bs):
    tiles = list(tiles)
    lane = lax.broadcasted_iota(jnp.int32, tiles[0].shape, 1)
    for kbit in range(3):
        sft = bs << kbit
        bit = (lane // sft) % 2 == 1
        for r in range(8):
            if (r >> kbit) & 1:
                continue
            r2 = r | (1 << kbit)
            lo, hi = tiles[r], tiles[r2]
            tiles[r] = jnp.where(bit, pltpu.roll(hi, sft, 1), lo)
            tiles[r2] = jnp.where(bit, hi, pltpu.roll(lo, LANES - sft, 1))
    return tiles


def _head_tiles(x):
    xt = x.T
    return [xt[h * RWKV_HEAD:(h + 1) * RWKV_HEAD, :] for h in range(H_RWKV)]


def _from_head_tiles(tiles):
    return jnp.concatenate(tiles, axis=0).T


def _relayout_ctx_kernel(r_ref, v_ref, nkk_ref, w_ref, kd_ref, b_ref, a_o, r_o, v_o, w_o, b_o, k_o, *, nb):
    steps = a_o.shape[1]
    for src, dst in ((nkk_ref, a_o), (r_ref, r_o), (v_ref, v_o)):
        tiles = _block_transpose(_head_tiles(src[...]), nb)
        for t in range(steps):
            dst[0, t] = tiles[t]
    for src, dst in ((w_ref, w_o), (b_ref, b_o), (kd_ref, k_o)):
        for d in range(2):
            tiles = _block_transpose(_head_tiles(src[d]), nb)
            for t in range(steps):
                dst[d, t] = tiles[t]


def _relayout_ctx(prep, nb, seq):
    r, v, nkk, _, _, w, kd, bb = prep
    steps = LANES // nb
    assert steps == H_RWKV and nb * H_RWKV == LANES
    tm = steps * nb
    nt = seq // steps
    c = D_RWKV
    nat = pl.BlockSpec((tm, c), lambda i: (i, 0))
    nat2 = pl.BlockSpec((2, tm, c), lambda i: (0, i, 0))
    out1 = pl.BlockSpec((1, steps, RWKV_HEAD, LANES), lambda i: (0, i, 0, 0))
    out2 = pl.BlockSpec((2, steps, RWKV_HEAD, LANES), lambda i: (0, i, 0, 0))
    s1 = jax.ShapeDtypeStruct((1, seq, RWKV_HEAD, LANES), F32)
    s2 = jax.ShapeDtypeStruct((2, seq, RWKV_HEAD, LANES), F32)
    a, r_, v_, w_, b_, k_ = pl.pallas_call(
        functools.partial(_relayout_ctx_kernel, nb=nb),
        grid=(nt,),
        in_specs=[nat, nat, nat, nat2, nat2, nat2],
        out_specs=[out1, out1, out1, out2, out2, out2],
        out_shape=[s1, s1, s1, s2, s2, s2],
        compiler_params=_params(("arbitrary",)),
        name="rwkv_relayout_ctx",
    )(r, v, nkk, w, kd, bb)
    return a, w_, b_, k_, r_, v_


def _unmix_ctx_kernel(y_ref, o_ref, *, nb):
    steps = y_ref.shape[1]
    tiles = [y_ref[0, t] + y_ref[1, t] for t in range(steps)]
    o_ref[...] = _from_head_tiles(_block_transpose(tiles, nb))


def _unmix_ctx(y, nb, seq, rows_total):
    steps = LANES // nb
    tm = steps * nb
    return pl.pallas_call(
        functools.partial(_unmix_ctx_kernel, nb=nb),
        grid=(seq // steps,),
        in_specs=[pl.BlockSpec((2, steps, RWKV_HEAD, LANES), lambda i: (0, i, 0, 0))],
        out_specs=pl.BlockSpec((tm, D_RWKV), lambda i: (i, 0)),
        out_shape=jax.ShapeDtypeStruct((rows_total, D_RWKV), F32),
        compiler_params=_params(("arbitrary",)),
        name="rwkv_unmix_ctx",
    )(y)


def _relayout_smp_kernel(ra, rb, va, vb, na, nb_, wa, wb, ka, kb, ba, bb, sel_ref,
                         a_o, r_o, v_o, w_o, b_o, k_o):
    steps = a_o.shape[1]
    lane_v = lax.broadcasted_iota(jnp.int32, (SMP_ROWS, LANES), 1)

    def conv(xa, xb):
        tiles = _block_transpose(_head_tiles(jnp.concatenate([xa, xb], axis=0)), 2)
        parts = []
        for x in tiles:
            hi = x.astype(BF16)
            r1 = x - hi.astype(F32)
            mid = r1.astype(BF16)
            lo = (r1 - mid.astype(F32)).astype(BF16)
            parts.append((hi, mid, lo))
        return parts

    def expand(o0, o1, half):
        order0 = range(8) if half == 0 else range(7, -1, -1)
        order1 = range(7, -1, -1) if half == 0 else range(8)
        cols = []
        for term in range(3):
            cols.append(jnp.concatenate([o0[s][term] for s in order0], axis=0))
            cols.append(jnp.concatenate([o1[s][term] for s in order1], axis=0))
        lhs = jnp.concatenate(cols, axis=1)
        out = [None] * steps
        for pair in range(steps // 16):
            res = jnp.dot(lhs, sel_ref[half, pair], preferred_element_type=F32)
            for q in range(2):
                for s_lo in range(8):
                    out[8 * (2 * pair + q) + s_lo] = res[64 * s_lo:64 * (s_lo + 1), LANES * q:LANES * (q + 1)]
        return out

    def value_rows(full):
        blk = lane_v // (LANES // SMP_REP)
        rows = [full[k * SMP_ROWS:(k + 1) * SMP_ROWS] for k in range(SMP_REP)]
        return jnp.where(blk == 0, rows[0], jnp.where(blk == 1, rows[1], jnp.where(blk == 2, rows[2], rows[3])))

    for srca, srcb, dst in ((na, nb_, a_o), (ra, rb, r_o), (va, vb, v_o)):
        o = conv(srca[...], srcb[...])
        for half in range(2):
            tiles = expand(o, o, half)
            for s in range(steps):
                dst[half, s] = value_rows(tiles[s]) if dst is v_o else tiles[s]
    for srca, srcb, dst in ((wa, wb, w_o), (ba, bb, b_o), (ka, kb, k_o)):
        o0 = conv(srca[0], srcb[0])
        o1 = conv(srca[1], srcb[1])
        for half in range(2):
            tiles = expand(o0, o1, half)
            for s in range(steps):
                dst[half, s] = tiles[s]


def _smp_selection():
    sel = np.zeros((2, 2, 6, LANES, 2 * LANES), np.float32)
    for half in range(2):
        for pair in range(2):
            for q in range(2):
                s_hi = 2 * pair + q
                grp_a = 0 * 64 + 16 * s_hi
                grp_b = 1 * 64 + 16 * (3 - s_hi)
                src = (grp_a, grp_b) if half == 0 else (grp_b, grp_a)
                for d in range(2):
                    for blk in range(SMP_REP):
                        for hb in range(16):
                            col = q * LANES + blk * 32 + d * 16 + hb
                            for term in range(3):
                                sel[half, pair, 2 * term + d, src[d] + hb, col] = 1.0
    return jnp.asarray(sel.reshape(2, 2, 6 * LANES, 2 * LANES), BF16)


def _relayout_smp(prep, nb, seq, row0):
    r, v, nkk, _, _, w, kd, bb = prep
    assert 2 * nb * H_RWKV * SMP_REP == LANES
    steps = 32
    tm = steps * nb
    nt = seq // steps
    half_t = seq // 2
    c = D_RWKV
    t0 = row0 // tm
    blk_a = pl.BlockSpec((tm, c), lambda g: (t0 + g, 0))
    blk_b = pl.BlockSpec((tm, c), lambda g: (t0 + nt - 1 - g, 0))
    blk2_a = pl.BlockSpec((2, tm, c), lambda g: (0, t0 + g, 0))
    blk2_b = pl.BlockSpec((2, tm, c), lambda g: (0, t0 + nt - 1 - g, 0))
    okey = pl.BlockSpec((2, steps, RWKV_HEAD, LANES), lambda g: (0, g, 0, 0))
    oval = pl.BlockSpec((2, steps, SMP_ROWS, LANES), lambda g: (0, g, 0, 0))
    skey = jax.ShapeDtypeStruct((2, half_t, RWKV_HEAD, LANES), F32)
    sval = jax.ShapeDtypeStruct((2, half_t, SMP_ROWS, LANES), F32)
    a, r_, v_, w_, b_, k_ = pl.pallas_call(
        _relayout_smp_kernel,
        grid=(nt // 2,),
        in_specs=[blk_a, blk_b, blk_a, blk_b, blk_a, blk_b, blk2_a, blk2_b, blk2_a, blk2_b, blk2_a, blk2_b,
                  pl.BlockSpec((2, 2, 6 * LANES, 2 * LANES), lambda g: (0, 0, 0, 0))],
        out_specs=[okey, okey, oval, okey, okey, okey],
        out_shape=[skey, skey, sval, skey, skey, skey],
        compiler_params=_params(("arbitrary",)),
        name="rwkv_relayout_smp",
    )(r, r, v, v, nkk, nkk, w, w, kd, kd, bb, bb, _smp_selection())
    return a, w_, b_, k_, r_, v_


def _unmix_smp_kernel(y_ref, prev_ref, o_ref, *, half_tiles):
    del prev_ref
    q = pl.program_id(0)
    steps = y_ref.shape[1]
    low = q < half_tiles
    h0 = jnp.where(low, 0, 1)
    lane = lax.broadcasted_iota(jnp.int32, (SMP_ROWS, LANES), 1)
    tiles = []
    for s_lo in range(8):
        blocks = []
        for blk in range(SMP_REP):
            acc = jnp.zeros((SMP_ROWS, LANES), F32)
            for s_hi in range(steps // 8):
                s = 8 * s_hi + s_lo
                u = jnp.where(low, s, steps - 1 - s)
                d0 = y_ref[h0, u]
                d1 = y_ref[1 - h0, u]
                sh0 = (16 * s_hi - 32 * blk) % LANES
                sh1 = (16 * s_hi - 32 * blk - 16) % LANES
                t0 = d0 if sh0 == 0 else pltpu.roll(d0, sh0, 1)
                t1 = d1 if sh1 == 0 else pltpu.roll(d1, sh1, 1)
                acc = jnp.where(lane // 16 == s_hi, t0 + t1, acc)
            blocks.append(acc)
        tiles.append(jnp.concatenate(blocks, axis=0))
    o_ref[...] = _from_head_tiles(_block_transpose(tiles, 2))


def _unmix_smp(y, prev, nb, seq, row0):
    steps = 64
    tm = steps * nb
    assert tm == LANES
    nq = seq // steps
    half_tiles = nq // 2
    t0 = row0 // tm
    return pl.pallas_call(
        functools.partial(_unmix_smp_kernel, half_tiles=half_tiles),
        grid=(nq,),
        in_specs=[pl.BlockSpec((2, steps, SMP_ROWS, LANES),
                               lambda q: (0, jnp.where(q < half_tiles, q, nq - 1 - q), 0, 0)),
                  pl.BlockSpec(memory_space=pl.ANY)],
        out_specs=pl.BlockSpec((tm, D_RWKV), lambda q: (t0 + q, 0)),
        out_shape=jax.ShapeDtypeStruct(prev.shape, F32),
        input_output_aliases={1: 0},
        compiler_params=_params(("arbitrary",)),
        name="rwkv_unmix_smp",
    )(y, prev)


def _rwkv_state_smp(state_l, nb):
    s = state_l.reshape(nb, 2, H_RWKV, SMP_REP, SMP_ROWS, RWKV_HEAD)
    s = s.transpose(5, 4, 3, 1, 2, 0)
    return s.reshape(1, RWKV_HEAD, SMP_ROWS, LANES)


def _rwkv_state_out_ctx(sfin, nb):
    s = sfin.reshape(2, RWKV_HEAD, RWKV_HEAD, H_RWKV, nb)
    return s.transpose(4, 0, 3, 2, 1)


def _mix_out_kernel(x_ref, pat_ref, u_ref, y5_ref, gs_ref, gr_ref, yrw_ref, bonus_ref, g_ref,
                    d_ref, wglu_ref, ones_ref, lnxw_ref, lnxb_ref, wups_ref, wupr_ref, wout_ref,
                    ln1g_ref, ln1b_ref, o_ref):
    y5 = u_ref[...] * d_ref[...] + y5_ref[0] + y5_ref[1]
    y5 = jax.nn.gelu(y5)
    y5 = y5 * _sigmoid(_dot(y5, wglu_ref[...]))
    ones = ones_ref[...]
    yr = yrw_ref[...]
    inv_n = 1.0 / RWKV_HEAD
    mu = _dot_split(yr, ones) * inv_n
    yc = yr - mu
    var = _dot_split(yc * yc, ones) * inv_n
    yr = yc * lax.rsqrt(var + GN_EPS) * lnxw_ref[...] + lnxb_ref[...]
    yr = (yr + bonus_ref[...]) * g_ref[...]
    merged = (_sigmoid(gs_ref[...]) * _dot(y5, wups_ref[...])
              + _sigmoid(gr_ref[...]) * _dot(yr, wupr_ref[...]))
    mix = _dot(merged, wout_ref[...])
    x3 = _modulate(mix, jnp.zeros_like(pat_ref[0, 2]), pat_ref[0, 2] - 1.0)
    o_ref[...] = _layer_norm(ALPHA * x_ref[...] + x3, ln1g_ref[...], ln1b_ref[...])


def _mix_out(x, pat, u, y5, gs, gr, yrw, bonus, g, p, rows_per_trunk, tm=256):
    rows = x.shape[0]
    d = D_MODEL
    c = D_RWKV
    tpt = rows_per_trunk // tm
    rowblk = lambda n: pl.BlockSpec((tm, n), lambda i: (i, 0))
    vec = lambda n: pl.BlockSpec((1, n), lambda i: (0, 0))
    mat = lambda a, b: pl.BlockSpec((a, b), lambda i: (0, 0))
    head_ones = jnp.kron(jnp.eye(H_RWKV, dtype=F32), jnp.ones((RWKV_HEAD, RWKV_HEAD), F32)).astype(BF16)
    return pl.pallas_call(
        _mix_out_kernel,
        grid=(rows // tm,),
        in_specs=[rowblk(d),
                  pl.BlockSpec((1, 6, SUBLANES, d), lambda i: (i // tpt, 0, 0, 0)),
                  rowblk(c), pl.BlockSpec((2, tm, c), lambda i: (0, i, 0)),
                  rowblk(d), rowblk(d), rowblk(c), rowblk(c), rowblk(c),
                  vec(c), mat(c, c), mat(c, c), vec(c), vec(c), mat(c, d), mat(c, d), mat(d, d),
                  vec(d), vec(d)],
        out_specs=rowblk(d),
        out_shape=jax.ShapeDtypeStruct((rows, d), F32),
        compiler_params=_params(("arbitrary",)),
        name="mix_out",
    )(x, pat, u, y5, gs, gr, yrw, bonus, g,
      p['s5_d'].reshape(1, c), p['s5_w_glu'].astype(BF16), head_ones,
      p['rw_lnx_w'].reshape(1, c), p['rw_lnx_b'].reshape(1, c),
      p['w_up_s5'].astype(BF16), p['w_up_rwkv'].astype(BF16), p['w_out'].astype(BF16),
      p['ln1_g'].reshape(1, d), p['ln1_b'].reshape(1, d))


def _first_max(x, lane, valid):
    xm = jnp.where(valid, x, -jnp.inf)
    m = jnp.max(xm, -1, keepdims=True)
    idx = jnp.min(jnp.where(xm == m, lane, float(N_EXPERTS)), -1, keepdims=True)
    return m, idx


def _moe_kernel(x_ref, pat_ref, wr_ref, br_ref, wg_ref, wu_ref, wd_ref, ln2g_ref, ln2b_ref, o_ref,
                hb_ref, comb_ref, acc_ref):
    e = pl.program_id(1)
    ne = pl.num_programs(1)
    tm = x_ref.shape[0]

    @pl.when(e == 0)
    def _():
        h = _modulate(x_ref[...], pat_ref[0, 3], pat_ref[0, 4])
        hb_ref[...] = h.astype(BF16)
        wr = wr_ref[...]
        w_hi = wr.astype(BF16)
        w_lo = (wr - w_hi.astype(F32)).astype(BF16)
        h_hi = h.astype(BF16)
        h_lo = (h - h_hi.astype(F32)).astype(BF16)
        logits = (jnp.dot(h_hi, w_hi, preferred_element_type=F32)
                  + jnp.dot(h_hi, w_lo, preferred_element_type=F32)
                  + jnp.dot(h_lo, w_hi, preferred_element_type=F32)) + br_ref[...]
        logits = logits - jnp.max(logits, -1, keepdims=True)
        ex = jnp.exp(logits)
        probs = ex / jnp.sum(ex, -1, keepdims=True)
        lane_i = lax.broadcasted_iota(jnp.int32, (tm, N_EXPERTS), 1)
        lane = lane_i.astype(F32)
        grp = (lane_i // EXPERTS_PER_GROUP).astype(F32)
        best_score = jnp.full((tm, 1), -jnp.inf, F32)
        best_grp = jnp.zeros((tm, 1), F32)
        for gi in range(N_GROUPS):
            in_g = grp == float(gi)
            m1, i1 = _first_max(probs, lane, in_g)
            m2, _ = _first_max(probs, lane, in_g & (lane != i1))
            score = m1 + m2
            better = score > best_score
            best_score = jnp.where(better, score, best_score)
            best_grp = jnp.where(better, float(gi), best_grp)
        in_best = grp == best_grp
        m1, i1 = _first_max(probs, lane, in_best)
        m2, i2 = _first_max(probs, lane, in_best & (lane != i1))
        tot = m1 + m2
        comb_ref[...] = jnp.where(lane == i1, m1 / tot, 0.0) + jnp.where(lane == i2, m2 / tot, 0.0)
        acc_ref[...] = jnp.zeros_like(acc_ref)

    hb = hb_ref[...]
    lane = lax.broadcasted_iota(jnp.int32, (tm, N_EXPERTS), 1)
    ce = jnp.sum(jnp.where(lane == e, comb_ref[...], 0.0), -1, keepdims=True)
    hid = _silu(_dot(hb, wg_ref[0])) * _dot(hb, wu_ref[0])
    acc_ref[...] += _dot(hid * ce, wd_ref[0])

    @pl.when(e == ne - 1)
    def _():
        ffn = _modulate(acc_ref[...], jnp.zeros_like(pat_ref[0, 5]), pat_ref[0, 5] - 1.0)
        o_ref[...] = _layer_norm(ALPHA * x_ref[...] + ffn, ln2g_ref[...], ln2b_ref[...])


def _moe(x, pat, w_router, b_router, wg, wu, wd, ln2_g, ln2_b, rows_per_trunk, tm=1024):
    rows = x.shape[0]
    tm = min(tm, rows_per_trunk)
    d = D_MODEL
    tpt = rows_per_trunk // tm
    rowblk = pl.BlockSpec((tm, d), lambda i, e: (i, 0))
    vec = lambda n: pl.BlockSpec((1, n), lambda i, e: (0, 0))
    return pl.pallas_call(
        _moe_kernel,
        grid=(rows // tm, N_EXPERTS),
        in_specs=[rowblk,
                  pl.BlockSpec((1, 6, SUBLANES, d), lambda i, e: (i // tpt, 0, 0, 0)),
                  pl.BlockSpec((d, N_EXPERTS), lambda i, e: (0, 0)), vec(N_EXPERTS),
                  pl.BlockSpec((1, d, D_EXPERT), lambda i, e: (e, 0, 0)),
                  pl.BlockSpec((1, d, D_EXPERT), lambda i, e: (e, 0, 0)),
                  pl.BlockSpec((1, D_EXPERT, d), lambda i, e: (e, 0, 0)),
                  vec(d), vec(d)],
        out_specs=rowblk,
        out_shape=jax.ShapeDtypeStruct((rows, d), F32),
        scratch_shapes=[pltpu.VMEM((tm, d), BF16), pltpu.VMEM((tm, N_EXPERTS), F32),
                        pltpu.VMEM((tm, d), F32)],
        compiler_params=_params(("arbitrary", "arbitrary")),
        name="moe",
    )(x, pat, w_router, b_router.reshape(1, N_EXPERTS), wg, wu, wd,
      ln2_g.reshape(1, d), ln2_b.reshape(1, d))


def _block_diag_in(bb):
    nd, g, p, n = bb.shape
    eye = jnp.eye(g, dtype=bb.dtype)
    return jnp.einsum('dgpn,gh->dgphn', bb, eye).reshape(nd, g * p, g * n)


def _block_diag_out(cc):
    nd, g, p, n = cc.shape
    eye = jnp.eye(g, dtype=cc.dtype)
    return jnp.einsum('dgpn,gh->dgnhp', cc, eye).reshape(nd, g * n, g * p)


def kernel(x_prompt, x_sample, state_s5, state_rwkv, c, c_ctx, w_ada, b_ada, w_in, s5_a_re, s5_a_im, s5_log_dt, s5_b_re, s5_b_im, s5_c_re, s5_c_im, s5_d, s5_w_glu, rw_mu, rw_w0, rw_w2, rw_a0, rw_a2, rw_g2, rw_k_k, rw_k_a, rw_r_k, rw_lnx_w, rw_lnx_b, w_up_s5, w_up_rwkv, w_out, ln1_g, ln1_b, ln2_g, ln2_b, w_router, b_router, w_exp_gate, w_exp_up, w_exp_down):
    nbc, tc_len, d = x_prompt.shape
    nbs, ts_len, _ = x_sample.shape
    nl = w_ada.shape[0]
    rc = nbc * tc_len
    rs = nbs * ts_len
    assert rc == rs, "both trunks are processed as equal halves of one row-major token matrix"
    assert SUBLANES % nbs == 0 and nbc % SUBLANES == 0

    x = jnp.concatenate([x_prompt.transpose(1, 0, 2).reshape(rc, d),
                         x_sample.transpose(1, 0, 2).reshape(rs, d)], axis=0)

    cond = jnp.concatenate([c_ctx[None], c], axis=0)
    cond8 = jnp.zeros((SUBLANES, d), F32).at[:cond.shape[0]].set(cond)
    mod = _ada_mod(cond8, w_ada, b_ada).reshape(nl, SUBLANES, 6, d)
    ctx_rows = jnp.zeros((SUBLANES,), jnp.int32)
    smp_rows = 1 + jnp.arange(SUBLANES, dtype=jnp.int32) % nbs
    pat_idx = jnp.stack([ctx_rows, smp_rows])
    pats = mod[:, pat_idx]
    pats = pats.transpose(0, 1, 3, 2, 4)

    m = nl * 2 * G_S5
    ab_re, ab_im, bb_re, bb_im = _s5_discretize(
        s5_a_re.reshape(m, N_S5), s5_a_im.reshape(m, N_S5), s5_log_dt.reshape(m, 1),
        s5_b_re.reshape(m, N_S5, S5_GROUP).transpose(0, 2, 1),
        s5_b_im.reshape(m, N_S5, S5_GROUP).transpose(0, 2, 1))
    ab_re = ab_re.reshape(nl, 2, 1, S5_STATE)
    ab_im = ab_im.reshape(nl, 2, 1, S5_STATE)
    bb_re = bb_re.reshape(nl, 2, G_S5, S5_GROUP, N_S5)
    bb_im = bb_im.reshape(nl, 2, G_S5, S5_GROUP, N_S5)

    zero_s5 = jnp.zeros((2, 2, nbc, S5_STATE), F32)
    zero_rw = jnp.zeros((2, RWKV_HEAD, RWKV_HEAD, nbc * H_RWKV), F32)
    s5_out, rw_out = [], []
    for l in range(nl):
        p = dict(s5_d=s5_d[l], s5_w_glu=s5_w_glu[l], rw_mu=rw_mu[l], rw_w0=rw_w0[l], rw_w2=rw_w2[l],
                 rw_a0=rw_a0[l], rw_a2=rw_a2[l], rw_g2=rw_g2[l], rw_k_k=rw_k_k[l], rw_k_a=rw_k_a[l],
                 rw_r_k=rw_r_k[l], rw_lnx_w=rw_lnx_w[l], rw_lnx_b=rw_lnx_b[l], w_up_s5=w_up_s5[l],
                 w_up_rwkv=w_up_rwkv[l], w_out=w_out[l], ln1_g=ln1_g[l], ln1_b=ln1_b[l])
        pat = pats[l]
        u, zr, gs, gr = _inproj(x, pat, w_in[l], rc)

        wb_re = _block_diag_in(bb_re[l]).astype(BF16)
        wb_im = _block_diag_in(bb_im[l]).astype(BF16)
        wc = jnp.concatenate([_block_diag_out(s5_c_re[l]), -_block_diag_out(s5_c_im[l])], axis=1).astype(BF16)
        y5, hfin = _s5_scan(u, wb_re, wb_im, ab_re[l], ab_im[l], wc, zero_s5, nbc, 0, rc)
        h0s = state_s5[:, l].reshape(nbs, 2, 2, S5_STATE).transpose(1, 2, 0, 3)
        y5, _ = _s5_scan(u, wb_re, wb_im, ab_re[l], ab_im[l], wc, h0s, nbs, rc, rs, prev=y5)
        s5_out.append(hfin.transpose(2, 0, 1, 3).reshape(nbc, 2, 2, G_S5, N_S5))

        prep_c = _rwkv_prep(zr, p, nbc, tc_len, False, 0)
        prep_s = _rwkv_prep(zr, p, nbs, ts_len, True, rc, prev=(prep_c[3], prep_c[4]))
        bonus, g = prep_s[3], prep_s[4]
        yc, sfin = _rwkv_scan(*_relayout_ctx(prep_c, nbc, tc_len), zero_rw, tc=min(32, tc_len), folded=False)
        ys, _ = _rwkv_scan(*_relayout_smp(prep_s, nbs, ts_len, 0), _rwkv_state_smp(state_rwkv[:, l], nbs),
                           tc=min(64, ts_len // 2), folded=True)
        yrw = _unmix_smp(ys, _unmix_ctx(yc, nbc, tc_len, rc + rs), nbs, ts_len, rc)
        rw_out.append(_rwkv_state_out_ctx(sfin, nbc))

        x = _mix_out(x, pat, u, y5, gs, gr, yrw, bonus, g, p, rc)
        x = _moe(x, pat, w_router, b_router, w_exp_gate[l], w_exp_up[l], w_exp_down[l], ln2_g[l], ln2_b[l], rc)

    y_prompt = x[:rc].reshape(tc_len, nbc, d).transpose(1, 0, 2)
    y_sample = x[rc:].reshape(ts_len, nbs, d).transpose(1, 0, 2)
    return (y_prompt, y_sample, jnp.stack(s5_out, 1), jnp.stack(rw_out, 1))
```

```python
import functools

import numpy as np
import jax
import jax.numpy as jnp
from jax import lax
from jax.experimental import pallas as pl
from jax.experimental.pallas import tpu as pltpu

D_MODEL = 1024
DEPTH = 2
GRID_W = 64
D_S5 = 512
S5_GROUP = 16
G_S5 = 32
N_S5 = 64
S5_STATE = G_S5 * N_S5
D_RWKV = 512
RWKV_HEAD = 64
H_RWKV = 8
LORA_W = 64
LORA_A = 64
LORA_G = 128
RWKV_COLS = 3 * D_RWKV + LORA_W + LORA_A + LORA_G
N_EXPERTS = 16
N_GROUPS = 4
EXPERTS_PER_GROUP = 4
D_EXPERT = 512
ALPHA = (2 * DEPTH) ** 0.25
LN_EPS = 1e-5
GN_EPS = 64e-5

SUBLANES = 8
LANES = 128
VMEM_LIMIT = 56 * 1024 * 1024

F32 = jnp.float32
BF16 = jnp.bfloat16


def _params(sem):
    return pltpu.CompilerParams(dimension_semantics=sem, vmem_limit_bytes=VMEM_LIMIT)


def _dot(a, b):
    return jnp.dot(a.astype(BF16), b.astype(BF16), preferred_element_type=F32)


def _dot_split(x, w_exact):
    hi = x.astype(BF16)
    lo = (x - hi.astype(F32)).astype(BF16)
    return (jnp.dot(hi, w_exact, preferred_element_type=F32)
            + jnp.dot(lo, w_exact, preferred_element_type=F32))


def _sigmoid(x):
    return 1.0 / (1.0 + jnp.exp(-x))


def _silu(x):
    return x * _sigmoid(x)


def _layer_norm(x, g, b):
    mu = jnp.mean(x, -1, keepdims=True)
    xc = x - mu
    var = jnp.mean(xc * xc, -1, keepdims=True)
    return xc * lax.rsqrt(var + LN_EPS) * g + b


def _modulate(x, shift8, scale8):
    rows, d = x.shape
    x3 = x.reshape(rows // SUBLANES, SUBLANES, d)
    return (x3 * (1.0 + scale8)[None] + shift8[None]).reshape(rows, d)


def _ada_kernel(cond_ref, w_ref, b_ref, o_ref):
    c = cond_ref[...]
    o_ref[0] = _dot(_silu(c), w_ref[0]) + b_ref[0]


def _ada_mod(cond8, w_ada, b_ada):
    nl = w_ada.shape[0]
    d = D_MODEL
    return pl.pallas_call(
        _ada_kernel,
        grid=(nl, 6),
        in_specs=[pl.BlockSpec((SUBLANES, d), lambda l, k: (0, 0)),
                  pl.BlockSpec((1, d, d), lambda l, k: (l, 0, k)),
                  pl.BlockSpec((1, 1, d), lambda l, k: (l, 0, k))],
        out_specs=pl.BlockSpec((1, SUBLANES, d), lambda l, k: (l, 0, k)),
        out_shape=jax.ShapeDtypeStruct((nl, SUBLANES, 6 * d), F32),
        compiler_params=_params(("arbitrary", "arbitrary")),
        name="ada_mod",
    )(cond8, w_ada, b_ada.reshape(nl, 1, 6 * d))


IN_SPLITS = (0, D_S5, D_S5 + RWKV_COLS, D_S5 + RWKV_COLS + D_MODEL, D_S5 + RWKV_COLS + 2 * D_MODEL)


def _inproj_kernel(x_ref, pat_ref, w_ref, u_ref, zr_ref, gs_ref, gr_ref):
    h = _modulate(x_ref[...], pat_ref[0, 0], pat_ref[0, 1]).astype(BF16)
    for k, out in enumerate((u_ref, zr_ref, gs_ref, gr_ref)):
        out[...] = jnp.dot(h, w_ref[0, :, IN_SPLITS[k]:IN_SPLITS[k + 1]], preferred_element_type=F32)


def _inproj(x, pat, w_in_bf16, layer, rows_per_trunk, tm=256):
    rows = x.shape[0]
    d = D_MODEL
    tpt = rows_per_trunk // tm
    widths = [IN_SPLITS[k + 1] - IN_SPLITS[k] for k in range(4)]
    rowblk = lambda n: pl.BlockSpec((tm, n), lambda i: (i, 0))
    return pl.pallas_call(
        _inproj_kernel,
        grid=(rows // tm,),
        in_specs=[rowblk(d),
                  pl.BlockSpec((1, 6, SUBLANES, d), lambda i: (i // tpt, 0, 0, 0)),
                  pl.BlockSpec((1, d, IN_SPLITS[-1]), lambda i: (layer, 0, 0))],
        out_specs=[rowblk(n) for n in widths],
        out_shape=[jax.ShapeDtypeStruct((rows, n), F32) for n in widths],
        compiler_params=_params(("arbitrary",)),
        name="in_proj",
    )(x, pat, w_in_bf16)


def _s5_disc_kernel(are_ref, aim_ref, ldt_ref, bre_ref, bim_ref, abre_ref, abim_ref, bbre_ref, bbim_ref):
    a_re = jnp.minimum(are_ref[...], -1e-4)
    a_im = aim_ref[...]
    dt = jnp.exp(ldt_ref[...])
    mag = jnp.exp(a_re * dt)
    ab_re = mag * jnp.cos(a_im * dt)
    ab_im = mag * jnp.sin(a_im * dt)
    den = a_re * a_re + a_im * a_im
    nr = ab_re - 1.0
    q_re = (nr * a_re + ab_im * a_im) / den
    q_im = (ab_im * a_re - nr * a_im) / den
    abre_ref[...] = ab_re
    abim_ref[...] = ab_im
    b_re = bre_ref[...]
    b_im = bim_ref[...]
    bbre_ref[...] = q_re[:, None, :] * b_re - q_im[:, None, :] * b_im
    bbim_ref[...] = q_re[:, None, :] * b_im + q_im[:, None, :] * b_re


def _s5_discretize(a_re, a_im, log_dt, b_re, b_im):
    m, n = a_re.shape
    p = b_re.shape[1]
    return pl.pallas_call(
        _s5_disc_kernel,
        out_shape=[jax.ShapeDtypeStruct((m, n), F32), jax.ShapeDtypeStruct((m, n), F32),
                   jax.ShapeDtypeStruct((m, p, n), F32), jax.ShapeDtypeStruct((m, p, n), F32)],
        name="s5_discretize",
    )(a_re, a_im, log_dt, b_re, b_im)


def _s5_scan_kernel(u_ref, wbre_ref, wbim_ref, abre_ref, abim_ref, wc_ref, h0_ref,
                    y_ref, hfin_ref, bure_ref, buim_ref, hre_ref, him_ref, *, nb, lane_w):
    d = pl.program_id(0)
    c = pl.program_id(1)
    nc = pl.num_programs(1)
    rows = u_ref.shape[0]

    @pl.when(c == 0)
    def _():
        hre_ref[...] = h0_ref[0, 0]
        him_ref[...] = h0_ref[0, 1]

    ub = u_ref[...].astype(BF16)
    ct = 2 * LANES
    for j in range(S5_STATE // ct):
        first_channel = j * ct * S5_GROUP // N_S5
        ks = slice(first_channel // ct * ct, first_channel // ct * ct + ct)
        js = slice(ct * j, ct * (j + 1))
        bure_ref[:, js] = jnp.dot(ub[:, ks], wbre_ref[0, ks, js], preferred_element_type=F32)
        buim_ref[:, js] = jnp.dot(ub[:, ks], wbim_ref[0, ks, js], preferred_element_type=F32)

    fwd = d == 0
    if nb >= SUBLANES:
        steps = rows // nb
        for lc in range(S5_STATE // lane_w):
            ls = slice(lc * lane_w, (lc + 1) * lane_w)
            ar = jnp.broadcast_to(abre_ref[0, :, ls], (nb, lane_w))
            ai = jnp.broadcast_to(abim_ref[0, :, ls], (nb, lane_w))

            def body(s, carry, ls=ls, ar=ar, ai=ai):
                hr, hi = carry
                t = jnp.where(fwd, s, steps - 1 - s)
                r0 = pl.multiple_of(t * nb, nb)
                br = bure_ref[pl.ds(r0, nb), ls]
                bi = buim_ref[pl.ds(r0, nb), ls]
                nr = ar * hr - ai * hi + br
                ni = ar * hi + ai * hr + bi
                bure_ref[pl.ds(r0, nb), ls] = nr
                buim_ref[pl.ds(r0, nb), ls] = ni
                return nr, ni

            hr, hi = lax.fori_loop(0, steps, body, (hre_ref[:, ls], him_ref[:, ls]))
            hre_ref[:, ls] = hr
            him_ref[:, ls] = hi
    else:
        per = SUBLANES // nb
        groups = rows // SUBLANES
        row_id = lax.broadcasted_iota(jnp.int32, (SUBLANES, lane_w), 0) // nb
        shift = jnp.where(fwd, nb, SUBLANES - nb)
        for lc in range(S5_STATE // lane_w):
            ls = slice(lc * lane_w, (lc + 1) * lane_w)
            ar = jnp.broadcast_to(abre_ref[0, :, ls], (SUBLANES, lane_w))
            ai = jnp.broadcast_to(abim_ref[0, :, ls], (SUBLANES, lane_w))
            h0r = hre_ref[:, ls]
            h0i = him_ref[:, ls]

            def body(gidx, carry, ls=ls, ar=ar, ai=ai):
                tr, ti = carry
                g = jnp.where(fwd, gidx, groups - 1 - gidx)
                r0 = pl.multiple_of(g * SUBLANES, SUBLANES)
                br = bure_ref[pl.ds(r0, SUBLANES), ls]
                bi = buim_ref[pl.ds(r0, SUBLANES), ls]
                for k in range(per):
                    pr = pltpu.roll(tr, shift, 0)
                    pi = pltpu.roll(ti, shift, 0)
                    nr = ar * pr - ai * pi + br
                    ni = ar * pi + ai * pr + bi
                    tgt = jnp.where(fwd, k, per - 1 - k)
                    sel = row_id == tgt
                    tr = jnp.where(sel, nr, tr)
                    ti = jnp.where(sel, ni, ti)
                bure_ref[pl.ds(r0, SUBLANES), ls] = tr
                buim_ref[pl.ds(r0, SUBLANES), ls] = ti
                return tr, ti

            tr, ti = lax.fori_loop(0, groups, body, (h0r, h0i))
            hre_ref[:, ls] = tr
            him_ref[:, ls] = ti

    for q in range(D_S5 // ct):
        qs = slice(ct * q, ct * (q + 1))
        ss = slice(q * ct * N_S5 // S5_GROUP, (q + 1) * ct * N_S5 // S5_GROUP)
        si = slice(S5_STATE + ss.start, S5_STATE + ss.stop)
        y_ref[0, :, qs] = (jnp.dot(bure_ref[:, ss].astype(BF16), wc_ref[0, ss, qs], preferred_element_type=F32)
                           + jnp.dot(buim_ref[:, ss].astype(BF16), wc_ref[0, si, qs],
                                     preferred_element_type=F32))

    @pl.when(c == nc - 1)
    def _():
        hfin_ref[0, 0] = hre_ref[...]
        hfin_ref[0, 1] = him_ref[...]


def _s5_scan(u, wb_re, wb_im, ab_re, ab_im, wc, h0, nb, row0, rows, chunk_rows=512):
    if nb < SUBLANES:
        h0 = jnp.tile(h0, (1, 1, SUBLANES // nb, 1))
    srows = max(nb, SUBLANES)
    cr = min(chunk_rows, rows)
    nc = rows // cr
    c0 = row0 // cr
    s = S5_STATE
    lane_w = 256 if nb >= SUBLANES else 512
    chunk = lambda d, c: c + d * (nc - 1 - 2 * c)
    y, hfin = pl.pallas_call(
        functools.partial(_s5_scan_kernel, nb=nb, lane_w=lane_w),
        grid=(2, nc),
        in_specs=[pl.BlockSpec((cr, D_S5), lambda d, c: (c0 + chunk(d, c), 0)),
                  pl.BlockSpec((1, D_S5, s), lambda d, c: (d, 0, 0)),
                  pl.BlockSpec((1, D_S5, s), lambda d, c: (d, 0, 0)),
                  pl.BlockSpec((1, 1, s), lambda d, c: (d, 0, 0)),
                  pl.BlockSpec((1, 1, s), lambda d, c: (d, 0, 0)),
                  pl.BlockSpec((1, 2 * s, D_S5), lambda d, c: (d, 0, 0)),
                  pl.BlockSpec((1, 2, srows, s), lambda d, c: (d, 0, 0, 0))],
        out_specs=[pl.BlockSpec((1, cr, D_S5), lambda d, c: (d, chunk(d, c), 0)),
                   pl.BlockSpec((1, 2, srows, s), lambda d, c: (d, 0, 0, 0))],
        out_shape=[jax.ShapeDtypeStruct((2, rows, D_S5), F32),
                   jax.ShapeDtypeStruct((2, 2, srows, s), F32)],
        scratch_shapes=[pltpu.VMEM((cr, s), F32), pltpu.VMEM((cr, s), F32),
                        pltpu.VMEM((srows, s), F32), pltpu.VMEM((srows, s), F32)],
        compiler_params=_params(("arbitrary", "arbitrary")),
        name="s5_scan",
    )(u, wb_re, wb_im, ab_re, ab_im, wc, h0)
    if nb < SUBLANES:
        hfin = jnp.stack([hfin[0, :, srows - nb:], hfin[1, :, :nb]])
    return y, hfin


def _rwkv_prep_kernel(prev_ref, cur_ref, next_ref, mu_ref, ones_ref, g2_ref, kk_ref, ka_ref, rk_ref,
                      w0_ref, w2_ref, a0_ref, a2_ref,
                      r_ref, v_ref, nkk_ref, bonus_ref, g_ref, w_ref, kd_ref, b_ref, *, nb, seq, grid_shift):
    tm = cur_ref.shape[0]
    i = pl.program_id(0)
    cur = cur_ref[...]
    prv = prev_ref[...]
    nxt = next_ref[...]
    row = lax.broadcasted_iota(jnp.int32, (tm, RWKV_COLS), 0)
    lane = lax.broadcasted_iota(jnp.int32, (tm, RWKV_COLS), 1)
    t = (i * tm + row) // nb

    def rows_before(s):
        if s == tm:
            return prv
        return jnp.where(row < s, pltpu.roll(prv, s, 0), pltpu.roll(cur, s, 0))

    def rows_after(s):
        if s == tm:
            return nxt
        return jnp.where(row >= tm - s, pltpu.roll(nxt, tm - s, 0), pltpu.roll(cur, tm - s, 0))

    if grid_shift:
        tw = t % GRID_W
        left = jnp.where(tw == 0, 0.0, rows_before(nb))
        right = jnp.where(tw == GRID_W - 1, 0.0, rows_after(nb))
        up = jnp.where(t < GRID_W, 0.0, rows_before(nb * GRID_W))
        down = jnp.where(t >= seq - GRID_W, 0.0, rows_after(nb * GRID_W))
        m4 = lane % 4
        sh = jnp.where(m4 == 0, left, jnp.where(m4 == 1, right, jnp.where(m4 == 2, up, down)))
    else:
        before = jnp.where(t == 0, 0.0, rows_before(nb))
        after = jnp.where(t == seq - 1, 0.0, rows_after(nb))
        sh = jnp.where(lane % 2 == 0, before, after)

    z = cur + (sh - cur) * mu_ref[...]
    r = z[:, 0:D_RWKV]
    k = z[:, D_RWKV:2 * D_RWKV]
    v = z[:, 2 * D_RWKV:3 * D_RWKV]
    xw = z[:, 3 * D_RWKV:3 * D_RWKV + LORA_W]
    xa = z[:, 3 * D_RWKV + LORA_W:3 * D_RWKV + LORA_W + LORA_A]
    xg = z[:, 3 * D_RWKV + LORA_W + LORA_A:]
    ones = ones_ref[...]

    g_ref[...] = _dot(_sigmoid(xg), g2_ref[...])
    kk = k * kk_ref[...]
    nrm = jnp.sqrt(_dot_split(kk * kk, ones))
    kk = kk / jnp.maximum(nrm, 1e-12)
    r_ref[...] = r
    v_ref[...] = v
    nkk_ref[...] = -kk
    bonus_ref[...] = _dot_split(r * k * rk_ref[...], ones) * v
    txw = jnp.tanh(xw)
    for d in range(2):
        zw = -(w0_ref[d] + _dot(txw, w2_ref[d]))
        softplus = jnp.maximum(zw, 0.0) + jnp.log(1.0 + jnp.exp(-jnp.abs(zw)))
        w_log = -softplus - 0.5
        w_ref[d] = jnp.exp(-jnp.exp(w_log))
        a = _sigmoid(a0_ref[d] + _dot(xa, a2_ref[d]))
        kd_ref[d] = k * (1.0 + (a - 1.0) * ka_ref[...])
        b_ref[d] = kk * a


def _rwkv_prep(zr, p, nb, seq, grid_shift, row0, tm=128):
    rows = seq * nb
    nt = rows // tm
    t0 = row0 // tm
    if grid_shift:
        assert tm == nb * GRID_W, "one tile must be one grid row of the latent grid"
    else:
        assert tm % nb == 0 and tm >= nb
    kern = functools.partial(_rwkv_prep_kernel, nb=nb, seq=seq, grid_shift=grid_shift)
    c = D_RWKV
    vec = lambda n: pl.BlockSpec((1, n), lambda i: (0, 0))
    mat = lambda a, b: pl.BlockSpec((a, b), lambda i: (0, 0))
    row_out = pl.BlockSpec((tm, c), lambda i: (i, 0))
    dir_out = pl.BlockSpec((2, tm, c), lambda i: (0, i, 0))
    head_ones = jnp.kron(jnp.eye(H_RWKV, dtype=F32), jnp.ones((RWKV_HEAD, RWKV_HEAD), F32)).astype(BF16)
    in_specs = [pl.BlockSpec((tm, RWKV_COLS), lambda i: (t0 + jnp.maximum(i - 1, 0), 0)),
                pl.BlockSpec((tm, RWKV_COLS), lambda i: (t0 + i, 0)),
                pl.BlockSpec((tm, RWKV_COLS), lambda i: (t0 + jnp.minimum(i + 1, nt - 1), 0)),
                vec(RWKV_COLS), mat(c, c), mat(LORA_G, c), vec(c), vec(c), vec(c),
                pl.BlockSpec((2, 1, c), lambda i: (0, 0, 0)),
                pl.BlockSpec((2, LORA_W, c), lambda i: (0, 0, 0)),
                pl.BlockSpec((2, 1, c), lambda i: (0, 0, 0)),
                pl.BlockSpec((2, LORA_A, c), lambda i: (0, 0, 0))]
    args = [zr, zr, zr, p['rw_mu'].reshape(1, -1), head_ones, p['rw_g2'], p['rw_k_k'].reshape(1, c),
            p['rw_k_a'].reshape(1, c), p['rw_r_k'].reshape(1, c), p['rw_w0'].reshape(2, 1, c), p['rw_w2'],
            p['rw_a0'].reshape(2, 1, c), p['rw_a2']]
    part = jax.ShapeDtypeStruct((rows, c), F32)
    return pl.pallas_call(
        kern,
        grid=(nt,),
        in_specs=in_specs,
        out_specs=[row_out] * 5 + [dir_out] * 3,
        out_shape=[part] * 5 + [jax.ShapeDtypeStruct((2, rows, c), F32)] * 3,
        compiler_params=_params(("arbitrary",)),
        name="rwkv_prep",
    )(*args)


def _rwkv_scan_kernel(a_ref, w_ref, b_ref, k_ref, r_ref, v_ref, s0_ref, y_ref, sfin_ref, s_ref, *,
                      nacc, row_blk, folded):
    d = pl.program_id(0)
    c = pl.program_id(1)
    nc = pl.num_programs(1)
    tc = a_ref.shape[1]
    nj = RWKV_HEAD
    j_unroll = 16 if row_blk >= 32 else nj
    rev = (c >= nc // 2) if folded else (d == 1)

    @pl.when(c == 0)
    def _():
        s_ref[...] = s0_ref[0]

    def step(s, carry):
        t = jnp.where(rev, tc - 1 - s, s)
        zero = jnp.zeros((row_blk, a_ref.shape[3]), F32)
        for rb in range(s_ref.shape[1] // row_blk):
            rows = pl.ds(rb * row_blk, row_blk)

            def pass1(jo, acc):
                acc = list(acc)
                for ji in range(j_unroll):
                    j = jo * j_unroll + ji
                    acc[ji % nacc] = acc[ji % nacc] + s_ref[j, rows, :] * a_ref[0, t, pl.ds(j, 1), :]
                return tuple(acc)

            sa = functools.reduce(lambda x, y: x + y, lax.fori_loop(0, nj // j_unroll, pass1, (zero,) * nacc))
            vv = v_ref[0, t, rows, :]

            def pass2(jo, acc):
                acc = list(acc)
                for ji in range(j_unroll):
                    j = jo * j_unroll + ji
                    s_new = (s_ref[j, rows, :] * w_ref[0, t, pl.ds(j, 1), :]
                             + sa * b_ref[0, t, pl.ds(j, 1), :]
                             + vv * k_ref[0, t, pl.ds(j, 1), :])
                    s_ref[j, rows, :] = s_new
                    acc[ji % nacc] = acc[ji % nacc] + s_new * r_ref[0, t, pl.ds(j, 1), :]
                return tuple(acc)

            y_ref[0, t, rows, :] = functools.reduce(
                lambda x, y: x + y, lax.fori_loop(0, nj // j_unroll, pass2, (zero,) * nacc))
        return carry

    lax.fori_loop(0, tc, step, 0)

    @pl.when(c == nc - 1)
    def _():
        sfin_ref[0] = s_ref[...]


def _rwkv_scan(a, w, b, k, r, v, s0, tc, folded):
    _, tp, _, nl = w.shape
    iv = v.shape[2]
    row_blk = min(iv, 32)
    nacc = 4 if row_blk <= 16 else 2
    nc_half = tp // tc
    if folded:
        nd, nc = 1, 2 * nc_half
        tmap = lambda d, c: (c // nc_half, jnp.where(c < nc_half, c, nc - 1 - c), 0, 0)
        shared = per_dir = tmap
    else:
        nd, nc = 2, nc_half
        chunk = lambda d, c: c + d * (nc - 1 - 2 * c)
        shared = lambda d, c: (0, chunk(d, c), 0, 0)
        per_dir = lambda d, c: (d, chunk(d, c), 0, 0)
    key = lambda m: pl.BlockSpec((1, tc, RWKV_HEAD, nl), m)
    val = lambda m: pl.BlockSpec((1, tc, iv, nl), m)
    sspec = pl.BlockSpec((1, RWKV_HEAD, iv, nl), lambda d, c: (d, 0, 0, 0))
    return pl.pallas_call(
        functools.partial(_rwkv_scan_kernel, nacc=nacc, row_blk=row_blk, folded=folded),
        grid=(nd, nc),
        in_specs=[key(shared), key(per_dir), key(per_dir), key(per_dir), key(shared), val(shared), sspec],
        out_specs=[val(per_dir), sspec],
        out_shape=[jax.ShapeDtypeStruct((2, tp, iv, nl), F32),
                   jax.ShapeDtypeStruct((nd, RWKV_HEAD, iv, nl), F32)],
        scratch_shapes=[pltpu.VMEM((RWKV_HEAD, iv, nl), F32)],
        compiler_params=_params(("arbitrary", "arbitrary")),
        name="rwkv_scan",
    )(a, w, b, k, r, v, s0)


SMP_REP = 4
SMP_ROWS = RWKV_HEAD // SMP_REP


def _block_transpose(tiles, bs):
    tiles = list(tiles)
    lane = lax.broadcasted_iota(jnp.int32, tiles[0].shape, 1)
    for kbit in range(3):
        sft = bs << kbit
        bit = (lane // sft) % 2 == 1
        for r in range(8):
            if (r >> kbit) & 1:
                continue
            r2 = r | (1 << kbit)
            lo, hi = tiles[r], tiles[r2]
            tiles[r] = jnp.where(bit, pltpu.roll(hi, sft, 1), lo)
            tiles[r2] = jnp.where(bit, hi, pltpu.roll(lo, LANES - sft, 1))
    return tiles


def _head_tiles(x):
    xt = x.T
    return [xt[h * RWKV_HEAD:(h + 1) * RWKV_HEAD, :] for h in range(H_RWKV)]


def _from_head_tiles(tiles):
    return jnp.concatenate(tiles, axis=0).T


def _relayout_ctx_kernel(r_ref, v_ref, nkk_ref, w_ref, kd_ref, b_ref, a_o, r_o, v_o, w_o, b_o, k_o, *, nb):
    steps = a_o.shape[1]
    for src, dst in ((nkk_ref, a_o), (r_ref, r_o), (v_ref, v_o)):
        tiles = _block_transpose(_head_tiles(src[...]), nb)
        for t in range(steps):
            dst[0, t] = tiles[t]
    for src, dst in ((w_ref, w_o), (b_ref, b_o), (kd_ref, k_o)):
        for d in range(2):
            tiles = _block_transpose(_head_tiles(src[d]), nb)
            for t in range(steps):
                dst[d, t] = tiles[t]


def _relayout_ctx(prep, nb, seq):
    r, v, nkk, _, _, w, kd, bb = prep
    steps = LANES // nb
    assert steps == H_RWKV and nb * H_RWKV == LANES
    tm = steps * nb
    nt = seq // steps
    c = D_RWKV
    nat = pl.BlockSpec((tm, c), lambda i: (i, 0))
    nat2 = pl.BlockSpec((2, tm, c), lambda i: (0, i, 0))
    out1 = pl.BlockSpec((1, steps, RWKV_HEAD, LANES), lambda i: (0, i, 0, 0))
    out2 = pl.BlockSpec((2, steps, RWKV_HEAD, LANES), lambda i: (0, i, 0, 0))
    s1 = jax.ShapeDtypeStruct((1, seq, RWKV_HEAD, LANES), F32)
    s2 = jax.ShapeDtypeStruct((2, seq, RWKV_HEAD, LANES), F32)
    a, r_, v_, w_, b_, k_ = pl.pallas_call(
        functools.partial(_relayout_ctx_kernel, nb=nb),
        grid=(nt,),
        in_specs=[nat, nat, nat, nat2, nat2, nat2],
        out_specs=[out1, out1, out1, out2, out2, out2],
        out_shape=[s1, s1, s1, s2, s2, s2],
        compiler_params=_params(("arbitrary",)),
        name="rwkv_relayout_ctx",
    )(r, v, nkk, w, kd, bb)
    return a, w_, b_, k_, r_, v_


def _unmix_ctx_kernel(y_ref, o_ref, *, nb):
    steps = y_ref.shape[1]
    tiles = [y_ref[0, t] + y_ref[1, t] for t in range(steps)]
    o_ref[...] = _from_head_tiles(_block_transpose(tiles, nb))


def _unmix_ctx(y, nb, seq):
    steps = LANES // nb
    tm = steps * nb
    return pl.pallas_call(
        functools.partial(_unmix_ctx_kernel, nb=nb),
        grid=(seq // steps,),
        in_specs=[pl.BlockSpec((2, steps, RWKV_HEAD, LANES), lambda i: (0, i, 0, 0))],
        out_specs=pl.BlockSpec((tm, D_RWKV), lambda i: (i, 0)),
        out_shape=jax.ShapeDtypeStruct((seq * nb, D_RWKV), F32),
        compiler_params=_params(("arbitrary",)),
        name="rwkv_unmix_ctx",
    )(y)


def _relayout_smp_kernel(ra, rb, va, vb, na, nb_, wa, wb, ka, kb, ba, bb, sel_ref,
                         a_o, r_o, v_o, w_o, b_o, k_o):
    steps = a_o.shape[1]
    lane_v = lax.broadcasted_iota(jnp.int32, (SMP_ROWS, LANES), 1)

    def conv(xa, xb):
        tiles = _block_transpose(_head_tiles(jnp.concatenate([xa, xb], axis=0)), 2)
        parts = []
        for x in tiles:
            hi = x.astype(BF16)
            r1 = x - hi.astype(F32)
            mid = r1.astype(BF16)
            lo = (r1 - mid.astype(F32)).astype(BF16)
            parts.append((hi, mid, lo))
        return parts

    def expand(o0, o1, half):
        order0 = range(8) if half == 0 else range(7, -1, -1)
        order1 = range(7, -1, -1) if half == 0 else range(8)
        cols = []
        for term in range(3):
            cols.append(jnp.concatenate([o0[s][term] for s in order0], axis=0))
            cols.append(jnp.concatenate([o1[s][term] for s in order1], axis=0))
        lhs = jnp.concatenate(cols, axis=1)
        out = [None] * steps
        for pair in range(steps // 16):
            res = jnp.dot(lhs, sel_ref[half, pair], preferred_element_type=F32)
            for q in range(2):
                for s_lo in range(8):
                    out[8 * (2 * pair + q) + s_lo] = res[64 * s_lo:64 * (s_lo + 1), LANES * q:LANES * (q + 1)]
        return out

    def value_rows(full):
        blk = lane_v // (LANES // SMP_REP)
        rows = [full[k * SMP_ROWS:(k + 1) * SMP_ROWS] for k in range(SMP_REP)]
        return jnp.where(blk == 0, rows[0], jnp.where(blk == 1, rows[1], jnp.where(blk == 2, rows[2], rows[3])))

    for srca, srcb, dst in ((na, nb_, a_o), (ra, rb, r_o), (va, vb, v_o)):
        o = conv(srca[...], srcb[...])
        for half in range(2):
            tiles = expand(o, o, half)
            for s in range(steps):
                dst[half, s] = value_rows(tiles[s]) if dst is v_o else tiles[s]
    for srca, srcb, dst in ((wa, wb, w_o), (ba, bb, b_o), (ka, kb, k_o)):
        o0 = conv(srca[0], srcb[0])
        o1 = conv(srca[1], srcb[1])
        for half in range(2):
            tiles = expand(o0, o1, half)
            for s in range(steps):
                dst[half, s] = tiles[s]


def _smp_selection():
    sel = np.zeros((2, 2, 6, LANES, 2 * LANES), np.float32)
    for half in range(2):
        for pair in range(2):
            for q in range(2):
                s_hi = 2 * pair + q
                grp_a = 0 * 64 + 16 * s_hi
                grp_b = 1 * 64 + 16 * (3 - s_hi)
                src = (grp_a, grp_b) if half == 0 else (grp_b, grp_a)
                for d in range(2):
                    for blk in range(SMP_REP):
                        for hb in range(16):
                            col = q * LANES + blk * 32 + d * 16 + hb
                            for term in range(3):
                                sel[half, pair, 2 * term + d, src[d] + hb, col] = 1.0
    return jnp.asarray(sel.reshape(2, 2, 6 * LANES, 2 * LANES), BF16)


def _relayout_smp(prep, nb, seq, row0):
    r, v, nkk, _, _, w, kd, bb = prep
    assert 2 * nb * H_RWKV * SMP_REP == LANES
    steps = 32
    tm = steps * nb
    nt = seq // steps
    half_t = seq // 2
    c = D_RWKV
    t0 = row0 // tm
    blk_a = pl.BlockSpec((tm, c), lambda g: (t0 + g, 0))
    blk_b = pl.BlockSpec((tm, c), lambda g: (t0 + nt - 1 - g, 0))
    blk2_a = pl.BlockSpec((2, tm, c), lambda g: (0, t0 + g, 0))
    blk2_b = pl.BlockSpec((2, tm, c), lambda g: (0, t0 + nt - 1 - g, 0))
    okey = pl.BlockSpec((2, steps, RWKV_HEAD, LANES), lambda g: (0, g, 0, 0))
    oval = pl.BlockSpec((2, steps, SMP_ROWS, LANES), lambda g: (0, g, 0, 0))
    skey = jax.ShapeDtypeStruct((2, half_t, RWKV_HEAD, LANES), F32)
    sval = jax.ShapeDtypeStruct((2, half_t, SMP_ROWS, LANES), F32)
    a, r_, v_, w_, b_, k_ = pl.pallas_call(
        _relayout_smp_kernel,
        grid=(nt // 2,),
        in_specs=[blk_a, blk_b, blk_a, blk_b, blk_a, blk_b, blk2_a, blk2_b, blk2_a, blk2_b, blk2_a, blk2_b,
                  pl.BlockSpec((2, 2, 6 * LANES, 2 * LANES), lambda g: (0, 0, 0, 0))],
        out_specs=[okey, okey, oval, okey, okey, okey],
        out_shape=[skey, skey, sval, skey, skey, skey],
        compiler_params=_params(("arbitrary",)),
        name="rwkv_relayout_smp",
    )(r, r, v, v, nkk, nkk, w, w, kd, kd, bb, bb, _smp_selection())
    return a, w_, b_, k_, r_, v_


def _unmix_smp_kernel(y_ref, o_ref, *, half_tiles):
    q = pl.program_id(0)
    steps = y_ref.shape[1]
    low = q < half_tiles
    h0 = jnp.where(low, 0, 1)
    lane = lax.broadcasted_iota(jnp.int32, (SMP_ROWS, LANES), 1)
    tiles = []
    for s_lo in range(8):
        blocks = []
        for blk in range(SMP_REP):
            acc = jnp.zeros((SMP_ROWS, LANES), F32)
            for s_hi in range(steps // 8):
                s = 8 * s_hi + s_lo
                u = jnp.where(low, s, steps - 1 - s)
                d0 = y_ref[h0, u]
                d1 = y_ref[1 - h0, u]
                sh0 = (16 * s_hi - 32 * blk) % LANES
                sh1 = (16 * s_hi - 32 * blk - 16) % LANES
                t0 = d0 if sh0 == 0 else pltpu.roll(d0, sh0, 1)
                t1 = d1 if sh1 == 0 else pltpu.roll(d1, sh1, 1)
                acc = jnp.where(lane // 16 == s_hi, t0 + t1, acc)
            blocks.append(acc)
        tiles.append(jnp.concatenate(blocks, axis=0))
    o_ref[...] = _from_head_tiles(_block_transpose(tiles, 2))


def _unmix_smp(y, nb, seq):
    steps = 64
    tm = steps * nb
    assert tm == LANES
    nq = seq // steps
    half_tiles = nq // 2
    return pl.pallas_call(
        functools.partial(_unmix_smp_kernel, half_tiles=half_tiles),
        grid=(nq,),
        in_specs=[pl.BlockSpec((2, steps, SMP_ROWS, LANES),
                               lambda q: (0, jnp.where(q < half_tiles, q, nq - 1 - q), 0, 0))],
        out_specs=pl.BlockSpec((tm, D_RWKV), lambda q: (q, 0)),
        out_shape=jax.ShapeDtypeStruct((seq * nb, D_RWKV), F32),
        compiler_params=_params(("arbitrary",)),
        name="rwkv_unmix_smp",
    )(y)


def _rwkv_state_smp(state_l, nb):
    s = state_l.reshape(nb, 2, H_RWKV, SMP_REP, SMP_ROWS, RWKV_HEAD)
    s = s.transpose(5, 4, 3, 1, 2, 0)
    return s.reshape(1, RWKV_HEAD, SMP_ROWS, LANES)


def _rwkv_state_out_ctx(sfin, nb):
    s = sfin.reshape(2, RWKV_HEAD, RWKV_HEAD, H_RWKV, nb)
    return s.transpose(4, 0, 3, 2, 1)


def _mix_out_kernel(x_ref, pat_ref, u_ref, gs_ref, gr_ref,
                    y5c_ref, y5s_ref, yrwc_ref, yrws_ref, bonc_ref, bons_ref, gc_ref, gs2_ref,
                    d_ref, wglu_ref, ones_ref, lnxw_ref, lnxb_ref, wups_ref, wupr_ref, wout_ref,
                    ln1g_ref, ln1b_ref, o_ref, *, ctx_tiles):
    is_ctx = pl.program_id(0) < ctx_tiles
    pick = lambda a, b: jnp.where(is_ctx, a, b)
    y5 = u_ref[...] * d_ref[...] + pick(y5c_ref[0] + y5c_ref[1], y5s_ref[0] + y5s_ref[1])
    y5 = jax.nn.gelu(y5)
    y5 = y5 * _sigmoid(_dot(y5, wglu_ref[...]))
    ones = ones_ref[...]
    yr = pick(yrwc_ref[...], yrws_ref[...])
    inv_n = 1.0 / RWKV_HEAD
    mu = _dot_split(yr, ones) * inv_n
    yc = yr - mu
    var = _dot_split(yc * yc, ones) * inv_n
    yr = yc * lax.rsqrt(var + GN_EPS) * lnxw_ref[...] + lnxb_ref[...]
    yr = (yr + pick(bonc_ref[...], bons_ref[...])) * pick(gc_ref[...], gs2_ref[...])
    merged = (_sigmoid(gs_ref[...]) * _dot(y5, wups_ref[...])
              + _sigmoid(gr_ref[...]) * _dot(yr, wupr_ref[...]))
    mix = _dot(merged, wout_ref[...])
    x3 = _modulate(mix, jnp.zeros_like(pat_ref[0, 2]), pat_ref[0, 2] - 1.0)
    o_ref[...] = _layer_norm(ALPHA * x_ref[...] + x3, ln1g_ref[...], ln1b_ref[...])


def _mix_out(x, pat, u, gs, gr, y5, yrw, bonus, g, p, rows_per_trunk, tm=256):
    rows = x.shape[0]
    d = D_MODEL
    c = D_RWKV
    tpt = rows_per_trunk // tm
    rowblk = lambda n: pl.BlockSpec((tm, n), lambda i: (i, 0))
    ctx_i = lambda i: jnp.minimum(i, tpt - 1)
    smp_i = lambda i: jnp.maximum(i - tpt, 0)
    pair = lambda spec_of: [spec_of(ctx_i), spec_of(smp_i)]
    row_t = lambda f: pl.BlockSpec((tm, c), lambda i: (f(i), 0))
    dir_t = lambda f: pl.BlockSpec((2, tm, c), lambda i: (0, f(i), 0))
    vec = lambda n: pl.BlockSpec((1, n), lambda i: (0, 0))
    mat = lambda a, b: pl.BlockSpec((a, b), lambda i: (0, 0))
    head_ones = jnp.kron(jnp.eye(H_RWKV, dtype=F32), jnp.ones((RWKV_HEAD, RWKV_HEAD), F32)).astype(BF16)
    return pl.pallas_call(
        functools.partial(_mix_out_kernel, ctx_tiles=tpt),
        grid=(rows // tm,),
        in_specs=[rowblk(d),
                  pl.BlockSpec((1, 6, SUBLANES, d), lambda i: (i // tpt, 0, 0, 0)),
                  rowblk(c), rowblk(d), rowblk(d)]
                 + pair(dir_t) + pair(row_t) + pair(row_t) + pair(row_t)
                 + [vec(c), mat(c, c), mat(c, c), vec(c), vec(c), mat(c, d), mat(c, d), mat(d, d),
                    vec(d), vec(d)],
        out_specs=rowblk(d),
        out_shape=jax.ShapeDtypeStruct((rows, d), F32),
        compiler_params=_params(("arbitrary",)),
        name="mix_out",
    )(x, pat, u, gs, gr, *y5, *yrw, *bonus, *g,
      p['s5_d'].reshape(1, c), p['s5_w_glu'].astype(BF16), head_ones,
      p['rw_lnx_w'].reshape(1, c), p['rw_lnx_b'].reshape(1, c),
      p['w_up_s5'].astype(BF16), p['w_up_rwkv'].astype(BF16), p['w_out'].astype(BF16),
      p['ln1_g'].reshape(1, d), p['ln1_b'].reshape(1, d))


def _first_max(x, lane, valid):
    xm = jnp.where(valid, x, -jnp.inf)
    m = jnp.max(xm, -1, keepdims=True)
    idx = jnp.min(jnp.where(xm == m, lane, float(N_EXPERTS)), -1, keepdims=True)
    return m, idx


def _moe_kernel(x_ref, pat_ref, wr_ref, br_ref, wg_ref, wu_ref, wd_ref, ln2g_ref, ln2b_ref, o_ref,
                hb_ref, comb_ref, acc_ref):
    e = pl.program_id(1)
    ne = pl.num_programs(1)
    tm = x_ref.shape[0]

    @pl.when(e == 0)
    def _():
        h = _modulate(x_ref[...], pat_ref[0, 3], pat_ref[0, 4])
        hb_ref[...] = h.astype(BF16)
        wr = wr_ref[...]
        w_hi = wr.astype(BF16)
        w_lo = (wr - w_hi.astype(F32)).astype(BF16)
        h_hi = h.astype(BF16)
        h_lo = (h - h_hi.astype(F32)).astype(BF16)
        logits = (jnp.dot(h_hi, w_hi, preferred_element_type=F32)
                  + jnp.dot(h_hi, w_lo, preferred_element_type=F32)
                  + jnp.dot(h_lo, w_hi, preferred_element_type=F32)) + br_ref[...]
        logits = logits - jnp.max(logits, -1, keepdims=True)
        ex = jnp.exp(logits)
        probs = ex / jnp.sum(ex, -1, keepdims=True)
        lane_i = lax.broadcasted_iota(jnp.int32, (tm, N_EXPERTS), 1)
        lane = lane_i.astype(F32)
        grp = (lane_i // EXPERTS_PER_GROUP).astype(F32)
        best_score = jnp.full((tm, 1), -jnp.inf, F32)
        best_grp = jnp.zeros((tm, 1), F32)
        for gi in range(N_GROUPS):
            in_g = grp == float(gi)
            m1, i1 = _first_max(probs, lane, in_g)
            m2, _ = _first_max(probs, lane, in_g & (lane != i1))
            score = m1 + m2
            better = score > best_score
            best_score = jnp.where(better, score, best_score)
            best_grp = jnp.where(better, float(gi), best_grp)
        in_best = grp == best_grp
        m1, i1 = _first_max(probs, lane, in_best)
        m2, i2 = _first_max(probs, lane, in_best & (lane != i1))
        tot = m1 + m2
        comb_ref[...] = jnp.where(lane == i1, m1 / tot, 0.0) + jnp.where(lane == i2, m2 / tot, 0.0)
        acc_ref[...] = jnp.zeros_like(acc_ref)

    hb = hb_ref[...]
    lane = lax.broadcasted_iota(jnp.int32, (tm, N_EXPERTS), 1)
    ce = jnp.sum(jnp.where(lane == e, comb_ref[...], 0.0), -1, keepdims=True)
    hid = _silu(_dot(hb, wg_ref[0])) * _dot(hb, wu_ref[0])
    acc_ref[...] += _dot(hid * ce, wd_ref[0])

    @pl.when(e == ne - 1)
    def _():
        ffn = _modulate(acc_ref[...], jnp.zeros_like(pat_ref[0, 5]), pat_ref[0, 5] - 1.0)
        o_ref[...] = _layer_norm(ALPHA * x_ref[...] + ffn, ln2g_ref[...], ln2b_ref[...])


def _moe(x, pat, w_router, b_router, wg, wu, wd, ln2_g, ln2_b, rows_per_trunk, tm=1024):
    rows = x.shape[0]
    tm = min(tm, rows_per_trunk)
    d = D_MODEL
    tpt = rows_per_trunk // tm
    rowblk = pl.BlockSpec((tm, d), lambda i, e: (i, 0))
    vec = lambda n: pl.BlockSpec((1, n), lambda i, e: (0, 0))
    return pl.pallas_call(
        _moe_kernel,
        grid=(rows // tm, N_EXPERTS),
        in_specs=[rowblk,
                  pl.BlockSpec((1, 6, SUBLANES, d), lambda i, e: (i // tpt, 0, 0, 0)),
                  pl.BlockSpec((d, N_EXPERTS), lambda i, e: (0, 0)), vec(N_EXPERTS),
                  pl.BlockSpec((1, d, D_EXPERT), lambda i, e: (e, 0, 0)),
                  pl.BlockSpec((1, d, D_EXPERT), lambda i, e: (e, 0, 0)),
                  pl.BlockSpec((1, D_EXPERT, d), lambda i, e: (e, 0, 0)),
                  vec(d), vec(d)],
        out_specs=rowblk,
        out_shape=jax.ShapeDtypeStruct((rows, d), F32),
        scratch_shapes=[pltpu.VMEM((tm, d), BF16), pltpu.VMEM((tm, N_EXPERTS), F32),
                        pltpu.VMEM((tm, d), F32)],
        compiler_params=_params(("arbitrary", "arbitrary")),
        name="moe",
    )(x, pat, w_router, b_router.reshape(1, N_EXPERTS), wg, wu, wd,
      ln2_g.reshape(1, d), ln2_b.reshape(1, d))


def _block_diag_in(bb):
    nd, g, p, n = bb.shape
    eye = jnp.eye(g, dtype=bb.dtype)
    return jnp.einsum('dgpn,gh->dgphn', bb, eye).reshape(nd, g * p, g * n)


def _block_diag_out(cc):
    nd, g, p, n = cc.shape
    eye = jnp.eye(g, dtype=cc.dtype)
    return jnp.einsum('dgpn,gh->dgnhp', cc, eye).reshape(nd, g * n, g * p)


def kernel(x_prompt, x_sample, state_s5, state_rwkv, c, c_ctx, w_ada, b_ada, w_in, s5_a_re, s5_a_im, s5_log_dt, s5_b_re, s5_b_im, s5_c_re, s5_c_im, s5_d, s5_w_glu, rw_mu, rw_w0, rw_w2, rw_a0, rw_a2, rw_g2, rw_k_k, rw_k_a, rw_r_k, rw_lnx_w, rw_lnx_b, w_up_s5, w_up_rwkv, w_out, ln1_g, ln1_b, ln2_g, ln2_b, w_router, b_router, w_exp_gate, w_exp_up, w_exp_down):
    nbc, tc_len, d = x_prompt.shape
    nbs, ts_len, _ = x_sample.shape
    nl = w_ada.shape[0]
    rc = nbc * tc_len
    rs = nbs * ts_len
    assert rc == rs, "both trunks are processed as equal halves of one row-major token matrix"
    assert SUBLANES % nbs == 0 and nbc % SUBLANES == 0

    x = jnp.concatenate([x_prompt.transpose(1, 0, 2).reshape(rc, d),
                         x_sample.transpose(1, 0, 2).reshape(rs, d)], axis=0)

    cond = jnp.concatenate([c_ctx[None], c], axis=0)
    cond8 = jnp.zeros((SUBLANES, d), F32).at[:cond.shape[0]].set(cond)
    mod = _ada_mod(cond8, w_ada, b_ada).reshape(nl, SUBLANES, 6, d)
    ctx_rows = jnp.zeros((SUBLANES,), jnp.int32)
    smp_rows = 1 + jnp.arange(SUBLANES, dtype=jnp.int32) % nbs
    pat_idx = jnp.stack([ctx_rows, smp_rows])
    pats = mod[:, pat_idx]
    pats = pats.transpose(0, 1, 3, 2, 4)

    m = nl * 2 * G_S5
    ab_re, ab_im, bb_re, bb_im = _s5_discretize(
        s5_a_re.reshape(m, N_S5), s5_a_im.reshape(m, N_S5), s5_log_dt.reshape(m, 1),
        s5_b_re.reshape(m, N_S5, S5_GROUP).transpose(0, 2, 1),
        s5_b_im.reshape(m, N_S5, S5_GROUP).transpose(0, 2, 1))
    ab_re = ab_re.reshape(nl, 2, 1, S5_STATE)
    ab_im = ab_im.reshape(nl, 2, 1, S5_STATE)
    bb_re = bb_re.reshape(nl, 2, G_S5, S5_GROUP, N_S5)
    bb_im = bb_im.reshape(nl, 2, G_S5, S5_GROUP, N_S5)

    w_in_bf16 = w_in.astype(BF16)
    zero_s5 = jnp.zeros((2, 2, nbc, S5_STATE), F32)
    zero_rw = jnp.zeros((2, RWKV_HEAD, RWKV_HEAD, nbc * H_RWKV), F32)
    s5_out, rw_out = [], []
    for l in range(nl):
        p = dict(s5_d=s5_d[l], s5_w_glu=s5_w_glu[l], rw_mu=rw_mu[l], rw_w0=rw_w0[l], rw_w2=rw_w2[l],
                 rw_a0=rw_a0[l], rw_a2=rw_a2[l], rw_g2=rw_g2[l], rw_k_k=rw_k_k[l], rw_k_a=rw_k_a[l],
                 rw_r_k=rw_r_k[l], rw_lnx_w=rw_lnx_w[l], rw_lnx_b=rw_lnx_b[l], w_up_s5=w_up_s5[l],
                 w_up_rwkv=w_up_rwkv[l], w_out=w_out[l], ln1_g=ln1_g[l], ln1_b=ln1_b[l])
        pat = pats[l]
        u, zr, gs, gr = _inproj(x, pat, w_in_bf16, l, rc)

        wb_re = _block_diag_in(bb_re[l]).astype(BF16)
        wb_im = _block_diag_in(bb_im[l]).astype(BF16)
        wc = jnp.concatenate([_block_diag_out(s5_c_re[l]), -_block_diag_out(s5_c_im[l])], axis=1).astype(BF16)
        y5c, hfin = _s5_scan(u, wb_re, wb_im, ab_re[l], ab_im[l], wc, zero_s5, nbc, 0, rc)
        h0s = state_s5[:, l].reshape(nbs, 2, 2, S5_STATE).transpose(1, 2, 0, 3)
        y5s, _ = _s5_scan(u, wb_re, wb_im, ab_re[l], ab_im[l], wc, h0s, nbs, rc, rs)
        s5_out.append(hfin.transpose(2, 0, 1, 3).reshape(nbc, 2, 2, G_S5, N_S5))

        prep_c = _rwkv_prep(zr, p, nbc, tc_len, False, 0)
        prep_s = _rwkv_prep(zr, p, nbs, ts_len, True, rc)
        yc, sfin = _rwkv_scan(*_relayout_ctx(prep_c, nbc, tc_len), zero_rw, tc=min(32, tc_len), folded=False)
        ys, _ = _rwkv_scan(*_relayout_smp(prep_s, nbs, ts_len, 0), _rwkv_state_smp(state_rwkv[:, l], nbs),
                           tc=min(64, ts_len // 2), folded=True)
        yrw = (_unmix_ctx(yc, nbc, tc_len), _unmix_smp(ys, nbs, ts_len))
        rw_out.append(_rwkv_state_out_ctx(sfin, nbc))

        x = _mix_out(x, pat, u, gs, gr, (y5c, y5s), yrw, (prep_c[3], prep_s[3]), (prep_c[4], prep_s[4]), p, rc)
        x = _moe(x, pat, w_router, b_router, w_exp_gate[l], w_exp_up[l], w_exp_down[l], ln2_g[l], ln2_b[l], rc)

    y_prompt = x[:rc].reshape(tc_len, nbc, d).transpose(1, 0, 2)
    y_sample = x[rc:].reshape(ts_len, nbs, d).transpose(1, 0, 2)
    return (y_prompt, y_sample, jnp.stack(s5_out, 1), jnp.stack(rw_out, 1))
```

```python
import functools

import numpy as np
import jax
import jax.numpy as jnp
from jax import lax
from jax.experimental import pallas as pl
from jax.experimental.pallas import tpu as pltpu

D_MODEL = 1024
DEPTH = 2
GRID_W = 64
D_S5 = 512
S5_GROUP = 16
G_S5 = 32
N_S5 = 64
S5_STATE = G_S5 * N_S5
D_RWKV = 512
RWKV_HEAD = 64
H_RWKV = 8
LORA_W = 64
LORA_A = 64
LORA_G = 128
RWKV_COLS = 3 * D_RWKV + LORA_W + LORA_A + LORA_G
N_EXPERTS = 16
N_GROUPS = 4
EXPERTS_PER_GROUP = 4
D_EXPERT = 512
ALPHA = (2 * DEPTH) ** 0.25
LN_EPS = 1e-5
GN_EPS = 64e-5

SUBLANES = 8
LANES = 128
VMEM_LIMIT = 56 * 1024 * 1024

F32 = jnp.float32
BF16 = jnp.bfloat16


def _params(sem):
    return pltpu.CompilerParams(dimension_semantics=sem, vmem_limit_bytes=VMEM_LIMIT)


def _dot(a, b):
    return jnp.dot(a.astype(BF16), b.astype(BF16), preferred_element_type=F32)


def _dot_split(x, w_exact):
    hi = x.astype(BF16)
    lo = (x - hi.astype(F32)).astype(BF16)
    return (jnp.dot(hi, w_exact, preferred_element_type=F32)
            + jnp.dot(lo, w_exact, preferred_element_type=F32))


def _sigmoid(x):
    return 1.0 / (1.0 + jnp.exp(-x))


def _silu(x):
    return x * _sigmoid(x)


def _layer_norm(x, g, b):
    mu = jnp.mean(x, -1, keepdims=True)
    xc = x - mu
    var = jnp.mean(xc * xc, -1, keepdims=True)
    return xc * lax.rsqrt(var + LN_EPS) * g + b


def _modulate(x, shift8, scale8):
    rows, d = x.shape
    x3 = x.reshape(rows // SUBLANES, SUBLANES, d)
    return (x3 * (1.0 + scale8)[None] + shift8[None]).reshape(rows, d)


def _to_time_major_kernel(xc_ref, xs_ref, o_ref, *, ctx_tiles):
    i = pl.program_id(0)

    @pl.when(i < ctx_tiles)
    def _():
        nb, steps, _ = xc_ref.shape
        for t in range(steps):
            o_ref[pl.ds(t * nb, nb), :] = xc_ref[:, t, :]

    @pl.when(i >= ctx_tiles)
    def _():
        nb, steps, d = xs_ref.shape
        rows = nb * steps
        r = lax.broadcasted_iota(jnp.int32, (rows, rows), 0)
        c = lax.broadcasted_iota(jnp.int32, (rows, rows), 1)
        perm = ((r % nb) * steps + r // nb == c).astype(BF16)
        o_ref[...] = _permute_rows(perm, xs_ref[...].reshape(rows, d))


def _permute_rows(perm, x):
    hi = x.astype(BF16)
    r1 = x - hi.astype(F32)
    mid = r1.astype(BF16)
    lo = (r1 - mid.astype(F32)).astype(BF16)
    return (jnp.dot(perm, hi, preferred_element_type=F32) + jnp.dot(perm, mid, preferred_element_type=F32)
            + jnp.dot(perm, lo, preferred_element_type=F32))


def _to_time_major(x_ctx, x_smp, tm=256):
    nbc, tcl, d = x_ctx.shape
    nbs, tsl, _ = x_smp.shape
    ct = nbc * tcl // tm
    st = nbs * tsl // tm
    return pl.pallas_call(
        functools.partial(_to_time_major_kernel, ctx_tiles=ct),
        grid=(ct + st,),
        in_specs=[pl.BlockSpec((nbc, tm // nbc, d), lambda i: (0, jnp.minimum(i, ct - 1), 0)),
                  pl.BlockSpec((nbs, tm // nbs, d), lambda i: (0, jnp.maximum(i - ct, 0), 0))],
        out_specs=pl.BlockSpec((tm, d), lambda i: (i, 0)),
        out_shape=jax.ShapeDtypeStruct((nbc * tcl + nbs * tsl, d), F32),
        compiler_params=_params(("arbitrary",)),
        name="to_time_major",
    )(x_ctx, x_smp)


def _from_time_major_kernel(x_ref, o_ref):
    nb, steps, _ = o_ref.shape
    if nb >= SUBLANES:
        for t in range(steps):
            o_ref[:, t, :] = x_ref[pl.ds(t * nb, nb), :]
    else:
        rows = nb * steps
        r = lax.broadcasted_iota(jnp.int32, (rows, rows), 0)
        c = lax.broadcasted_iota(jnp.int32, (rows, rows), 1)
        perm = ((r % steps) * nb + r // steps == c).astype(BF16)
        o_ref[...] = _permute_rows(perm, x_ref[...]).reshape(o_ref.shape)


def _from_time_major(x, nb, seq, row0, tm=256):
    d = x.shape[1]
    t0 = row0 // tm
    return pl.pallas_call(
        _from_time_major_kernel,
        grid=(seq * nb // tm,),
        in_specs=[pl.BlockSpec((tm, d), lambda i: (t0 + i, 0))],
        out_specs=pl.BlockSpec((nb, tm // nb, d), lambda i: (0, i, 0)),
        out_shape=jax.ShapeDtypeStruct((nb, seq, d), F32),
        compiler_params=_params(("arbitrary",)),
        name="from_time_major",
    )(x)


def _ada_kernel(cond_ref, w_ref, b_ref, o_ref):
    c = cond_ref[...]
    o_ref[0] = _dot(_silu(c), w_ref[0]) + b_ref[0]


def _ada_mod(cond8, w_ada, b_ada):
    nl = w_ada.shape[0]
    d = D_MODEL
    return pl.pallas_call(
        _ada_kernel,
        grid=(nl, 6),
        in_specs=[pl.BlockSpec((SUBLANES, d), lambda l, k: (0, 0)),
                  pl.BlockSpec((1, d, d), lambda l, k: (l, 0, k)),
                  pl.BlockSpec((1, 1, d), lambda l, k: (l, 0, k))],
        out_specs=pl.BlockSpec((1, SUBLANES, d), lambda l, k: (l, 0, k)),
        out_shape=jax.ShapeDtypeStruct((nl, SUBLANES, 6 * d), F32),
        compiler_params=_params(("arbitrary", "arbitrary")),
        name="ada_mod",
    )(cond8, w_ada, b_ada.reshape(nl, 1, 6 * d))


IN_SPLITS = (0, D_S5, D_S5 + RWKV_COLS, D_S5 + RWKV_COLS + D_MODEL, D_S5 + RWKV_COLS + 2 * D_MODEL)


def _inproj_kernel(x_ref, pat_ref, w_ref, u_ref, zr_ref, gs_ref, gr_ref):
    h = _modulate(x_ref[...], pat_ref[0, 0], pat_ref[0, 1]).astype(BF16)
    for k, out in enumerate((u_ref, zr_ref, gs_ref, gr_ref)):
        out[...] = jnp.dot(h, w_ref[0, :, IN_SPLITS[k]:IN_SPLITS[k + 1]], preferred_element_type=F32)


def _inproj(x, pat, w_in_bf16, layer, rows_per_trunk, tm=256):
    rows = x.shape[0]
    d = D_MODEL
    tpt = rows_per_trunk // tm
    widths = [IN_SPLITS[k + 1] - IN_SPLITS[k] for k in range(4)]
    rowblk = lambda n: pl.BlockSpec((tm, n), lambda i: (i, 0))
    return pl.pallas_call(
        _inproj_kernel,
        grid=(rows // tm,),
        in_specs=[rowblk(d),
                  pl.BlockSpec((1, 6, SUBLANES, d), lambda i: (i // tpt, 0, 0, 0)),
                  pl.BlockSpec((1, d, IN_SPLITS[-1]), lambda i: (layer, 0, 0))],
        out_specs=[rowblk(n) for n in widths],
        out_shape=[jax.ShapeDtypeStruct((rows, n), F32) for n in widths],
        compiler_params=_params(("arbitrary",)),
        name="in_proj",
    )(x, pat, w_in_bf16)


def _s5_disc_kernel(are_ref, aim_ref, ldt_ref, bre_ref, bim_ref, cre_ref, cim_ref,
                    abre_ref, abim_ref, wbre_ref, wbim_ref, wc_ref):
    a_re = jnp.minimum(are_ref[0], -1e-4)
    a_im = aim_ref[0]
    dt = jnp.exp(ldt_ref[0])
    mag = jnp.exp(a_re * dt)
    ab_re = mag * jnp.cos(a_im * dt)
    ab_im = mag * jnp.sin(a_im * dt)
    den = a_re * a_re + a_im * a_im
    nr = ab_re - 1.0
    q_re = (nr * a_re + ab_im * a_im) / den
    q_im = (ab_im * a_re - nr * a_im) / den
    abre_ref[0] = ab_re
    abim_ref[0] = ab_im
    b_re = bre_ref[0]
    b_im = bim_ref[0]
    bb_re = (q_re[:, None, :] * b_re - q_im[:, None, :] * b_im).reshape(D_S5, N_S5)
    bb_im = (q_re[:, None, :] * b_im + q_im[:, None, :] * b_re).reshape(D_S5, N_S5)

    def spread(x, n_rep, row_group, col_group):
        rows, w = x.shape
        tiled_eye = (lax.broadcasted_iota(jnp.int32, (w, n_rep * w), 0)
                     == lax.broadcasted_iota(jnp.int32, (w, n_rep * w), 1) % w).astype(BF16)
        rep = jnp.dot(x.astype(BF16), tiled_eye, preferred_element_type=F32)
        keep = (lax.broadcasted_iota(jnp.int32, rep.shape, 0) // row_group
                == lax.broadcasted_iota(jnp.int32, rep.shape, 1) // col_group)
        return jnp.where(keep, rep, 0.0).astype(BF16)

    wbre_ref[0] = spread(bb_re, G_S5, S5_GROUP, N_S5)
    wbim_ref[0] = spread(bb_im, G_S5, S5_GROUP, N_S5)
    wc_ref[0, :S5_STATE] = spread(cre_ref[0], G_S5, N_S5, S5_GROUP)
    wc_ref[0, S5_STATE:] = spread(-cim_ref[0], G_S5, N_S5, S5_GROUP)


def _s5_weights(a_re, a_im, log_dt, b_re, b_im, c_re, c_im):
    m = a_re.shape[0]
    s = S5_STATE
    blk = lambda *shape: pl.BlockSpec((1,) + shape, lambda i: (i,) + (0,) * len(shape))
    return pl.pallas_call(
        _s5_disc_kernel,
        grid=(m,),
        in_specs=[blk(G_S5, N_S5), blk(G_S5, N_S5), blk(G_S5, 1), blk(G_S5, S5_GROUP, N_S5),
                  blk(G_S5, S5_GROUP, N_S5), blk(s, S5_GROUP), blk(s, S5_GROUP)],
        out_specs=[blk(G_S5, N_S5), blk(G_S5, N_S5), blk(D_S5, s), blk(D_S5, s), blk(2 * s, D_S5)],
        out_shape=[jax.ShapeDtypeStruct((m, G_S5, N_S5), F32), jax.ShapeDtypeStruct((m, G_S5, N_S5), F32),
                   jax.ShapeDtypeStruct((m, D_S5, s), BF16), jax.ShapeDtypeStruct((m, D_S5, s), BF16),
                   jax.ShapeDtypeStruct((m, 2 * s, D_S5), BF16)],
        compiler_params=_params(("arbitrary",)),
        name="s5_weights",
    )(a_re, a_im, log_dt, b_re, b_im, c_re, c_im)


def _s5_scan_kernel(u_ref, wbre_ref, wbim_ref, abre_ref, abim_ref, wc_ref, h0_ref,
                    y_ref, hfin_ref, bure_ref, buim_ref, hre_ref, him_ref, *, nb, lane_w):
    d = pl.program_id(0)
    c = pl.program_id(1)
    nc = pl.num_programs(1)
    rows = u_ref.shape[0]

    @pl.when(c == 0)
    def _():
        hre_ref[...] = h0_ref[0, 0]
        him_ref[...] = h0_ref[0, 1]

    ub = u_ref[...].astype(BF16)
    ct = 2 * LANES
    for j in range(S5_STATE // ct):
        first_channel = j * ct * S5_GROUP // N_S5
        ks = slice(first_channel // ct * ct, first_channel // ct * ct + ct)
        js = slice(ct * j, ct * (j + 1))
        bure_ref[:, js] = jnp.dot(ub[:, ks], wbre_ref[0, ks, js], preferred_element_type=F32)
        buim_ref[:, js] = jnp.dot(ub[:, ks], wbim_ref[0, ks, js], preferred_element_type=F32)

    fwd = d == 0
    if nb >= SUBLANES:
        steps = rows // nb
        for lc in range(S5_STATE // lane_w):
            ls = slice(lc * lane_w, (lc + 1) * lane_w)
            ar = jnp.broadcast_to(abre_ref[0, :, ls], (nb, lane_w))
            ai = jnp.broadcast_to(abim_ref[0, :, ls], (nb, lane_w))

            def body(s, carry, ls=ls, ar=ar, ai=ai):
                hr, hi = carry
                t = jnp.where(fwd, s, steps - 1 - s)
                r0 = pl.multiple_of(t * nb, nb)
                br = bure_ref[pl.ds(r0, nb), ls]
                bi = buim_ref[pl.ds(r0, nb), ls]
                nr = ar * hr - ai * hi + br
                ni = ar * hi + ai * hr + bi
                bure_ref[pl.ds(r0, nb), ls] = nr
                buim_ref[pl.ds(r0, nb), ls] = ni
                return nr, ni

            hr, hi = lax.fori_loop(0, steps, body, (hre_ref[:, ls], him_ref[:, ls]))
            hre_ref[:, ls] = hr
            him_ref[:, ls] = hi
    else:
        per = SUBLANES // nb
        groups = rows // SUBLANES
        row_id = lax.broadcasted_iota(jnp.int32, (SUBLANES, lane_w), 0) // nb
        shift = jnp.where(fwd, nb, SUBLANES - nb)
        for lc in range(S5_STATE // lane_w):
            ls = slice(lc * lane_w, (lc + 1) * lane_w)
            ar = jnp.broadcast_to(abre_ref[0, :, ls], (SUBLANES, lane_w))
            ai = jnp.broadcast_to(abim_ref[0, :, ls], (SUBLANES, lane_w))
            h0r = hre_ref[:, ls]
            h0i = him_ref[:, ls]

            def body(gidx, carry, ls=ls, ar=ar, ai=ai):
                tr, ti = carry
                g = jnp.where(fwd, gidx, groups - 1 - gidx)
                r0 = pl.multiple_of(g * SUBLANES, SUBLANES)
                br = bure_ref[pl.ds(r0, SUBLANES), ls]
                bi = buim_ref[pl.ds(r0, SUBLANES), ls]
                for k in range(per):
                    pr = pltpu.roll(tr, shift, 0)
                    pi = pltpu.roll(ti, shift, 0)
                    nr = ar * pr - ai * pi + br
                    ni = ar * pi + ai * pr + bi
                    tgt = jnp.where(fwd, k, per - 1 - k)
                    sel = row_id == tgt
                    tr = jnp.where(sel, nr, tr)
                    ti = jnp.where(sel, ni, ti)
                bure_ref[pl.ds(r0, SUBLANES), ls] = tr
                buim_ref[pl.ds(r0, SUBLANES), ls] = ti
                return tr, ti

            tr, ti = lax.fori_loop(0, groups, body, (h0r, h0i))
            hre_ref[:, ls] = tr
            him_ref[:, ls] = ti

    for q in range(D_S5 // ct):
        qs = slice(ct * q, ct * (q + 1))
        ss = slice(q * ct * N_S5 // S5_GROUP, (q + 1) * ct * N_S5 // S5_GROUP)
        si = slice(S5_STATE + ss.start, S5_STATE + ss.stop)
        y_ref[0, :, qs] = (jnp.dot(bure_ref[:, ss].astype(BF16), wc_ref[0, ss, qs], preferred_element_type=F32)
                           + jnp.dot(buim_ref[:, ss].astype(BF16), wc_ref[0, si, qs],
                                     preferred_element_type=F32))

    @pl.when(c == nc - 1)
    def _():
        hfin_ref[0, 0] = hre_ref[...]
        hfin_ref[0, 1] = him_ref[...]


def _s5_scan(u, wb_re, wb_im, ab_re, ab_im, wc, h0, nb, row0, rows, chunk_rows=512):
    if nb < SUBLANES:
        h0 = jnp.tile(h0, (1, 1, SUBLANES // nb, 1))
    srows = max(nb, SUBLANES)
    cr = min(chunk_rows, rows)
    nc = rows // cr
    c0 = row0 // cr
    s = S5_STATE
    lane_w = 256 if nb >= SUBLANES else 512
    chunk = lambda d, c: c + d * (nc - 1 - 2 * c)
    y, hfin = pl.pallas_call(
        functools.partial(_s5_scan_kernel, nb=nb, lane_w=lane_w),
        grid=(2, nc),
        in_specs=[pl.BlockSpec((cr, D_S5), lambda d, c: (c0 + chunk(d, c), 0)),
                  pl.BlockSpec((1, D_S5, s), lambda d, c: (d, 0, 0)),
                  pl.BlockSpec((1, D_S5, s), lambda d, c: (d, 0, 0)),
                  pl.BlockSpec((1, 1, s), lambda d, c: (d, 0, 0)),
                  pl.BlockSpec((1, 1, s), lambda d, c: (d, 0, 0)),
                  pl.BlockSpec((1, 2 * s, D_S5), lambda d, c: (d, 0, 0)),
                  pl.BlockSpec((1, 2, srows, s), lambda d, c: (d, 0, 0, 0))],
        out_specs=[pl.BlockSpec((1, cr, D_S5), lambda d, c: (d, chunk(d, c), 0)),
                   pl.BlockSpec((1, 2, srows, s), lambda d, c: (d, 0, 0, 0))],
        out_shape=[jax.ShapeDtypeStruct((2, rows, D_S5), F32),
                   jax.ShapeDtypeStruct((2, 2, srows, s), F32)],
        scratch_shapes=[pltpu.VMEM((cr, s), F32), pltpu.VMEM((cr, s), F32),
                        pltpu.VMEM((srows, s), F32), pltpu.VMEM((srows, s), F32)],
        compiler_params=_params(("arbitrary", "arbitrary")),
        name="s5_scan",
    )(u, wb_re, wb_im, ab_re, ab_im, wc, h0)
    if nb < SUBLANES:
        hfin = jnp.stack([hfin[0, :, srows - nb:], hfin[1, :, :nb]])
    return y, hfin


def _rwkv_prep_kernel(prev_ref, cur_ref, next_ref, mu_ref, ones_ref, g2_ref, kk_ref, ka_ref, rk_ref,
                      w0_ref, w2_ref, a0_ref, a2_ref,
                      r_ref, v_ref, nkk_ref, bonus_ref, g_ref, w_ref, kd_ref, b_ref, *, nb, seq, grid_shift):
    tm = cur_ref.shape[0]
    i = pl.program_id(0)
    cur = cur_ref[...]
    prv = prev_ref[...]
    nxt = next_ref[...]
    row = lax.broadcasted_iota(jnp.int32, (tm, RWKV_COLS), 0)
    lane = lax.broadcasted_iota(jnp.int32, (tm, RWKV_COLS), 1)
    t = (i * tm + row) // nb

    def rows_before(s):
        if s == tm:
            return prv
        return jnp.where(row < s, pltpu.roll(prv, s, 0), pltpu.roll(cur, s, 0))

    def rows_after(s):
        if s == tm:
            return nxt
        return jnp.where(row >= tm - s, pltpu.roll(nxt, tm - s, 0), pltpu.roll(cur, tm - s, 0))

    if grid_shift:
        tw = t % GRID_W
        left = jnp.where(tw == 0, 0.0, rows_before(nb))
        right = jnp.where(tw == GRID_W - 1, 0.0, rows_after(nb))
        up = jnp.where(t < GRID_W, 0.0, rows_before(nb * GRID_W))
        down = jnp.where(t >= seq - GRID_W, 0.0, rows_after(nb * GRID_W))
        m4 = lane % 4
        sh = jnp.where(m4 == 0, left, jnp.where(m4 == 1, right, jnp.where(m4 == 2, up, down)))
    else:
        before = jnp.where(t == 0, 0.0, rows_before(nb))
        after = jnp.where(t == seq - 1, 0.0, rows_after(nb))
        sh = jnp.where(lane % 2 == 0, before, after)

    z = cur + (sh - cur) * mu_ref[...]
    r = z[:, 0:D_RWKV]
    k = z[:, D_RWKV:2 * D_RWKV]
    v = z[:, 2 * D_RWKV:3 * D_RWKV]
    xw = z[:, 3 * D_RWKV:3 * D_RWKV + LORA_W]
    xa = z[:, 3 * D_RWKV + LORA_W:3 * D_RWKV + LORA_W + LORA_A]
    xg = z[:, 3 * D_RWKV + LORA_W + LORA_A:]
    ones = ones_ref[...]

    g_ref[...] = _dot(_sigmoid(xg), g2_ref[...])
    kk = k * kk_ref[...]
    nrm = jnp.sqrt(_dot_split(kk * kk, ones))
    kk = kk / jnp.maximum(nrm, 1e-12)
    r_ref[...] = r
    v_ref[...] = v
    nkk_ref[...] = -kk
    bonus_ref[...] = _dot_split(r * k * rk_ref[...], ones) * v
    txw = jnp.tanh(xw)
    for d in range(2):
        zw = -(w0_ref[d] + _dot(txw, w2_ref[d]))
        softplus = jnp.maximum(zw, 0.0) + jnp.log(1.0 + jnp.exp(-jnp.abs(zw)))
        w_log = -softplus - 0.5
        w_ref[d] = jnp.exp(-jnp.exp(w_log))
        a = _sigmoid(a0_ref[d] + _dot(xa, a2_ref[d]))
        kd_ref[d] = k * (1.0 + (a - 1.0) * ka_ref[...])
        b_ref[d] = kk * a


def _rwkv_prep(zr, p, nb, seq, grid_shift, row0, tm=128):
    rows = seq * nb
    nt = rows // tm
    t0 = row0 // tm
    if grid_shift:
        assert tm == nb * GRID_W, "one tile must be one grid row of the latent grid"
    else:
        assert tm % nb == 0 and tm >= nb
    kern = functools.partial(_rwkv_prep_kernel, nb=nb, seq=seq, grid_shift=grid_shift)
    c = D_RWKV
    vec = lambda n: pl.BlockSpec((1, n), lambda i: (0, 0))
    mat = lambda a, b: pl.BlockSpec((a, b), lambda i: (0, 0))
    row_out = pl.BlockSpec((tm, c), lambda i: (i, 0))
    dir_out = pl.BlockSpec((2, tm, c), lambda i: (0, i, 0))
    head_ones = jnp.kron(jnp.eye(H_RWKV, dtype=F32), jnp.ones((RWKV_HEAD, RWKV_HEAD), F32)).astype(BF16)
    in_specs = [pl.BlockSpec((tm, RWKV_COLS), lambda i: (t0 + jnp.maximum(i - 1, 0), 0)),
                pl.BlockSpec((tm, RWKV_COLS), lambda i: (t0 + i, 0)),
                pl.BlockSpec((tm, RWKV_COLS), lambda i: (t0 + jnp.minimum(i + 1, nt - 1), 0)),
                vec(RWKV_COLS), mat(c, c), mat(LORA_G, c), vec(c), vec(c), vec(c),
                pl.BlockSpec((2, 1, c), lambda i: (0, 0, 0)),
                pl.BlockSpec((2, LORA_W, c), lambda i: (0, 0, 0)),
                pl.BlockSpec((2, 1, c), lambda i: (0, 0, 0)),
                pl.BlockSpec((2, LORA_A, c), lambda i: (0, 0, 0))]
    args = [zr, zr, zr, p['rw_mu'].reshape(1, -1), head_ones, p['rw_g2'], p['rw_k_k'].reshape(1, c),
            p['rw_k_a'].reshape(1, c), p['rw_r_k'].reshape(1, c), p['rw_w0'].reshape(2, 1, c), p['rw_w2'],
            p['rw_a0'].reshape(2, 1, c), p['rw_a2']]
    part = jax.ShapeDtypeStruct((rows, c), F32)
    return pl.pallas_call(
        kern,
        grid=(nt,),
        in_specs=in_specs,
        out_specs=[row_out] * 5 + [dir_out] * 3,
        out_shape=[part] * 5 + [jax.ShapeDtypeStruct((2, rows, c), F32)] * 3,
        compiler_params=_params(("arbitrary",)),
        name="rwkv_prep",
    )(*args)


def _rwkv_scan_kernel(a_ref, w_ref, b_ref, k_ref, r_ref, v_ref, s0_ref, y_ref, sfin_ref, s_ref, *,
                      nacc, row_blk, folded):
    d = pl.program_id(0)
    c = pl.program_id(1)
    nc = pl.num_programs(1)
    tc = a_ref.shape[1]
    nj = RWKV_HEAD
    j_unroll = 16 if row_blk >= 32 else nj
    rev = (c >= nc // 2) if folded else (d == 1)

    @pl.when(c == 0)
    def _():
        s_ref[...] = s0_ref[0]

    def step(s, carry):
        t = jnp.where(rev, tc - 1 - s, s)
        zero = jnp.zeros((row_blk, a_ref.shape[3]), F32)
        for rb in range(s_ref.shape[1] // row_blk):
            rows = pl.ds(rb * row_blk, row_blk)

            def pass1(jo, acc):
                acc = list(acc)
                for ji in range(j_unroll):
                    j = jo * j_unroll + ji
                    acc[ji % nacc] = acc[ji % nacc] + s_ref[j, rows, :] * a_ref[0, t, pl.ds(j, 1), :]
                return tuple(acc)

            sa = functools.reduce(lambda x, y: x + y, lax.fori_loop(0, nj // j_unroll, pass1, (zero,) * nacc))
            vv = v_ref[0, t, rows, :]

            def pass2(jo, acc):
                acc = list(acc)
                for ji in range(j_unroll):
                    j = jo * j_unroll + ji
                    s_new = (s_ref[j, rows, :] * w_ref[0, t, pl.ds(j, 1), :]
                             + sa * b_ref[0, t, pl.ds(j, 1), :]
                             + vv * k_ref[0, t, pl.ds(j, 1), :])
                    s_ref[j, rows, :] = s_new
                    acc[ji % nacc] = acc[ji % nacc] + s_new * r_ref[0, t, pl.ds(j, 1), :]
                return tuple(acc)

            y_ref[0, t, rows, :] = functools.reduce(
                lambda x, y: x + y, lax.fori_loop(0, nj // j_unroll, pass2, (zero,) * nacc))
        return carry

    lax.fori_loop(0, tc, step, 0)

    @pl.when(c == nc - 1)
    def _():
        sfin_ref[0] = s_ref[...]


def _rwkv_scan(a, w, b, k, r, v, s0, tc, folded):
    _, tp, _, nl = w.shape
    iv = v.shape[2]
    row_blk = min(iv, 32)
    nacc = 4 if row_blk <= 16 else 2
    nc_half = tp // tc
    if folded:
        nd, nc = 1, 2 * nc_half
        tmap = lambda d, c: (c // nc_half, jnp.where(c < nc_half, c, nc - 1 - c), 0, 0)
        shared = per_dir = tmap
    else:
        nd, nc = 2, nc_half
        chunk = lambda d, c: c + d * (nc - 1 - 2 * c)
        shared = lambda d, c: (0, chunk(d, c), 0, 0)
        per_dir = lambda d, c: (d, chunk(d, c), 0, 0)
    key = lambda m: pl.BlockSpec((1, tc, RWKV_HEAD, nl), m)
    val = lambda m: pl.BlockSpec((1, tc, iv, nl), m)
    sspec = pl.BlockSpec((1, RWKV_HEAD, iv, nl), lambda d, c: (d, 0, 0, 0))
    return pl.pallas_call(
        functools.partial(_rwkv_scan_kernel, nacc=nacc, row_blk=row_blk, folded=folded),
        grid=(nd, nc),
        in_specs=[key(shared), key(per_dir), key(per_dir), key(per_dir), key(shared), val(shared), sspec],
        out_specs=[val(per_dir), sspec],
        out_shape=[jax.ShapeDtypeStruct((2, tp, iv, nl), F32),
                   jax.ShapeDtypeStruct((nd, RWKV_HEAD, iv, nl), F32)],
        scratch_shapes=[pltpu.VMEM((RWKV_HEAD, iv, nl), F32)],
        compiler_params=_params(("arbitrary", "arbitrary")),
        name="rwkv_scan",
    )(a, w, b, k, r, v, s0)


SMP_REP = 4
SMP_ROWS = RWKV_HEAD // SMP_REP


def _block_transpose(tiles, bs):
    tiles = list(tiles)
    lane = lax.broadcasted_iota(jnp.int32, tiles[0].shape, 1)
    for kbit in range(3):
        sft = bs << kbit
        bit = (lane // sft) % 2 == 1
        for r in range(8):
            if (r >> kbit) & 1:
                continue
            r2 = r | (1 << kbit)
            lo, hi = tiles[r], tiles[r2]
            tiles[r] = jnp.where(bit, pltpu.roll(hi, sft, 1), lo)
            tiles[r2] = jnp.where(bit, hi, pltpu.roll(lo, LANES - sft, 1))
    return tiles


def _head_tiles(x):
    xt = x.T
    return [xt[h * RWKV_HEAD:(h + 1) * RWKV_HEAD, :] for h in range(H_RWKV)]


def _from_head_tiles(tiles):
    return jnp.concatenate(tiles, axis=0).T


def _relayout_ctx_kernel(r_ref, v_ref, nkk_ref, w_ref, kd_ref, b_ref, a_o, r_o, v_o, w_o, b_o, k_o, *, nb):
    steps = a_o.shape[1]
    for src, dst in ((nkk_ref, a_o), (r_ref, r_o), (v_ref, v_o)):
        tiles = _block_transpose(_head_tiles(src[...]), nb)
        for t in range(steps):
            dst[0, t] = tiles[t]
    for src, dst in ((w_ref, w_o), (b_ref, b_o), (kd_ref, k_o)):
        for d in range(2):
            tiles = _block_transpose(_head_tiles(src[d]), nb)
            for t in range(steps):
                dst[d, t] = tiles[t]


def _relayout_ctx(prep, nb, seq):
    r, v, nkk, _, _, w, kd, bb = prep
    steps = LANES // nb
    assert steps == H_RWKV and nb * H_RWKV == LANES
    tm = steps * nb
    nt = seq // steps
    c = D_RWKV
    nat = pl.BlockSpec((tm, c), lambda i: (i, 0))
    nat2 = pl.BlockSpec((2, tm, c), lambda i: (0, i, 0))
    out1 = pl.BlockSpec((1, steps, RWKV_HEAD, LANES), lambda i: (0, i, 0, 0))
    out2 = pl.BlockSpec((2, steps, RWKV_HEAD, LANES), lambda i: (0, i, 0, 0))
    s1 = jax.ShapeDtypeStruct((1, seq, RWKV_HEAD, LANES), F32)
    s2 = jax.ShapeDtypeStruct((2, seq, RWKV_HEAD, LANES), F32)
    a, r_, v_, w_, b_, k_ = pl.pallas_call(
        functools.partial(_relayout_ctx_kernel, nb=nb),
        grid=(nt,),
        in_specs=[nat, nat, nat, nat2, nat2, nat2],
        out_specs=[out1, out1, out1, out2, out2, out2],
        out_shape=[s1, s1, s1, s2, s2, s2],
        compiler_params=_params(("arbitrary",)),
        name="rwkv_relayout_ctx",
    )(r, v, nkk, w, kd, bb)
    return a, w_, b_, k_, r_, v_


def _unmix_ctx_kernel(y_ref, o_ref, *, nb):
    steps = y_ref.shape[1]
    tiles = [y_ref[0, t] + y_ref[1, t] for t in range(steps)]
    o_ref[...] = _from_head_tiles(_block_transpose(tiles, nb))


def _unmix_ctx(y, nb, seq):
    steps = LANES // nb
    tm = steps * nb
    return pl.pallas_call(
        functools.partial(_unmix_ctx_kernel, nb=nb),
        grid=(seq // steps,),
        in_specs=[pl.BlockSpec((2, steps, RWKV_HEAD, LANES), lambda i: (0, i, 0, 0))],
        out_specs=pl.BlockSpec((tm, D_RWKV), lambda i: (i, 0)),
        out_shape=jax.ShapeDtypeStruct((seq * nb, D_RWKV), F32),
        compiler_params=_params(("arbitrary",)),
        name="rwkv_unmix_ctx",
    )(y)


def _relayout_smp_kernel(ra, rb, va, vb, na, nb_, wa, wb, ka, kb, ba, bb, sel_ref,
                         a_o, r_o, v_o, w_o, b_o, k_o):
    steps = a_o.shape[1]
    lane_v = lax.broadcasted_iota(jnp.int32, (SMP_ROWS, LANES), 1)

    def conv(xa, xb):
        tiles = _block_transpose(_head_tiles(jnp.concatenate([xa, xb], axis=0)), 2)
        parts = []
        for x in tiles:
            hi = x.astype(BF16)
            r1 = x - hi.astype(F32)
            mid = r1.astype(BF16)
            lo = (r1 - mid.astype(F32)).astype(BF16)
            parts.append((hi, mid, lo))
        return parts

    def expand(o0, o1, half):
        order0 = range(8) if half == 0 else range(7, -1, -1)
        order1 = range(7, -1, -1) if half == 0 else range(8)
        cols = []
        for term in range(3):
            cols.append(jnp.concatenate([o0[s][term] for s in order0], axis=0))
            cols.append(jnp.concatenate([o1[s][term] for s in order1], axis=0))
        lhs = jnp.concatenate(cols, axis=1)
        out = [None] * steps
        for pair in range(steps // 16):
            res = jnp.dot(lhs, sel_ref[half, pair], preferred_element_type=F32)
            for q in range(2):
                for s_lo in range(8):
                    out[8 * (2 * pair + q) + s_lo] = res[64 * s_lo:64 * (s_lo + 1), LANES * q:LANES * (q + 1)]
        return out

    def value_rows(full):
        blk = lane_v // (LANES // SMP_REP)
        rows = [full[k * SMP_ROWS:(k + 1) * SMP_ROWS] for k in range(SMP_REP)]
        return jnp.where(blk == 0, rows[0], jnp.where(blk == 1, rows[1], jnp.where(blk == 2, rows[2], rows[3])))

    for srca, srcb, dst in ((na, nb_, a_o), (ra, rb, r_o), (va, vb, v_o)):
        o = conv(srca[...], srcb[...])
        for half in range(2):
            tiles = expand(o, o, half)
            for s in range(steps):
                dst[half, s] = value_rows(tiles[s]) if dst is v_o else tiles[s]
    for srca, srcb, dst in ((wa, wb, w_o), (ba, bb, b_o), (ka, kb, k_o)):
        o0 = conv(srca[0], srcb[0])
        o1 = conv(srca[1], srcb[1])
        for half in range(2):
            tiles = expand(o0, o1, half)
            for s in range(steps):
                dst[half, s] = tiles[s]


def _smp_selection():
    sel = np.zeros((2, 2, 6, LANES, 2 * LANES), np.float32)
    for half in range(2):
        for pair in range(2):
            for q in range(2):
                s_hi = 2 * pair + q
                grp_a = 0 * 64 + 16 * s_hi
                grp_b = 1 * 64 + 16 * (3 - s_hi)
                src = (grp_a, grp_b) if half == 0 else (grp_b, grp_a)
                for d in range(2):
                    for blk in range(SMP_REP):
                        for hb in range(16):
                            col = q * LANES + blk * 32 + d * 16 + hb
                            for term in range(3):
                                sel[half, pair, 2 * term + d, src[d] + hb, col] = 1.0
    return jnp.asarray(sel.reshape(2, 2, 6 * LANES, 2 * LANES), BF16)


def _relayout_smp(prep, nb, seq, row0):
    r, v, nkk, _, _, w, kd, bb = prep
    assert 2 * nb * H_RWKV * SMP_REP == LANES
    steps = 32
    tm = steps * nb
    nt = seq // steps
    half_t = seq // 2
    c = D_RWKV
    t0 = row0 // tm
    blk_a = pl.BlockSpec((tm, c), lambda g: (t0 + g, 0))
    blk_b = pl.BlockSpec((tm, c), lambda g: (t0 + nt - 1 - g, 0))
    blk2_a = pl.BlockSpec((2, tm, c), lambda g: (0, t0 + g, 0))
    blk2_b = pl.BlockSpec((2, tm, c), lambda g: (0, t0 + nt - 1 - g, 0))
    okey = pl.BlockSpec((2, steps, RWKV_HEAD, LANES), lambda g: (0, g, 0, 0))
    oval = pl.BlockSpec((2, steps, SMP_ROWS, LANES), lambda g: (0, g, 0, 0))
    skey = jax.ShapeDtypeStruct((2, half_t, RWKV_HEAD, LANES), F32)
    sval = jax.ShapeDtypeStruct((2, half_t, SMP_ROWS, LANES), F32)
    a, r_, v_, w_, b_, k_ = pl.pallas_call(
        _relayout_smp_kernel,
        grid=(nt // 2,),
        in_specs=[blk_a, blk_b, blk_a, blk_b, blk_a, blk_b, blk2_a, blk2_b, blk2_a, blk2_b, blk2_a, blk2_b,
                  pl.BlockSpec((2, 2, 6 * LANES, 2 * LANES), lambda g: (0, 0, 0, 0))],
        out_specs=[okey, okey, oval, okey, okey, okey],
        out_shape=[skey, skey, sval, skey, skey, skey],
        compiler_params=_params(("arbitrary",)),
        name="rwkv_relayout_smp",
    )(r, r, v, v, nkk, nkk, w, w, kd, kd, bb, bb, _smp_selection())
    return a, w_, b_, k_, r_, v_


def _unmix_smp_kernel(y_ref, o_ref, *, half_tiles):
    q = pl.program_id(0)
    steps = y_ref.shape[1]
    low = q < half_tiles
    h0 = jnp.where(low, 0, 1)
    lane = lax.broadcasted_iota(jnp.int32, (SMP_ROWS, LANES), 1)
    tiles = []
    for s_lo in range(8):
        blocks = []
        for blk in range(SMP_REP):
            acc = jnp.zeros((SMP_ROWS, LANES), F32)
            for s_hi in range(steps // 8):
                s = 8 * s_hi + s_lo
                u = jnp.where(low, s, steps - 1 - s)
                d0 = y_ref[h0, u]
                d1 = y_ref[1 - h0, u]
                sh0 = (16 * s_hi - 32 * blk) % LANES
                sh1 = (16 * s_hi - 32 * blk - 16) % LANES
                t0 = d0 if sh0 == 0 else pltpu.roll(d0, sh0, 1)
                t1 = d1 if sh1 == 0 else pltpu.roll(d1, sh1, 1)
                acc = jnp.where(lane // 16 == s_hi, t0 + t1, acc)
            blocks.append(acc)
        tiles.append(jnp.concatenate(blocks, axis=0))
    o_ref[...] = _from_head_tiles(_block_transpose(tiles, 2))


def _unmix_smp(y, nb, seq):
    steps = 64
    tm = steps * nb
    assert tm == LANES
    nq = seq // steps
    half_tiles = nq // 2
    return pl.pallas_call(
        functools.partial(_unmix_smp_kernel, half_tiles=half_tiles),
        grid=(nq,),
        in_specs=[pl.BlockSpec((2, steps, SMP_ROWS, LANES),
                               lambda q: (0, jnp.where(q < half_tiles, q, nq - 1 - q), 0, 0))],
        out_specs=pl.BlockSpec((tm, D_RWKV), lambda q: (q, 0)),
        out_shape=jax.ShapeDtypeStruct((seq * nb, D_RWKV), F32),
        compiler_params=_params(("arbitrary",)),
        name="rwkv_unmix_smp",
    )(y)


def _rwkv_state_smp(state_l, nb):
    s = state_l.reshape(nb, 2, H_RWKV, SMP_REP, SMP_ROWS, RWKV_HEAD)
    s = s.transpose(5, 4, 3, 1, 2, 0)
    return s.reshape(1, RWKV_HEAD, SMP_ROWS, LANES)


def _rwkv_state_out_ctx(sfin, nb):
    s = sfin.reshape(2, RWKV_HEAD, RWKV_HEAD, H_RWKV, nb)
    return s.transpose(4, 0, 3, 2, 1)


def _mix_out_kernel(x_ref, pat_ref, u_ref, gs_ref, gr_ref,
                    y5c_ref, y5s_ref, yrwc_ref, yrws_ref, bonc_ref, bons_ref, gc_ref, gs2_ref,
                    d_ref, wglu_ref, ones_ref, lnxw_ref, lnxb_ref, wups_ref, wupr_ref, wout_ref,
                    ln1g_ref, ln1b_ref, o_ref, *, ctx_tiles):
    is_ctx = pl.program_id(0) < ctx_tiles
    pick = lambda a, b: jnp.where(is_ctx, a, b)
    y5 = u_ref[...] * d_ref[...] + pick(y5c_ref[0] + y5c_ref[1], y5s_ref[0] + y5s_ref[1])
    y5 = jax.nn.gelu(y5)
    y5 = y5 * _sigmoid(_dot(y5, wglu_ref[...]))
    ones = ones_ref[...]
    yr = pick(yrwc_ref[...], yrws_ref[...])
    inv_n = 1.0 / RWKV_HEAD
    mu = _dot_split(yr, ones) * inv_n
    yc = yr - mu
    var = _dot_split(yc * yc, ones) * inv_n
    yr = yc * lax.rsqrt(var + GN_EPS) * lnxw_ref[...] + lnxb_ref[...]
    yr = (yr + pick(bonc_ref[...], bons_ref[...])) * pick(gc_ref[...], gs2_ref[...])
    merged = (_sigmoid(gs_ref[...]) * _dot(y5, wups_ref[...])
              + _sigmoid(gr_ref[...]) * _dot(yr, wupr_ref[...]))
    mix = _dot(merged, wout_ref[...])
    x3 = _modulate(mix, jnp.zeros_like(pat_ref[0, 2]), pat_ref[0, 2] - 1.0)
    o_ref[...] = _layer_norm(ALPHA * x_ref[...] + x3, ln1g_ref[...], ln1b_ref[...])


def _mix_out(x, pat, u, gs, gr, y5, yrw, bonus, g, p, rows_per_trunk, tm=256):
    rows = x.shape[0]
    d = D_MODEL
    c = D_RWKV
    tpt = rows_per_trunk // tm
    rowblk = lambda n: pl.BlockSpec((tm, n), lambda i: (i, 0))
    ctx_i = lambda i: jnp.minimum(i, tpt - 1)
    smp_i = lambda i: jnp.maximum(i - tpt, 0)
    pair = lambda spec_of: [spec_of(ctx_i), spec_of(smp_i)]
    row_t = lambda f: pl.BlockSpec((tm, c), lambda i: (f(i), 0))
    dir_t = lambda f: pl.BlockSpec((2, tm, c), lambda i: (0, f(i), 0))
    vec = lambda n: pl.BlockSpec((1, n), lambda i: (0, 0))
    mat = lambda a, b: pl.BlockSpec((a, b), lambda i: (0, 0))
    head_ones = jnp.kron(jnp.eye(H_RWKV, dtype=F32), jnp.ones((RWKV_HEAD, RWKV_HEAD), F32)).astype(BF16)
    return pl.pallas_call(
        functools.partial(_mix_out_kernel, ctx_tiles=tpt),
        grid=(rows // tm,),
        in_specs=[rowblk(d),
                  pl.BlockSpec((1, 6, SUBLANES, d), lambda i: (i // tpt, 0, 0, 0)),
                  rowblk(c), rowblk(d), rowblk(d)]
                 + pair(dir_t) + pair(row_t) + pair(row_t) + pair(row_t)
                 + [vec(c), mat(c, c), mat(c, c), vec(c), vec(c), mat(c, d), mat(c, d), mat(d, d),
                    vec(d), vec(d)],
        out_specs=rowblk(d),
        out_shape=jax.ShapeDtypeStruct((rows, d), F32),
        compiler_params=_params(("arbitrary",)),
        name="mix_out",
    )(x, pat, u, gs, gr, *y5, *yrw, *bonus, *g,
      p['s5_d'].reshape(1, c), p['s5_w_glu'].astype(BF16), head_ones,
      p['rw_lnx_w'].reshape(1, c), p['rw_lnx_b'].reshape(1, c),
      p['w_up_s5'].astype(BF16), p['w_up_rwkv'].astype(BF16), p['w_out'].astype(BF16),
      p['ln1_g'].reshape(1, d), p['ln1_b'].reshape(1, d))


def _first_max(x, lane, valid):
    xm = jnp.where(valid, x, -jnp.inf)
    m = jnp.max(xm, -1, keepdims=True)
    idx = jnp.min(jnp.where(xm == m, lane, float(N_EXPERTS)), -1, keepdims=True)
    return m, idx


def _moe_kernel(x_ref, pat_ref, wr_ref, br_ref, wg_ref, wu_ref, wd_ref, ln2g_ref, ln2b_ref, o_ref,
                hb_ref, comb_ref, acc_ref):
    e = pl.program_id(1)
    ne = pl.num_programs(1)
    tm = x_ref.shape[0]

    @pl.when(e == 0)
    def _():
        h = _modulate(x_ref[...], pat_ref[0, 3], pat_ref[0, 4])
        hb_ref[...] = h.astype(BF16)
        wr = wr_ref[...]
        w_hi = wr.astype(BF16)
        w_lo = (wr - w_hi.astype(F32)).astype(BF16)
        h_hi = h.astype(BF16)
        h_lo = (h - h_hi.astype(F32)).astype(BF16)
        logits = (jnp.dot(h_hi, w_hi, preferred_element_type=F32)
                  + jnp.dot(h_hi, w_lo, preferred_element_type=F32)
                  + jnp.dot(h_lo, w_hi, preferred_element_type=F32)) + br_ref[...]
        logits = logits - jnp.max(logits, -1, keepdims=True)
        ex = jnp.exp(logits)
        probs = ex / jnp.sum(ex, -1, keepdims=True)
        lane_i = lax.broadcasted_iota(jnp.int32, (tm, N_EXPERTS), 1)
        lane = lane_i.astype(F32)
        grp = (lane_i // EXPERTS_PER_GROUP).astype(F32)
        best_score = jnp.full((tm, 1), -jnp.inf, F32)
        best_grp = jnp.zeros((tm, 1), F32)
        for gi in range(N_GROUPS):
            in_g = grp == float(gi)
            m1, i1 = _first_max(probs, lane, in_g)
            m2, _ = _first_max(probs, lane, in_g & (lane != i1))
            score = m1 + m2
            better = score > best_score
            best_score = jnp.where(better, score, best_score)
            best_grp = jnp.where(better, float(gi), best_grp)
        in_best = grp == best_grp
        m1, i1 = _first_max(probs, lane, in_best)
        m2, i2 = _first_max(probs, lane, in_best & (lane != i1))
        tot = m1 + m2
        comb_ref[...] = jnp.where(lane == i1, m1 / tot, 0.0) + jnp.where(lane == i2, m2 / tot, 0.0)
        acc_ref[...] = jnp.zeros_like(acc_ref)

    wg = wg_ref[0, 0].astype(BF16)
    wu = wu_ref[0, 0].astype(BF16)
    wd = wd_ref[0, 0].astype(BF16)
    sub = min(tm, MOE_SUB_ROWS)

    for rb in range(tm // sub):
        rs = pl.ds(rb * sub, sub)
        hb = hb_ref[rs, :]
        lane = lax.broadcasted_iota(jnp.int32, (sub, N_EXPERTS), 1)
        ce = jnp.sum(jnp.where(lane == e, comb_ref[rs, :], 0.0), -1, keepdims=True)
        hid = (_silu(jnp.dot(hb, wg, preferred_element_type=F32))
               * jnp.dot(hb, wu, preferred_element_type=F32))
        acc_ref[rs, :] += jnp.dot((hid * ce).astype(BF16), wd, preferred_element_type=F32)

    @pl.when(e == ne - 1)
    def _():
        ffn = _modulate(acc_ref[...], jnp.zeros_like(pat_ref[0, 5]), pat_ref[0, 5] - 1.0)
        o_ref[...] = _layer_norm(ALPHA * x_ref[...] + ffn, ln2g_ref[...], ln2b_ref[...])


MOE_SUB_ROWS = 256


def _moe(x, pat, w_router, b_router, wg, wu, wd, layer, ln2_g, ln2_b, rows_per_trunk, tm=1024):
    rows = x.shape[0]
    tm = min(tm, rows_per_trunk)
    d = D_MODEL
    tpt = rows_per_trunk // tm
    rowblk = pl.BlockSpec((tm, d), lambda i, e: (i, 0))
    vec = lambda n: pl.BlockSpec((1, n), lambda i, e: (0, 0))
    return pl.pallas_call(
        _moe_kernel,
        grid=(rows // tm, N_EXPERTS),
        in_specs=[rowblk,
                  pl.BlockSpec((1, 6, SUBLANES, d), lambda i, e: (i // tpt, 0, 0, 0)),
                  pl.BlockSpec((d, N_EXPERTS), lambda i, e: (0, 0)), vec(N_EXPERTS),
                  pl.BlockSpec((1, 1, d, D_EXPERT), lambda i, e: (layer, e, 0, 0)),
                  pl.BlockSpec((1, 1, d, D_EXPERT), lambda i, e: (layer, e, 0, 0)),
                  pl.BlockSpec((1, 1, D_EXPERT, d), lambda i, e: (layer, e, 0, 0)),
                  vec(d), vec(d)],
        out_specs=rowblk,
        out_shape=jax.ShapeDtypeStruct((rows, d), F32),
        scratch_shapes=[pltpu.VMEM((tm, d), BF16), pltpu.VMEM((tm, N_EXPERTS), F32),
                        pltpu.VMEM((tm, d), F32)],
        compiler_params=_params(("arbitrary", "arbitrary")),
        name="moe",
    )(x, pat, w_router, b_router.reshape(1, N_EXPERTS), wg, wu, wd,
      ln2_g.reshape(1, d), ln2_b.reshape(1, d))


def kernel(x_prompt, x_sample, state_s5, state_rwkv, c, c_ctx, w_ada, b_ada, w_in, s5_a_re, s5_a_im, s5_log_dt, s5_b_re, s5_b_im, s5_c_re, s5_c_im, s5_d, s5_w_glu, rw_mu, rw_w0, rw_w2, rw_a0, rw_a2, rw_g2, rw_k_k, rw_k_a, rw_r_k, rw_lnx_w, rw_lnx_b, w_up_s5, w_up_rwkv, w_out, ln1_g, ln1_b, ln2_g, ln2_b, w_router, b_router, w_exp_gate, w_exp_up, w_exp_down):
    nbc, tc_len, d = x_prompt.shape
    nbs, ts_len, _ = x_sample.shape
    nl = w_ada.shape[0]
    rc = nbc * tc_len
    rs = nbs * ts_len
    assert rc == rs, "both trunks are processed as equal halves of one row-major token matrix"
    assert SUBLANES % nbs == 0 and nbc % SUBLANES == 0

    x = _to_time_major(x_prompt, x_sample)

    cond = jnp.concatenate([c_ctx[None], c], axis=0)
    cond8 = jnp.zeros((SUBLANES, d), F32).at[:cond.shape[0]].set(cond)
    mod = _ada_mod(cond8, w_ada, b_ada).reshape(nl, SUBLANES, 6, d)
    ctx_rows = jnp.zeros((SUBLANES,), jnp.int32)
    smp_rows = 1 + jnp.arange(SUBLANES, dtype=jnp.int32) % nbs
    pat_idx = jnp.stack([ctx_rows, smp_rows])
    pats = mod[:, pat_idx]
    pats = pats.transpose(0, 1, 3, 2, 4)

    m = nl * 2
    ab_re, ab_im, wb_re, wb_im, wc = _s5_weights(
        s5_a_re.reshape(m, G_S5, N_S5), s5_a_im.reshape(m, G_S5, N_S5), s5_log_dt.reshape(m, G_S5, 1),
        s5_b_re.reshape(m, G_S5, N_S5, S5_GROUP).transpose(0, 1, 3, 2),
        s5_b_im.reshape(m, G_S5, N_S5, S5_GROUP).transpose(0, 1, 3, 2),
        s5_c_re.reshape(m, G_S5, S5_GROUP, N_S5).transpose(0, 1, 3, 2).reshape(m, S5_STATE, S5_GROUP),
        s5_c_im.reshape(m, G_S5, S5_GROUP, N_S5).transpose(0, 1, 3, 2).reshape(m, S5_STATE, S5_GROUP))
    ab_re = ab_re.reshape(nl, 2, 1, S5_STATE)
    ab_im = ab_im.reshape(nl, 2, 1, S5_STATE)
    wb_re = wb_re.reshape(nl, 2, D_S5, S5_STATE)
    wb_im = wb_im.reshape(nl, 2, D_S5, S5_STATE)
    wc = wc.reshape(nl, 2, 2 * S5_STATE, D_S5)

    w_in_bf16 = w_in.astype(BF16)
    zero_s5 = jnp.zeros((2, 2, nbc, S5_STATE), F32)
    zero_rw = jnp.zeros((2, RWKV_HEAD, RWKV_HEAD, nbc * H_RWKV), F32)
    s5_out, rw_out = [], []
    for l in range(nl):
        p = dict(s5_d=s5_d[l], s5_w_glu=s5_w_glu[l], rw_mu=rw_mu[l], rw_w0=rw_w0[l], rw_w2=rw_w2[l],
                 rw_a0=rw_a0[l], rw_a2=rw_a2[l], rw_g2=rw_g2[l], rw_k_k=rw_k_k[l], rw_k_a=rw_k_a[l],
                 rw_r_k=rw_r_k[l], rw_lnx_w=rw_lnx_w[l], rw_lnx_b=rw_lnx_b[l], w_up_s5=w_up_s5[l],
                 w_up_rwkv=w_up_rwkv[l], w_out=w_out[l], ln1_g=ln1_g[l], ln1_b=ln1_b[l])
        pat = pats[l]
        u, zr, gs, gr = _inproj(x, pat, w_in_bf16, l, rc)

        s5w = (wb_re[l], wb_im[l], ab_re[l], ab_im[l], wc[l])
        y5c, hfin = _s5_scan(u, *s5w, zero_s5, nbc, 0, rc)
        h0s = state_s5[:, l].reshape(nbs, 2, 2, S5_STATE).transpose(1, 2, 0, 3)
        y5s, _ = _s5_scan(u, *s5w, h0s, nbs, rc, rs)
        s5_out.append(hfin.transpose(2, 0, 1, 3).reshape(nbc, 2, 2, G_S5, N_S5))

        prep_c = _rwkv_prep(zr, p, nbc, tc_len, False, 0)
        prep_s = _rwkv_prep(zr, p, nbs, ts_len, True, rc)
        yc, sfin = _rwkv_scan(*_relayout_ctx(prep_c, nbc, tc_len), zero_rw, tc=min(32, tc_len), folded=False)
        ys, _ = _rwkv_scan(*_relayout_smp(prep_s, nbs, ts_len, 0), _rwkv_state_smp(state_rwkv[:, l], nbs),
                           tc=min(64, ts_len // 2), folded=True)
        yrw = (_unmix_ctx(yc, nbc, tc_len), _unmix_smp(ys, nbs, ts_len))
        rw_out.append(_rwkv_state_out_ctx(sfin, nbc))

        x = _mix_out(x, pat, u, gs, gr, (y5c, y5s), yrw, (prep_c[3], prep_s[3]), (prep_c[4], prep_s[4]), p, rc)
        x = _moe(x, pat, w_router, b_router, w_exp_gate, w_exp_up, w_exp_down, l, ln2_g[l], ln2_b[l], rc)

    y_prompt = _from_time_major(x, nbc, tc_len, 0)
    y_sample = _from_time_major(x, nbs, ts_len, rc)
    return (y_prompt, y_sample, jnp.stack(s5_out, 1), jnp.stack(rw_out, 1))
```

```python
import functools

import numpy as np
import jax
import jax.numpy as jnp
from jax import lax
from jax.experimental import pallas as pl
from jax.experimental.pallas import tpu as pltpu

D_MODEL = 1024
DEPTH = 2
GRID_W = 64
D_S5 = 512
S5_GROUP = 16
G_S5 = 32
N_S5 = 64
S5_STATE = G_S5 * N_S5
D_RWKV = 512
RWKV_HEAD = 64
H_RWKV = 8
LORA_W = 64
LORA_A = 64
LORA_G = 128
RWKV_COLS = 3 * D_RWKV + LORA_W + LORA_A + LORA_G
N_EXPERTS = 16
N_GROUPS = 4
EXPERTS_PER_GROUP = 4
D_EXPERT = 512
ALPHA = (2 * DEPTH) ** 0.25
LN_EPS = 1e-5
GN_EPS = 64e-5

SUBLANES = 8
LANES = 128
VMEM_LIMIT = 56 * 1024 * 1024

F32 = jnp.float32
BF16 = jnp.bfloat16


def _params(sem):
    return pltpu.CompilerParams(dimension_semantics=sem, vmem_limit_bytes=VMEM_LIMIT)


def _dot(a, b):
    return jnp.dot(a.astype(BF16), b.astype(BF16), preferred_element_type=F32)


def _dot_split(x, w_exact):
    hi = x.astype(BF16)
    lo = (x - hi.astype(F32)).astype(BF16)
    return (jnp.dot(hi, w_exact, preferred_element_type=F32)
            + jnp.dot(lo, w_exact, preferred_element_type=F32))


def _sigmoid(x):
    return 1.0 / (1.0 + jnp.exp(-x))


def _silu(x):
    return x * _sigmoid(x)


def _layer_norm(x, g, b):
    mu = jnp.mean(x, -1, keepdims=True)
    xc = x - mu
    var = jnp.mean(xc * xc, -1, keepdims=True)
    return xc * lax.rsqrt(var + LN_EPS) * g + b


def _modulate(x, shift8, scale8):
    rows, d = x.shape
    x3 = x.reshape(rows // SUBLANES, SUBLANES, d)
    return (x3 * (1.0 + scale8)[None] + shift8[None]).reshape(rows, d)


def _to_time_major_kernel(xc_ref, xs_ref, o_ref, *, ctx_tiles):
    i = pl.program_id(0)

    @pl.when(i < ctx_tiles)
    def _():
        nb, steps, _ = xc_ref.shape
        for t in range(steps):
            o_ref[pl.ds(t * nb, nb), :] = xc_ref[:, t, :]

    @pl.when(i >= ctx_tiles)
    def _():
        nb, steps, d = xs_ref.shape
        rows = nb * steps
        r = lax.broadcasted_iota(jnp.int32, (rows, rows), 0)
        c = lax.broadcasted_iota(jnp.int32, (rows, rows), 1)
        perm = ((r % nb) * steps + r // nb == c).astype(BF16)
        o_ref[...] = _permute_rows(perm, xs_ref[...].reshape(rows, d))


def _permute_rows(perm, x):
    hi = x.astype(BF16)
    r1 = x - hi.astype(F32)
    mid = r1.astype(BF16)
    lo = (r1 - mid.astype(F32)).astype(BF16)
    return (jnp.dot(perm, hi, preferred_element_type=F32) + jnp.dot(perm, mid, preferred_element_type=F32)
            + jnp.dot(perm, lo, preferred_element_type=F32))


def _to_time_major(x_ctx, x_smp, tm=256):
    nbc, tcl, d = x_ctx.shape
    nbs, tsl, _ = x_smp.shape
    ct = nbc * tcl // tm
    st = nbs * tsl // tm
    return pl.pallas_call(
        functools.partial(_to_time_major_kernel, ctx_tiles=ct),
        grid=(ct + st,),
        in_specs=[pl.BlockSpec((nbc, tm // nbc, d), lambda i: (0, jnp.minimum(i, ct - 1), 0)),
                  pl.BlockSpec((nbs, tm // nbs, d), lambda i: (0, jnp.maximum(i - ct, 0), 0))],
        out_specs=pl.BlockSpec((tm, d), lambda i: (i, 0)),
        out_shape=jax.ShapeDtypeStruct((nbc * tcl + nbs * tsl, d), F32),
        compiler_params=_params(("arbitrary",)),
        name="to_time_major",
    )(x_ctx, x_smp)


def _from_time_major_kernel(x_ref, o_ref):
    nb, steps, _ = o_ref.shape
    if nb >= SUBLANES:
        for t in range(steps):
            o_ref[:, t, :] = x_ref[pl.ds(t * nb, nb), :]
    else:
        rows = nb * steps
        r = lax.broadcasted_iota(jnp.int32, (rows, rows), 0)
        c = lax.broadcasted_iota(jnp.int32, (rows, rows), 1)
        perm = ((r % steps) * nb + r // steps == c).astype(BF16)
        o_ref[...] = _permute_rows(perm, x_ref[...]).reshape(o_ref.shape)


def _from_time_major(x, nb, seq, row0, tm=256):
    d = x.shape[1]
    t0 = row0 // tm
    return pl.pallas_call(
        _from_time_major_kernel,
        grid=(seq * nb // tm,),
        in_specs=[pl.BlockSpec((tm, d), lambda i: (t0 + i, 0))],
        out_specs=pl.BlockSpec((nb, tm // nb, d), lambda i: (0, i, 0)),
        out_shape=jax.ShapeDtypeStruct((nb, seq, d), F32),
        compiler_params=_params(("arbitrary",)),
        name="from_time_major",
    )(x)


def _ada_kernel(cond_ref, w_ref, b_ref, o_ref):
    c = cond_ref[...]
    o_ref[0] = _dot(_silu(c), w_ref[0]) + b_ref[0]


def _ada_mod(cond8, w_ada, b_ada):
    nl = w_ada.shape[0]
    d = D_MODEL
    return pl.pallas_call(
        _ada_kernel,
        grid=(nl, 6),
        in_specs=[pl.BlockSpec((SUBLANES, d), lambda l, k: (0, 0)),
                  pl.BlockSpec((1, d, d), lambda l, k: (l, 0, k)),
                  pl.BlockSpec((1, 1, d), lambda l, k: (l, 0, k))],
        out_specs=pl.BlockSpec((1, SUBLANES, d), lambda l, k: (l, 0, k)),
        out_shape=jax.ShapeDtypeStruct((nl, SUBLANES, 6 * d), F32),
        compiler_params=_params(("arbitrary", "arbitrary")),
        name="ada_mod",
    )(cond8, w_ada, b_ada.reshape(nl, 1, 6 * d))


IN_SPLITS = (0, D_S5, D_S5 + RWKV_COLS, D_S5 + RWKV_COLS + D_MODEL, D_S5 + RWKV_COLS + 2 * D_MODEL)


def _inproj_kernel(x_ref, pat_ref, w_ref, u_ref, zr_ref, gs_ref, gr_ref):
    h = _modulate(x_ref[...], pat_ref[0, 0], pat_ref[0, 1]).astype(BF16)
    for k, out in enumerate((u_ref, zr_ref, gs_ref, gr_ref)):
        out[...] = jnp.dot(h, w_ref[0, :, IN_SPLITS[k]:IN_SPLITS[k + 1]], preferred_element_type=F32)


def _inproj(x, pat, w_in_bf16, layer, rows_per_trunk, tm=256):
    rows = x.shape[0]
    d = D_MODEL
    tpt = rows_per_trunk // tm
    widths = [IN_SPLITS[k + 1] - IN_SPLITS[k] for k in range(4)]
    rowblk = lambda n: pl.BlockSpec((tm, n), lambda i: (i, 0))
    return pl.pallas_call(
        _inproj_kernel,
        grid=(rows // tm,),
        in_specs=[rowblk(d),
                  pl.BlockSpec((1, 6, SUBLANES, d), lambda i: (i // tpt, 0, 0, 0)),
                  pl.BlockSpec((1, d, IN_SPLITS[-1]), lambda i: (layer, 0, 0))],
        out_specs=[rowblk(n) for n in widths],
        out_shape=[jax.ShapeDtypeStruct((rows, n), F32) for n in widths],
        compiler_params=_params(("arbitrary",)),
        name="in_proj",
    )(x, pat, w_in_bf16)


def _s5_disc_kernel(are_ref, aim_ref, ldt_ref, bre_ref, bim_ref, cre_ref, cim_ref,
                    abre_ref, abim_ref, wbre_ref, wbim_ref, wc_ref):
    a_re = jnp.minimum(are_ref[0], -1e-4)
    a_im = aim_ref[0]
    dt = jnp.exp(ldt_ref[0])
    mag = jnp.exp(a_re * dt)
    ab_re = mag * jnp.cos(a_im * dt)
    ab_im = mag * jnp.sin(a_im * dt)
    den = a_re * a_re + a_im * a_im
    nr = ab_re - 1.0
    q_re = (nr * a_re + ab_im * a_im) / den
    q_im = (ab_im * a_re - nr * a_im) / den
    abre_ref[0] = ab_re
    abim_ref[0] = ab_im
    b_re = bre_ref[0]
    b_im = bim_ref[0]
    bb_re = (q_re[:, None, :] * b_re - q_im[:, None, :] * b_im).reshape(D_S5, N_S5)
    bb_im = (q_re[:, None, :] * b_im + q_im[:, None, :] * b_re).reshape(D_S5, N_S5)

    def spread(x, n_rep, row_group, col_group):
        rows, w = x.shape
        tiled_eye = (lax.broadcasted_iota(jnp.int32, (w, n_rep * w), 0)
                     == lax.broadcasted_iota(jnp.int32, (w, n_rep * w), 1) % w).astype(BF16)
        rep = jnp.dot(x.astype(BF16), tiled_eye, preferred_element_type=F32)
        keep = (lax.broadcasted_iota(jnp.int32, rep.shape, 0) // row_group
                == lax.broadcasted_iota(jnp.int32, rep.shape, 1) // col_group)
        return jnp.where(keep, rep, 0.0).astype(BF16)

    wbre_ref[0] = spread(bb_re, G_S5, S5_GROUP, N_S5)
    wbim_ref[0] = spread(bb_im, G_S5, S5_GROUP, N_S5)
    wc_ref[0, :S5_STATE] = spread(cre_ref[0], G_S5, N_S5, S5_GROUP)
    wc_ref[0, S5_STATE:] = spread(-cim_ref[0], G_S5, N_S5, S5_GROUP)


def _s5_weights(a_re, a_im, log_dt, b_re, b_im, c_re, c_im):
    m = a_re.shape[0]
    s = S5_STATE
    blk = lambda *shape: pl.BlockSpec((1,) + shape, lambda i: (i,) + (0,) * len(shape))
    return pl.pallas_call(
        _s5_disc_kernel,
        grid=(m,),
        in_specs=[blk(G_S5, N_S5), blk(G_S5, N_S5), blk(G_S5, 1), blk(G_S5, S5_GROUP, N_S5),
                  blk(G_S5, S5_GROUP, N_S5), blk(s, S5_GROUP), blk(s, S5_GROUP)],
        out_specs=[blk(G_S5, N_S5), blk(G_S5, N_S5), blk(D_S5, s), blk(D_S5, s), blk(2 * s, D_S5)],
        out_shape=[jax.ShapeDtypeStruct((m, G_S5, N_S5), F32), jax.ShapeDtypeStruct((m, G_S5, N_S5), F32),
                   jax.ShapeDtypeStruct((m, D_S5, s), BF16), jax.ShapeDtypeStruct((m, D_S5, s), BF16),
                   jax.ShapeDtypeStruct((m, 2 * s, D_S5), BF16)],
        compiler_params=_params(("arbitrary",)),
        name="s5_weights",
    )(a_re, a_im, log_dt, b_re, b_im, c_re, c_im)


def _s5_scan_kernel(u_ref, wbre_ref, wbim_ref, abre_ref, abim_ref, wc_ref, h0_ref,
                    y_ref, hfin_ref, bure_ref, buim_ref, hre_ref, him_ref, *, nb, lane_w):
    d = pl.program_id(0)
    c = pl.program_id(1)
    nc = pl.num_programs(1)
    rows = u_ref.shape[0]

    @pl.when(c == 0)
    def _():
        hre_ref[...] = h0_ref[0, 0]
        him_ref[...] = h0_ref[0, 1]

    ub = u_ref[...].astype(BF16)
    ct = 2 * LANES
    for j in range(S5_STATE // ct):
        first_channel = j * ct * S5_GROUP // N_S5
        ks = slice(first_channel // ct * ct, first_channel // ct * ct + ct)
        js = slice(ct * j, ct * (j + 1))
        bure_ref[:, js] = jnp.dot(ub[:, ks], wbre_ref[0, ks, js], preferred_element_type=F32)
        buim_ref[:, js] = jnp.dot(ub[:, ks], wbim_ref[0, ks, js], preferred_element_type=F32)

    fwd = d == 0
    if nb >= SUBLANES:
        steps = rows // nb
        for lc in range(S5_STATE // lane_w):
            ls = slice(lc * lane_w, (lc + 1) * lane_w)
            ar = jnp.broadcast_to(abre_ref[0, :, ls], (nb, lane_w))
            ai = jnp.broadcast_to(abim_ref[0, :, ls], (nb, lane_w))

            def body(s, carry, ls=ls, ar=ar, ai=ai):
                hr, hi = carry
                t = jnp.where(fwd, s, steps - 1 - s)
                r0 = pl.multiple_of(t * nb, nb)
                br = bure_ref[pl.ds(r0, nb), ls]
                bi = buim_ref[pl.ds(r0, nb), ls]
                nr = ar * hr - ai * hi + br
                ni = ar * hi + ai * hr + bi
                bure_ref[pl.ds(r0, nb), ls] = nr
                buim_ref[pl.ds(r0, nb), ls] = ni
                return nr, ni

            hr, hi = lax.fori_loop(0, steps, body, (hre_ref[:, ls], him_ref[:, ls]))
            hre_ref[:, ls] = hr
            him_ref[:, ls] = hi
    else:
        per = SUBLANES // nb
        groups = rows // SUBLANES
        row_id = lax.broadcasted_iota(jnp.int32, (SUBLANES, lane_w), 0) // nb
        shift = jnp.where(fwd, nb, SUBLANES - nb)
        for lc in range(S5_STATE // lane_w):
            ls = slice(lc * lane_w, (lc + 1) * lane_w)
            ar = jnp.broadcast_to(abre_ref[0, :, ls], (SUBLANES, lane_w))
            ai = jnp.broadcast_to(abim_ref[0, :, ls], (SUBLANES, lane_w))
            h0r = hre_ref[:, ls]
            h0i = him_ref[:, ls]

            def body(gidx, carry, ls=ls, ar=ar, ai=ai):
                tr, ti = carry
                g = jnp.where(fwd, gidx, groups - 1 - gidx)
                r0 = pl.multiple_of(g * SUBLANES, SUBLANES)
                br = bure_ref[pl.ds(r0, SUBLANES), ls]
                bi = buim_ref[pl.ds(r0, SUBLANES), ls]
                for k in range(per):
                    pr = pltpu.roll(tr, shift, 0)
                    pi = pltpu.roll(ti, shift, 0)
                    nr = ar * pr - ai * pi + br
                    ni = ar * pi + ai * pr + bi
                    tgt = jnp.where(fwd, k, per - 1 - k)
                    sel = row_id == tgt
                    tr = jnp.where(sel, nr, tr)
                    ti = jnp.where(sel, ni, ti)
                bure_ref[pl.ds(r0, SUBLANES), ls] = tr
                buim_ref[pl.ds(r0, SUBLANES), ls] = ti
                return tr, ti

            tr, ti = lax.fori_loop(0, groups, body, (h0r, h0i))
            hre_ref[:, ls] = tr
            him_ref[:, ls] = ti

    for q in range(D_S5 // ct):
        qs = slice(ct * q, ct * (q + 1))
        ss = slice(q * ct * N_S5 // S5_GROUP, (q + 1) * ct * N_S5 // S5_GROUP)
        si = slice(S5_STATE + ss.start, S5_STATE + ss.stop)
        y_ref[0, :, qs] = (jnp.dot(bure_ref[:, ss].astype(BF16), wc_ref[0, ss, qs], preferred_element_type=F32)
                           + jnp.dot(buim_ref[:, ss].astype(BF16), wc_ref[0, si, qs],
                                     preferred_element_type=F32))

    @pl.when(c == nc - 1)
    def _():
        hfin_ref[0, 0] = hre_ref[...]
        hfin_ref[0, 1] = him_ref[...]


def _s5_scan(u, wb_re, wb_im, ab_re, ab_im, wc, h0, nb, row0, rows, chunk_rows=512):
    if nb < SUBLANES:
        h0 = jnp.tile(h0, (1, 1, SUBLANES // nb, 1))
    srows = max(nb, SUBLANES)
    cr = min(chunk_rows, rows)
    nc = rows // cr
    c0 = row0 // cr
    s = S5_STATE
    lane_w = 256 if nb >= SUBLANES else 512
    chunk = lambda d, c: c + d * (nc - 1 - 2 * c)
    y, hfin = pl.pallas_call(
        functools.partial(_s5_scan_kernel, nb=nb, lane_w=lane_w),
        grid=(2, nc),
        in_specs=[pl.BlockSpec((cr, D_S5), lambda d, c: (c0 + chunk(d, c), 0)),
                  pl.BlockSpec((1, D_S5, s), lambda d, c: (d, 0, 0)),
                  pl.BlockSpec((1, D_S5, s), lambda d, c: (d, 0, 0)),
                  pl.BlockSpec((1, 1, s), lambda d, c: (d, 0, 0)),
                  pl.BlockSpec((1, 1, s), lambda d, c: (d, 0, 0)),
                  pl.BlockSpec((1, 2 * s, D_S5), lambda d, c: (d, 0, 0)),
                  pl.BlockSpec((1, 2, srows, s), lambda d, c: (d, 0, 0, 0))],
        out_specs=[pl.BlockSpec((1, cr, D_S5), lambda d, c: (d, chunk(d, c), 0)),
                   pl.BlockSpec((1, 2, srows, s), lambda d, c: (d, 0, 0, 0))],
        out_shape=[jax.ShapeDtypeStruct((2, rows, D_S5), F32),
                   jax.ShapeDtypeStruct((2, 2, srows, s), F32)],
        scratch_shapes=[pltpu.VMEM((cr, s), F32), pltpu.VMEM((cr, s), F32),
                        pltpu.VMEM((srows, s), F32), pltpu.VMEM((srows, s), F32)],
        compiler_params=_params(("arbitrary", "arbitrary")),
        name="s5_scan",
    )(u, wb_re, wb_im, ab_re, ab_im, wc, h0)
    if nb < SUBLANES:
        hfin = jnp.stack([hfin[0, :, srows - nb:], hfin[1, :, :nb]])
    return y, hfin


def _rwkv_prep_kernel(prev_ref, cur_ref, next_ref, mu_ref, ones_ref, g2_ref, kk_ref, ka_ref, rk_ref,
                      w0_ref, w2_ref, a0_ref, a2_ref,
                      r_ref, v_ref, nkk_ref, bonus_ref, g_ref, w_ref, kd_ref, b_ref, *, nb, seq, grid_shift):
    tm = cur_ref.shape[0]
    i = pl.program_id(0)
    cur = cur_ref[...]
    prv = prev_ref[...]
    nxt = next_ref[...]
    row = lax.broadcasted_iota(jnp.int32, (tm, RWKV_COLS), 0)
    lane = lax.broadcasted_iota(jnp.int32, (tm, RWKV_COLS), 1)
    t = (i * tm + row) // nb

    def rows_before(s):
        if s == tm:
            return prv
        return jnp.where(row < s, pltpu.roll(prv, s, 0), pltpu.roll(cur, s, 0))

    def rows_after(s):
        if s == tm:
            return nxt
        return jnp.where(row >= tm - s, pltpu.roll(nxt, tm - s, 0), pltpu.roll(cur, tm - s, 0))

    if grid_shift:
        tw = t % GRID_W
        left = jnp.where(tw == 0, 0.0, rows_before(nb))
        right = jnp.where(tw == GRID_W - 1, 0.0, rows_after(nb))
        up = jnp.where(t < GRID_W, 0.0, rows_before(nb * GRID_W))
        down = jnp.where(t >= seq - GRID_W, 0.0, rows_after(nb * GRID_W))
        m4 = lane % 4
        sh = jnp.where(m4 == 0, left, jnp.where(m4 == 1, right, jnp.where(m4 == 2, up, down)))
    else:
        before = jnp.where(t == 0, 0.0, rows_before(nb))
        after = jnp.where(t == seq - 1, 0.0, rows_after(nb))
        sh = jnp.where(lane % 2 == 0, before, after)

    z = cur + (sh - cur) * mu_ref[...]
    r = z[:, 0:D_RWKV]
    k = z[:, D_RWKV:2 * D_RWKV]
    v = z[:, 2 * D_RWKV:3 * D_RWKV]
    xw = z[:, 3 * D_RWKV:3 * D_RWKV + LORA_W]
    xa = z[:, 3 * D_RWKV + LORA_W:3 * D_RWKV + LORA_W + LORA_A]
    xg = z[:, 3 * D_RWKV + LORA_W + LORA_A:]
    ones = ones_ref[...]

    g_ref[...] = _dot(_sigmoid(xg), g2_ref[...])
    kk = k * kk_ref[...]
    nrm = jnp.sqrt(_dot_split(kk * kk, ones))
    kk = kk / jnp.maximum(nrm, 1e-12)
    r_ref[...] = r
    v_ref[...] = v
    nkk_ref[...] = -kk
    bonus_ref[...] = _dot_split(r * k * rk_ref[...], ones) * v
    txw = jnp.tanh(xw)
    for d in range(2):
        zw = -(w0_ref[d] + _dot(txw, w2_ref[d]))
        softplus = jnp.maximum(zw, 0.0) + jnp.log(1.0 + jnp.exp(-jnp.abs(zw)))
        w_log = -softplus - 0.5
        w_ref[d] = jnp.exp(-jnp.exp(w_log))
        a = _sigmoid(a0_ref[d] + _dot(xa, a2_ref[d]))
        kd_ref[d] = k * (1.0 + (a - 1.0) * ka_ref[...])
        b_ref[d] = kk * a


def _rwkv_prep(zr, p, nb, seq, grid_shift, row0, tm=128):
    rows = seq * nb
    nt = rows // tm
    t0 = row0 // tm
    if grid_shift:
        assert tm == nb * GRID_W, "one tile must be one grid row of the latent grid"
    else:
        assert tm % nb == 0 and tm >= nb
    kern = functools.partial(_rwkv_prep_kernel, nb=nb, seq=seq, grid_shift=grid_shift)
    c = D_RWKV
    vec = lambda n: pl.BlockSpec((1, n), lambda i: (0, 0))
    mat = lambda a, b: pl.BlockSpec((a, b), lambda i: (0, 0))
    row_out = pl.BlockSpec((tm, c), lambda i: (i, 0))
    dir_out = pl.BlockSpec((2, tm, c), lambda i: (0, i, 0))
    head_ones = jnp.kron(jnp.eye(H_RWKV, dtype=F32), jnp.ones((RWKV_HEAD, RWKV_HEAD), F32)).astype(BF16)
    in_specs = [pl.BlockSpec((tm, RWKV_COLS), lambda i: (t0 + jnp.maximum(i - 1, 0), 0)),
                pl.BlockSpec((tm, RWKV_COLS), lambda i: (t0 + i, 0)),
                pl.BlockSpec((tm, RWKV_COLS), lambda i: (t0 + jnp.minimum(i + 1, nt - 1), 0)),
                vec(RWKV_COLS), mat(c, c), mat(LORA_G, c), vec(c), vec(c), vec(c),
                pl.BlockSpec((2, 1, c), lambda i: (0, 0, 0)),
                pl.BlockSpec((2, LORA_W, c), lambda i: (0, 0, 0)),
                pl.BlockSpec((2, 1, c), lambda i: (0, 0, 0)),
                pl.BlockSpec((2, LORA_A, c), lambda i: (0, 0, 0))]
    args = [zr, zr, zr, p['rw_mu'].reshape(1, -1), head_ones, p['rw_g2'], p['rw_k_k'].reshape(1, c),
            p['rw_k_a'].reshape(1, c), p['rw_r_k'].reshape(1, c), p['rw_w0'].reshape(2, 1, c), p['rw_w2'],
            p['rw_a0'].reshape(2, 1, c), p['rw_a2']]
    part = jax.ShapeDtypeStruct((rows, c), F32)
    return pl.pallas_call(
        kern,
        grid=(nt,),
        in_specs=in_specs,
        out_specs=[row_out] * 5 + [dir_out] * 3,
        out_shape=[part] * 5 + [jax.ShapeDtypeStruct((2, rows, c), F32)] * 3,
        compiler_params=_params(("arbitrary",)),
        name="rwkv_prep",
    )(*args)


def _rwkv_scan_kernel(a_ref, w_ref, b_ref, k_ref, r_ref, v_ref, s0_ref, y_ref, sfin_ref, s_ref, *,
                      nacc, row_blk, folded):
    d = pl.program_id(0)
    c = pl.program_id(1)
    nc = pl.num_programs(1)
    tc = a_ref.shape[1]
    nj = RWKV_HEAD
    j_unroll = 16 if row_blk >= 32 else nj
    rev = (c >= nc // 2) if folded else (d == 1)

    @pl.when(c == 0)
    def _():
        s_ref[...] = s0_ref[0]

    def step(s, carry):
        t = jnp.where(rev, tc - 1 - s, s)
        zero = jnp.zeros((row_blk, a_ref.shape[3]), F32)
        for rb in range(s_ref.shape[1] // row_blk):
            rows = pl.ds(rb * row_blk, row_blk)

            def pass1(jo, acc):
                acc = list(acc)
                for ji in range(j_unroll):
                    j = jo * j_unroll + ji
                    acc[ji % nacc] = acc[ji % nacc] + s_ref[j, rows, :] * a_ref[0, t, pl.ds(j, 1), :]
                return tuple(acc)

            sa = functools.reduce(lambda x, y: x + y, lax.fori_loop(0, nj // j_unroll, pass1, (zero,) * nacc))
            vv = v_ref[0, t, rows, :]

            def pass2(jo, acc):
                acc = list(acc)
                for ji in range(j_unroll):
                    j = jo * j_unroll + ji
                    s_new = (s_ref[j, rows, :] * w_ref[0, t, pl.ds(j, 1), :]
                             + sa * b_ref[0, t, pl.ds(j, 1), :]
                             + vv * k_ref[0, t, pl.ds(j, 1), :])
                    s_ref[j, rows, :] = s_new
                    acc[ji % nacc] = acc[ji % nacc] + s_new * r_ref[0, t, pl.ds(j, 1), :]
                return tuple(acc)

            y_ref[0, t, rows, :] = functools.reduce(
                lambda x, y: x + y, lax.fori_loop(0, nj // j_unroll, pass2, (zero,) * nacc))
        return carry

    lax.fori_loop(0, tc, step, 0)

    @pl.when(c == nc - 1)
    def _():
        sfin_ref[0] = s_ref[...]


def _rwkv_scan(a, w, b, k, r, v, s0, tc, folded):
    _, tp, _, nl = w.shape
    iv = v.shape[2]
    row_blk = min(iv, 32)
    nacc = 4 if row_blk <= 16 else 2
    nc_half = tp // tc
    if folded:
        nd, nc = 1, 2 * nc_half
        tmap = lambda d, c: (c // nc_half, jnp.where(c < nc_half, c, nc - 1 - c), 0, 0)
        shared = per_dir = tmap
    else:
        nd, nc = 2, nc_half
        chunk = lambda d, c: c + d * (nc - 1 - 2 * c)
        shared = lambda d, c: (0, chunk(d, c), 0, 0)
        per_dir = lambda d, c: (d, chunk(d, c), 0, 0)
    key = lambda m: pl.BlockSpec((1, tc, RWKV_HEAD, nl), m)
    val = lambda m: pl.BlockSpec((1, tc, iv, nl), m)
    sspec = pl.BlockSpec((1, RWKV_HEAD, iv, nl), lambda d, c: (d, 0, 0, 0))
    return pl.pallas_call(
        functools.partial(_rwkv_scan_kernel, nacc=nacc, row_blk=row_blk, folded=folded),
        grid=(nd, nc),
        in_specs=[key(shared), key(per_dir), key(per_dir), key(per_dir), key(shared), val(shared), sspec],
        out_specs=[val(per_dir), sspec],
        out_shape=[jax.ShapeDtypeStruct((2, tp, iv, nl), F32),
                   jax.ShapeDtypeStruct((nd, RWKV_HEAD, iv, nl), F32)],
        scratch_shapes=[pltpu.VMEM((RWKV_HEAD, iv, nl), F32)],
        compiler_params=_params(("arbitrary", "arbitrary")),
        name="rwkv_scan",
    )(a, w, b, k, r, v, s0)


SMP_REP = 4
SMP_ROWS = RWKV_HEAD // SMP_REP


def _block_transpose(tiles, bs):
    tiles = list(tiles)
    lane = lax.broadcasted_iota(jnp.int32, tiles[0].shape, 1)
    for kbit in range(3):
        sft = bs << kbit
        bit = (lane // sft) % 2 == 1
        for r in range(8):
            if (r >> kbit) & 1:
                continue
            r2 = r | (1 << kbit)
            lo, hi = tiles[r], tiles[r2]
            tiles[r] = jnp.where(bit, pltpu.roll(hi, sft, 1), lo)
            tiles[r2] = jnp.where(bit, hi, pltpu.roll(lo, LANES - sft, 1))
    return tiles


def _head_tiles(x):
    xt = x.T
    return [xt[h * RWKV_HEAD:(h + 1) * RWKV_HEAD, :] for h in range(H_RWKV)]


def _from_head_tiles(tiles):
    return jnp.concatenate(tiles, axis=0).T


def _relayout_ctx_kernel(r_ref, v_ref, nkk_ref, w_ref, kd_ref, b_ref, a_o, r_o, v_o, w_o, b_o, k_o, *, nb):
    steps = a_o.shape[1]
    for src, dst in ((nkk_ref, a_o), (r_ref, r_o), (v_ref, v_o)):
        tiles = _block_transpose(_head_tiles(src[...]), nb)
        for t in range(steps):
            dst[0, t] = tiles[t]
    for src, dst in ((w_ref, w_o), (b_ref, b_o), (kd_ref, k_o)):
        for d in range(2):
            tiles = _block_transpose(_head_tiles(src[d]), nb)
            for t in range(steps):
                dst[d, t] = tiles[t]


def _relayout_ctx(prep, nb, seq):
    r, v, nkk, _, _, w, kd, bb = prep
    steps = LANES // nb
    assert steps == H_RWKV and nb * H_RWKV == LANES
    tm = steps * nb
    nt = seq // steps
    c = D_RWKV
    nat = pl.BlockSpec((tm, c), lambda i: (i, 0))
    nat2 = pl.BlockSpec((2, tm, c), lambda i: (0, i, 0))
    out1 = pl.BlockSpec((1, steps, RWKV_HEAD, LANES), lambda i: (0, i, 0, 0))
    out2 = pl.BlockSpec((2, steps, RWKV_HEAD, LANES), lambda i: (0, i, 0, 0))
    s1 = jax.ShapeDtypeStruct((1, seq, RWKV_HEAD, LANES), F32)
    s2 = jax.ShapeDtypeStruct((2, seq, RWKV_HEAD, LANES), F32)
    a, r_, v_, w_, b_, k_ = pl.pallas_call(
        functools.partial(_relayout_ctx_kernel, nb=nb),
        grid=(nt,),
        in_specs=[nat, nat, nat, nat2, nat2, nat2],
        out_specs=[out1, out1, out1, out2, out2, out2],
        out_shape=[s1, s1, s1, s2, s2, s2],
        compiler_params=_params(("arbitrary",)),
        name="rwkv_relayout_ctx",
    )(r, v, nkk, w, kd, bb)
    return a, w_, b_, k_, r_, v_


def _unmix_ctx_kernel(y_ref, o_ref, *, nb):
    steps = y_ref.shape[1]
    tiles = [y_ref[0, t] + y_ref[1, t] for t in range(steps)]
    o_ref[...] = _from_head_tiles(_block_transpose(tiles, nb))


def _unmix_ctx(y, nb, seq):
    steps = LANES // nb
    tm = steps * nb
    return pl.pallas_call(
        functools.partial(_unmix_ctx_kernel, nb=nb),
        grid=(seq // steps,),
        in_specs=[pl.BlockSpec((2, steps, RWKV_HEAD, LANES), lambda i: (0, i, 0, 0))],
        out_specs=pl.BlockSpec((tm, D_RWKV), lambda i: (i, 0)),
        out_shape=jax.ShapeDtypeStruct((seq * nb, D_RWKV), F32),
        compiler_params=_params(("arbitrary",)),
        name="rwkv_unmix_ctx",
    )(y)


def _relayout_smp_kernel(ra, rb, va, vb, na, nb_, wa, wb, ka, kb, ba, bb, sel_ref,
                         a_o, r_o, v_o, w_o, b_o, k_o):
    steps = a_o.shape[1]
    lane_v = lax.broadcasted_iota(jnp.int32, (SMP_ROWS, LANES), 1)

    def conv(xa, xb):
        tiles = _block_transpose(_head_tiles(jnp.concatenate([xa, xb], axis=0)), 2)
        parts = []
        for x in tiles:
            hi = x.astype(BF16)
            r1 = x - hi.astype(F32)
            mid = r1.astype(BF16)
            lo = (r1 - mid.astype(F32)).astype(BF16)
            parts.append((hi, mid, lo))
        return parts

    def expand(o0, o1, half):
        order0 = range(8) if half == 0 else range(7, -1, -1)
        order1 = range(7, -1, -1) if half == 0 else range(8)
        cols = []
        for term in range(3):
            cols.append(jnp.concatenate([o0[s][term] for s in order0], axis=0))
            cols.append(jnp.concatenate([o1[s][term] for s in order1], axis=0))
        lhs = jnp.concatenate(cols, axis=1)
        out = [None] * steps
        for pair in range(steps // 16):
            res = jnp.dot(lhs, sel_ref[half, pair], preferred_element_type=F32)
            for q in range(2):
                for s_lo in range(8):
                    out[8 * (2 * pair + q) + s_lo] = res[64 * s_lo:64 * (s_lo + 1), LANES * q:LANES * (q + 1)]
        return out

    def value_rows(full):
        blk = lane_v // (LANES // SMP_REP)
        rows = [full[k * SMP_ROWS:(k + 1) * SMP_ROWS] for k in range(SMP_REP)]
        return jnp.where(blk == 0, rows[0], jnp.where(blk == 1, rows[1], jnp.where(blk == 2, rows[2], rows[3])))

    for srca, srcb, dst in ((na, nb_, a_o), (ra, rb, r_o), (va, vb, v_o)):
        o = conv(srca[...], srcb[...])
        for half in range(2):
            tiles = expand(o, o, half)
            for s in range(steps):
                dst[half, s] = value_rows(tiles[s]) if dst is v_o else tiles[s]
    for srca, srcb, dst in ((wa, wb, w_o), (ba, bb, b_o), (ka, kb, k_o)):
        o0 = conv(srca[0], srcb[0])
        o1 = conv(srca[1], srcb[1])
        for half in range(2):
            tiles = expand(o0, o1, half)
            for s in range(steps):
                dst[half, s] = tiles[s]


def _smp_selection():
    sel = np.zeros((2, 2, 6, LANES, 2 * LANES), np.float32)
    for half in range(2):
        for pair in range(2):
            for q in range(2):
                s_hi = 2 * pair + q
                grp_a = 0 * 64 + 16 * s_hi
                grp_b = 1 * 64 + 16 * (3 - s_hi)
                src = (grp_a, grp_b) if half == 0 else (grp_b, grp_a)
                for d in range(2):
                    for blk in range(SMP_REP):
                        for hb in range(16):
                            col = q * LANES + blk * 32 + d * 16 + hb
                            for term in range(3):
                                sel[half, pair, 2 * term + d, src[d] + hb, col] = 1.0
    return jnp.asarray(sel.reshape(2, 2, 6 * LANES, 2 * LANES), BF16)


def _relayout_smp(prep, nb, seq, row0):
    r, v, nkk, _, _, w, kd, bb = prep
    assert 2 * nb * H_RWKV * SMP_REP == LANES
    steps = 32
    tm = steps * nb
    nt = seq // steps
    half_t = seq // 2
    c = D_RWKV
    t0 = row0 // tm
    blk_a = pl.BlockSpec((tm, c), lambda g: (t0 + g, 0))
    blk_b = pl.BlockSpec((tm, c), lambda g: (t0 + nt - 1 - g, 0))
    blk2_a = pl.BlockSpec((2, tm, c), lambda g: (0, t0 + g, 0))
    blk2_b = pl.BlockSpec((2, tm, c), lambda g: (0, t0 + nt - 1 - g, 0))
    okey = pl.BlockSpec((2, steps, RWKV_HEAD, LANES), lambda g: (0, g, 0, 0))
    oval = pl.BlockSpec((2, steps, SMP_ROWS, LANES), lambda g: (0, g, 0, 0))
    skey = jax.ShapeDtypeStruct((2, half_t, RWKV_HEAD, LANES), F32)
    sval = jax.ShapeDtypeStruct((2, half_t, SMP_ROWS, LANES), F32)
    a, r_, v_, w_, b_, k_ = pl.pallas_call(
        _relayout_smp_kernel,
        grid=(nt // 2,),
        in_specs=[blk_a, blk_b, blk_a, blk_b, blk_a, blk_b, blk2_a, blk2_b, blk2_a, blk2_b, blk2_a, blk2_b,
                  pl.BlockSpec((2, 2, 6 * LANES, 2 * LANES), lambda g: (0, 0, 0, 0))],
        out_specs=[okey, okey, oval, okey, okey, okey],
        out_shape=[skey, skey, sval, skey, skey, skey],
        compiler_params=_params(("arbitrary",)),
        name="rwkv_relayout_smp",
    )(r, r, v, v, nkk, nkk, w, w, kd, kd, bb, bb, _smp_selection())
    return a, w_, b_, k_, r_, v_


def _unmix_smp_kernel(y_ref, o_ref, *, half_tiles):
    q = pl.program_id(0)
    steps = y_ref.shape[1]
    low = q < half_tiles
    h0 = jnp.where(low, 0, 1)
    lane = lax.broadcasted_iota(jnp.int32, (SMP_ROWS, LANES), 1)
    tiles = []
    for s_lo in range(8):
        blocks = []
        for blk in range(SMP_REP):
            acc = jnp.zeros((SMP_ROWS, LANES), F32)
            for s_hi in range(steps // 8):
                s = 8 * s_hi + s_lo
                u = jnp.where(low, s, steps - 1 - s)
                d0 = y_ref[h0, u]
                d1 = y_ref[1 - h0, u]
                sh0 = (16 * s_hi - 32 * blk) % LANES
                sh1 = (16 * s_hi - 32 * blk - 16) % LANES
                t0 = d0 if sh0 == 0 else pltpu.roll(d0, sh0, 1)
                t1 = d1 if sh1 == 0 else pltpu.roll(d1, sh1, 1)
                acc = jnp.where(lane // 16 == s_hi, t0 + t1, acc)
            blocks.append(acc)
        tiles.append(jnp.concatenate(blocks, axis=0))
    o_ref[...] = _from_head_tiles(_block_transpose(tiles, 2))


def _unmix_smp(y, nb, seq):
    steps = 64
    tm = steps * nb
    assert tm == LANES
    nq = seq // steps
    half_tiles = nq // 2
    return pl.pallas_call(
        functools.partial(_unmix_smp_kernel, half_tiles=half_tiles),
        grid=(nq,),
        in_specs=[pl.BlockSpec((2, steps, SMP_ROWS, LANES),
                               lambda q: (0, jnp.where(q < half_tiles, q, nq - 1 - q), 0, 0))],
        out_specs=pl.BlockSpec((tm, D_RWKV), lambda q: (q, 0)),
        out_shape=jax.ShapeDtypeStruct((seq * nb, D_RWKV), F32),
        compiler_params=_params(("arbitrary",)),
        name="rwkv_unmix_smp",
    )(y)


def _rwkv_state_smp(state_l, nb):
    s = state_l.reshape(nb, 2, H_RWKV, SMP_REP, SMP_ROWS, RWKV_HEAD)
    s = s.transpose(5, 4, 3, 1, 2, 0)
    return s.reshape(1, RWKV_HEAD, SMP_ROWS, LANES)


def _rwkv_state_out_ctx(sfin, nb):
    s = sfin.reshape(2, RWKV_HEAD, RWKV_HEAD, H_RWKV, nb)
    return s.transpose(4, 0, 3, 2, 1)


def _mix_out_kernel(x_ref, pat_ref, u_ref, gs_ref, gr_ref,
                    y5c_ref, y5s_ref, yrwc_ref, yrws_ref, bonc_ref, bons_ref, gc_ref, gs2_ref,
                    d_ref, wglu_ref, ones_ref, lnxw_ref, lnxb_ref, wups_ref, wupr_ref, wout_ref,
                    ln1g_ref, ln1b_ref, o_ref, *, ctx_tiles):
    is_ctx = pl.program_id(0) < ctx_tiles
    pick = lambda a, b: jnp.where(is_ctx, a, b)
    y5 = u_ref[...] * d_ref[...] + pick(y5c_ref[0] + y5c_ref[1], y5s_ref[0] + y5s_ref[1])
    y5 = jax.nn.gelu(y5)
    y5 = y5 * _sigmoid(_dot(y5, wglu_ref[...]))
    ones = ones_ref[...]
    yr = pick(yrwc_ref[...], yrws_ref[...])
    inv_n = 1.0 / RWKV_HEAD
    mu = _dot_split(yr, ones) * inv_n
    yc = yr - mu
    var = _dot_split(yc * yc, ones) * inv_n
    yr = yc * lax.rsqrt(var + GN_EPS) * lnxw_ref[...] + lnxb_ref[...]
    yr = (yr + pick(bonc_ref[...], bons_ref[...])) * pick(gc_ref[...], gs2_ref[...])
    merged = (_sigmoid(gs_ref[...]) * _dot(y5, wups_ref[...])
              + _sigmoid(gr_ref[...]) * _dot(yr, wupr_ref[...]))
    mix = _dot(merged, wout_ref[...])
    x3 = _modulate(mix, jnp.zeros_like(pat_ref[0, 2]), pat_ref[0, 2] - 1.0)
    o_ref[...] = _layer_norm(ALPHA * x_ref[...] + x3, ln1g_ref[...], ln1b_ref[...])


def _mix_out(x, pat, u, gs, gr, y5, yrw, bonus, g, p, rows_per_trunk, tm=256):
    rows = x.shape[0]
    d = D_MODEL
    c = D_RWKV
    tpt = rows_per_trunk // tm
    rowblk = lambda n: pl.BlockSpec((tm, n), lambda i: (i, 0))
    ctx_i = lambda i: jnp.minimum(i, tpt - 1)
    smp_i = lambda i: jnp.maximum(i - tpt, 0)
    pair = lambda spec_of: [spec_of(ctx_i), spec_of(smp_i)]
    row_t = lambda f: pl.BlockSpec((tm, c), lambda i: (f(i), 0))
    dir_t = lambda f: pl.BlockSpec((2, tm, c), lambda i: (0, f(i), 0))
    vec = lambda n: pl.BlockSpec((1, n), lambda i: (0, 0))
    mat = lambda a, b: pl.BlockSpec((a, b), lambda i: (0, 0))
    head_ones = jnp.kron(jnp.eye(H_RWKV, dtype=F32), jnp.ones((RWKV_HEAD, RWKV_HEAD), F32)).astype(BF16)
    return pl.pallas_call(
        functools.partial(_mix_out_kernel, ctx_tiles=tpt),
        grid=(rows // tm,),
        in_specs=[rowblk(d),
                  pl.BlockSpec((1, 6, SUBLANES, d), lambda i: (i // tpt, 0, 0, 0)),
                  rowblk(c), rowblk(d), rowblk(d)]
                 + pair(dir_t) + pair(row_t) + pair(row_t) + pair(row_t)
                 + [vec(c), mat(c, c), mat(c, c), vec(c), vec(c), mat(c, d), mat(c, d), mat(d, d),
                    vec(d), vec(d)],
        out_specs=rowblk(d),
        out_shape=jax.ShapeDtypeStruct((rows, d), F32),
        compiler_params=_params(("arbitrary",)),
        name="mix_out",
    )(x, pat, u, gs, gr, *y5, *yrw, *bonus, *g,
      p['s5_d'].reshape(1, c), p['s5_w_glu'].astype(BF16), head_ones,
      p['rw_lnx_w'].reshape(1, c), p['rw_lnx_b'].reshape(1, c),
      p['w_up_s5'].astype(BF16), p['w_up_rwkv'].astype(BF16), p['w_out'].astype(BF16),
      p['ln1_g'].reshape(1, d), p['ln1_b'].reshape(1, d))


def _first_max(x, lane, valid):
    xm = jnp.where(valid, x, -jnp.inf)
    m = jnp.max(xm, -1, keepdims=True)
    idx = jnp.min(jnp.where(xm == m, lane, float(N_EXPERTS)), -1, keepdims=True)
    return m, idx


def _moe_kernel(x_ref, pat_ref, wr_ref, br_ref, wg_ref, wu_ref, wd_ref, ln2g_ref, ln2b_ref, o_ref,
                hp_ref, cp_ref, acc_ref, pos_ref, bounds_ref):
    e = pl.program_id(1)
    ne = pl.num_programs(1)
    tm = x_ref.shape[0]

    @pl.when(e == 0)
    def _():
        h = _modulate(x_ref[...], pat_ref[0, 3], pat_ref[0, 4])
        wr = wr_ref[...]
        w_hi = wr.astype(BF16)
        w_lo = (wr - w_hi.astype(F32)).astype(BF16)
        h_hi = h.astype(BF16)
        h_lo = (h - h_hi.astype(F32)).astype(BF16)
        logits = (jnp.dot(h_hi, w_hi, preferred_element_type=F32)
                  + jnp.dot(h_hi, w_lo, preferred_element_type=F32)
                  + jnp.dot(h_lo, w_hi, preferred_element_type=F32)) + br_ref[...]
        logits = logits - jnp.max(logits, -1, keepdims=True)
        ex = jnp.exp(logits)
        probs = ex / jnp.sum(ex, -1, keepdims=True)
        lane_i = lax.broadcasted_iota(jnp.int32, (tm, N_EXPERTS), 1)
        lane = lane_i.astype(F32)
        grp = (lane_i // EXPERTS_PER_GROUP).astype(F32)
        best_score = jnp.full((tm, 1), -jnp.inf, F32)
        best_grp = jnp.zeros((tm, 1), F32)
        for gi in range(N_GROUPS):
            in_g = grp == float(gi)
            m1, i1 = _first_max(probs, lane, in_g)
            m2, _ = _first_max(probs, lane, in_g & (lane != i1))
            score = m1 + m2
            better = score > best_score
            best_score = jnp.where(better, score, best_score)
            best_grp = jnp.where(better, float(gi), best_grp)
        in_best = grp == best_grp
        m1, i1 = _first_max(probs, lane, in_best)
        m2, i2 = _first_max(probs, lane, in_best & (lane != i1))
        tot = m1 + m2
        comb = jnp.where(lane == i1, m1 / tot, 0.0) + jnp.where(lane == i2, m2 / tot, 0.0)

        glane = lax.broadcasted_iota(jnp.int32, (tm, LANES), 1).astype(F32)
        onehot = (glane == best_grp).astype(BF16)
        r_id = lax.broadcasted_iota(jnp.int32, (tm, tm), 0)
        c_id = lax.broadcasted_iota(jnp.int32, (tm, tm), 1)
        prefix = jnp.dot((r_id >= c_id).astype(BF16), onehot, preferred_element_type=F32)
        counts = prefix[tm - 1:tm, :]
        lane1 = lax.broadcasted_iota(jnp.int32, (1, LANES), 1)
        start = jnp.zeros((1, LANES), F32)
        run = jnp.zeros((1, 1), F32)
        bounds_ref[0] = 0
        for gi in range(N_GROUPS):
            start = start + jnp.where(lane1 == gi, run, 0.0)
            run = run + jnp.sum(jnp.where(lane1 == gi, counts, 0.0), -1, keepdims=True)
            bounds_ref[gi + 1] = run[0, 0].astype(jnp.int32)
        pos = jnp.sum(onehot.astype(F32) * (start + prefix - 1.0), -1, keepdims=True)
        pos_row = jnp.transpose(jnp.broadcast_to(pos, (tm, LANES)))[0:1, :]
        perm = (r_id.astype(F32) == pos_row).astype(BF16)
        hp_ref[...] = jnp.dot(perm, h.astype(BF16), preferred_element_type=F32).astype(BF16)
        cp_ref[...] = _permute_rows(perm, comb)
        pos_ref[...] = pos
        acc_ref[...] = jnp.zeros_like(acc_ref)

    wg = wg_ref[0, 0].astype(BF16)
    wu = wu_ref[0, 0].astype(BF16)
    wd = wd_ref[0, 0].astype(BF16)
    sub = min(tm, MOE_SUB_ROWS)
    grp_first = bounds_ref[e // EXPERTS_PER_GROUP]
    grp_end = bounds_ref[e // EXPERTS_PER_GROUP + 1]

    for rb in range(tm // sub):
        @pl.when((grp_first < (rb + 1) * sub) & (grp_end > rb * sub))
        def _(rb=rb):
            rs = pl.ds(rb * sub, sub)
            hb = hp_ref[rs, :]
            lane = lax.broadcasted_iota(jnp.int32, (sub, N_EXPERTS), 1)
            ce = jnp.sum(jnp.where(lane == e, cp_ref[rs, :], 0.0), -1, keepdims=True)
            hid = (_silu(jnp.dot(hb, wg, preferred_element_type=F32))
                   * jnp.dot(hb, wu, preferred_element_type=F32))
            acc_ref[rs, :] += jnp.dot((hid * ce).astype(BF16), wd, preferred_element_type=F32)

    @pl.when(e == ne - 1)
    def _():
        c_lane = lax.broadcasted_iota(jnp.int32, (tm, tm), 1).astype(F32)
        unperm = (c_lane == pos_ref[...]).astype(BF16)
        ffn = _permute_rows(unperm, acc_ref[...])
        ffn = _modulate(ffn, jnp.zeros_like(pat_ref[0, 5]), pat_ref[0, 5] - 1.0)
        o_ref[...] = _layer_norm(ALPHA * x_ref[...] + ffn, ln2g_ref[...], ln2b_ref[...])


MOE_SUB_ROWS = 256


def _moe(x, pat, w_router, b_router, wg, wu, wd, layer, ln2_g, ln2_b, rows_per_trunk, tm=1024):
    rows = x.shape[0]
    tm = min(tm, rows_per_trunk)
    d = D_MODEL
    tpt = rows_per_trunk // tm
    rowblk = pl.BlockSpec((tm, d), lambda i, e: (i, 0))
    vec = lambda n: pl.BlockSpec((1, n), lambda i, e: (0, 0))
    return pl.pallas_call(
        _moe_kernel,
        grid=(rows // tm, N_EXPERTS),
        in_specs=[rowblk,
                  pl.BlockSpec((1, 6, SUBLANES, d), lambda i, e: (i // tpt, 0, 0, 0)),
                  pl.BlockSpec((d, N_EXPERTS), lambda i, e: (0, 0)), vec(N_EXPERTS),
                  pl.BlockSpec((1, 1, d, D_EXPERT), lambda i, e: (layer, e, 0, 0)),
                  pl.BlockSpec((1, 1, d, D_EXPERT), lambda i, e: (layer, e, 0, 0)),
                  pl.BlockSpec((1, 1, D_EXPERT, d), lambda i, e: (layer, e, 0, 0)),
                  vec(d), vec(d)],
        out_specs=rowblk,
        out_shape=jax.ShapeDtypeStruct((rows, d), F32),
        scratch_shapes=[pltpu.VMEM((tm, d), BF16), pltpu.VMEM((tm, N_EXPERTS), F32),
                        pltpu.VMEM((tm, d), F32), pltpu.VMEM((tm, 1), F32),
                        pltpu.SMEM((SUBLANES,), jnp.int32)],
        compiler_params=_params(("arbitrary", "arbitrary")),
        name="moe",
    )(x, pat, w_router, b_router.reshape(1, N_EXPERTS), wg, wu, wd,
      ln2_g.reshape(1, d), ln2_b.reshape(1, d))


def kernel(x_prompt, x_sample, state_s5, state_rwkv, c, c_ctx, w_ada, b_ada, w_in, s5_a_re, s5_a_im, s5_log_dt, s5_b_re, s5_b_im, s5_c_re, s5_c_im, s5_d, s5_w_glu, rw_mu, rw_w0, rw_w2, rw_a0, rw_a2, rw_g2, rw_k_k, rw_k_a, rw_r_k, rw_lnx_w, rw_lnx_b, w_up_s5, w_up_rwkv, w_out, ln1_g, ln1_b, ln2_g, ln2_b, w_router, b_router, w_exp_gate, w_exp_up, w_exp_down):
    nbc, tc_len, d = x_prompt.shape
    nbs, ts_len, _ = x_sample.shape
    nl = w_ada.shape[0]
    rc = nbc * tc_len
    rs = nbs * ts_len
    assert rc == rs, "both trunks are processed as equal halves of one row-major token matrix"
    assert SUBLANES % nbs == 0 and nbc % SUBLANES == 0

    x = _to_time_major(x_prompt, x_sample)

    cond = jnp.concatenate([c_ctx[None], c], axis=0)
    cond8 = jnp.zeros((SUBLANES, d), F32).at[:cond.shape[0]].set(cond)
    mod = _ada_mod(cond8, w_ada, b_ada).reshape(nl, SUBLANES, 6, d)
    ctx_rows = jnp.zeros((SUBLANES,), jnp.int32)
    smp_rows = 1 + jnp.arange(SUBLANES, dtype=jnp.int32) % nbs
    pat_idx = jnp.stack([ctx_rows, smp_rows])
    pats = mod[:, pat_idx]
    pats = pats.transpose(0, 1, 3, 2, 4)

    m = nl * 2
    ab_re, ab_im, wb_re, wb_im, wc = _s5_weights(
        s5_a_re.reshape(m, G_S5, N_S5), s5_a_im.reshape(m, G_S5, N_S5), s5_log_dt.reshape(m, G_S5, 1),
        s5_b_re.reshape(m, G_S5, N_S5, S5_GROUP).transpose(0, 1, 3, 2),
        s5_b_im.reshape(m, G_S5, N_S5, S5_GROUP).transpose(0, 1, 3, 2),
        s5_c_re.reshape(m, G_S5, S5_GROUP, N_S5).transpose(0, 1, 3, 2).reshape(m, S5_STATE, S5_GROUP),
        s5_c_im.reshape(m, G_S5, S5_GROUP, N_S5).transpose(0, 1, 3, 2).reshape(m, S5_STATE, S5_GROUP))
    ab_re = ab_re.reshape(nl, 2, 1, S5_STATE)
    ab_im = ab_im.reshape(nl, 2, 1, S5_STATE)
    wb_re = wb_re.reshape(nl, 2, D_S5, S5_STATE)
    wb_im = wb_im.reshape(nl, 2, D_S5, S5_STATE)
    wc = wc.reshape(nl, 2, 2 * S5_STATE, D_S5)

    w_in_bf16 = w_in.astype(BF16)
    zero_s5 = jnp.zeros((2, 2, nbc, S5_STATE), F32)
    zero_rw = jnp.zeros((2, RWKV_HEAD, RWKV_HEAD, nbc * H_RWKV), F32)
    s5_out, rw_out = [], []
    for l in range(nl):
        p = dict(s5_d=s5_d[l], s5_w_glu=s5_w_glu[l], rw_mu=rw_mu[l], rw_w0=rw_w0[l], rw_w2=rw_w2[l],
                 rw_a0=rw_a0[l], rw_a2=rw_a2[l], rw_g2=rw_g2[l], rw_k_k=rw_k_k[l], rw_k_a=rw_k_a[l],
                 rw_r_k=rw_r_k[l], rw_lnx_w=rw_lnx_w[l], rw_lnx_b=rw_lnx_b[l], w_up_s5=w_up_s5[l],
                 w_up_rwkv=w_up_rwkv[l], w_out=w_out[l], ln1_g=ln1_g[l], ln1_b=ln1_b[l])
        pat = pats[l]
        u, zr, gs, gr = _inproj(x, pat, w_in_bf16, l, rc)

        s5w = (wb_re[l], wb_im[l], ab_re[l], ab_im[l], wc[l])
        y5c, hfin = _s5_scan(u, *s5w, zero_s5, nbc, 0, rc)
        h0s = state_s5[:, l].reshape(nbs, 2, 2, S5_STATE).transpose(1, 2, 0, 3)
        y5s, _ = _s5_scan(u, *s5w, h0s, nbs, rc, rs)
        s5_out.append(hfin.transpose(2, 0, 1, 3).reshape(nbc, 2, 2, G_S5, N_S5))

        prep_c = _rwkv_prep(zr, p, nbc, tc_len, False, 0)
        prep_s = _rwkv_prep(zr, p, nbs, ts_len, True, rc)
        yc, sfin = _rwkv_scan(*_relayout_ctx(prep_c, nbc, tc_len), zero_rw, tc=min(32, tc_len), folded=False)
        ys, _ = _rwkv_scan(*_relayout_smp(prep_s, nbs, ts_len, 0), _rwkv_state_smp(state_rwkv[:, l], nbs),
                           tc=min(64, ts_len // 2), folded=True)
        yrw = (_unmix_ctx(yc, nbc, tc_len), _unmix_smp(ys, nbs, ts_len))
        rw_out.append(_rwkv_state_out_ctx(sfin, nbc))

        x = _mix_out(x, pat, u, gs, gr, (y5c, y5s), yrw, (prep_c[3], prep_s[3]), (prep_c[4], prep_s[4]), p, rc)
        x = _moe(x, pat, w_router, b_router, w_exp_gate, w_exp_up, w_exp_down, l, ln2_g[l], ln2_b[l], rc)

    y_prompt = _from_time_major(x, nbc, tc_len, 0)
    y_sample = _from_time_major(x, nbs, ts_len, rc)
    return (y_prompt, y_sample, jnp.stack(s5_out, 1), jnp.stack(rw_out, 1))
```

```python
import functools

import numpy as np
import jax
import jax.numpy as jnp
from jax import lax
from jax.experimental import pallas as pl
from jax.experimental.pallas import tpu as pltpu

D_MODEL = 1024
DEPTH = 2
GRID_W = 64
D_S5 = 512
S5_GROUP = 16
G_S5 = 32
N_S5 = 64
S5_STATE = G_S5 * N_S5
D_RWKV = 512
RWKV_HEAD = 64
H_RWKV = 8
LORA_W = 64
LORA_A = 64
LORA_G = 128
RWKV_COLS = 3 * D_RWKV + LORA_W + LORA_A + LORA_G
N_EXPERTS = 16
N_GROUPS = 4
EXPERTS_PER_GROUP = 4
D_EXPERT = 512
ALPHA = (2 * DEPTH) ** 0.25
LN_EPS = 1e-5
GN_EPS = 64e-5

SUBLANES = 8
LANES = 128
VMEM_LIMIT = 56 * 1024 * 1024

F32 = jnp.float32
BF16 = jnp.bfloat16


def _params(sem):
    return pltpu.CompilerParams(dimension_semantics=sem, vmem_limit_bytes=VMEM_LIMIT)


def _dot(a, b):
    return jnp.dot(a.astype(BF16), b.astype(BF16), preferred_element_type=F32)


def _dot_split(x, w_exact):
    hi = x.astype(BF16)
    lo = (x - hi.astype(F32)).astype(BF16)
    return (jnp.dot(hi, w_exact, preferred_element_type=F32)
            + jnp.dot(lo, w_exact, preferred_element_type=F32))


def _sigmoid(x):
    return 1.0 / (1.0 + jnp.exp(-x))


def _silu(x):
    return x * _sigmoid(x)


def _layer_norm(x, g, b):
    mu = jnp.mean(x, -1, keepdims=True)
    xc = x - mu
    var = jnp.mean(xc * xc, -1, keepdims=True)
    return xc * lax.rsqrt(var + LN_EPS) * g + b


def _modulate(x, shift8, scale8):
    rows, d = x.shape
    x3 = x.reshape(rows // SUBLANES, SUBLANES, d)
    return (x3 * (1.0 + scale8)[None] + shift8[None]).reshape(rows, d)


def _to_time_major_kernel(xc_ref, xs_ref, o_ref, *, ctx_tiles):
    i = pl.program_id(0)

    @pl.when(i < ctx_tiles)
    def _():
        nb, steps, _ = xc_ref.shape
        for t in range(steps):
            o_ref[pl.ds(t * nb, nb), :] = xc_ref[:, t, :]

    @pl.when(i >= ctx_tiles)
    def _():
        nb, steps, d = xs_ref.shape
        rows = nb * steps
        r = lax.broadcasted_iota(jnp.int32, (rows, rows), 0)
        c = lax.broadcasted_iota(jnp.int32, (rows, rows), 1)
        perm = ((r % nb) * steps + r // nb == c).astype(BF16)
        o_ref[...] = _permute_rows(perm, xs_ref[...].reshape(rows, d))


def _permute_rows(perm, x):
    hi = x.astype(BF16)
    r1 = x - hi.astype(F32)
    mid = r1.astype(BF16)
    lo = (r1 - mid.astype(F32)).astype(BF16)
    return (jnp.dot(perm, hi, preferred_element_type=F32) + jnp.dot(perm, mid, preferred_element_type=F32)
            + jnp.dot(perm, lo, preferred_element_type=F32))


def _to_time_major(x_ctx, x_smp, tm=256):
    nbc, tcl, d = x_ctx.shape
    nbs, tsl, _ = x_smp.shape
    ct = nbc * tcl // tm
    st = nbs * tsl // tm
    return pl.pallas_call(
        functools.partial(_to_time_major_kernel, ctx_tiles=ct),
        grid=(ct + st,),
        in_specs=[pl.BlockSpec((nbc, tm // nbc, d), lambda i: (0, jnp.minimum(i, ct - 1), 0)),
                  pl.BlockSpec((nbs, tm // nbs, d), lambda i: (0, jnp.maximum(i - ct, 0), 0))],
        out_specs=pl.BlockSpec((tm, d), lambda i: (i, 0)),
        out_shape=jax.ShapeDtypeStruct((nbc * tcl + nbs * tsl, d), F32),
        compiler_params=_params(("arbitrary",)),
        name="to_time_major",
    )(x_ctx, x_smp)


def _from_time_major_kernel(x_ref, o_ref):
    nb, steps, _ = o_ref.shape
    if nb >= SUBLANES:
        for t in range(steps):
            o_ref[:, t, :] = x_ref[pl.ds(t * nb, nb), :]
    else:
        rows = nb * steps
        r = lax.broadcasted_iota(jnp.int32, (rows, rows), 0)
        c = lax.broadcasted_iota(jnp.int32, (rows, rows), 1)
        perm = ((r % steps) * nb + r // steps == c).astype(BF16)
        o_ref[...] = _permute_rows(perm, x_ref[...]).reshape(o_ref.shape)


def _from_time_major(x, nb, seq, row0, tm=256):
    d = x.shape[1]
    t0 = row0 // tm
    return pl.pallas_call(
        _from_time_major_kernel,
        grid=(seq * nb // tm,),
        in_specs=[pl.BlockSpec((tm, d), lambda i: (t0 + i, 0))],
        out_specs=pl.BlockSpec((nb, tm // nb, d), lambda i: (0, i, 0)),
        out_shape=jax.ShapeDtypeStruct((nb, seq, d), F32),
        compiler_params=_params(("arbitrary",)),
        name="from_time_major",
    )(x)


def _ada_kernel(cond_ref, w_ref, b_ref, o_ref):
    c = cond_ref[...]
    o_ref[0] = _dot(_silu(c), w_ref[0]) + b_ref[0]


def _ada_mod(cond8, w_ada, b_ada):
    nl = w_ada.shape[0]
    d = D_MODEL
    return pl.pallas_call(
        _ada_kernel,
        grid=(nl, 6),
        in_specs=[pl.BlockSpec((SUBLANES, d), lambda l, k: (0, 0)),
                  pl.BlockSpec((1, d, d), lambda l, k: (l, 0, k)),
                  pl.BlockSpec((1, 1, d), lambda l, k: (l, 0, k))],
        out_specs=pl.BlockSpec((1, SUBLANES, d), lambda l, k: (l, 0, k)),
        out_shape=jax.ShapeDtypeStruct((nl, SUBLANES, 6 * d), F32),
        compiler_params=_params(("arbitrary", "arbitrary")),
        name="ada_mod",
    )(cond8, w_ada, b_ada.reshape(nl, 1, 6 * d))


IN_SPLITS = (0, D_S5, D_S5 + RWKV_COLS, D_S5 + RWKV_COLS + D_MODEL, D_S5 + RWKV_COLS + 2 * D_MODEL)


def _inproj_kernel(x_ref, pat_ref, w_ref, u_ref, zr_ref, gs_ref, gr_ref):
    h = _modulate(x_ref[...], pat_ref[0, 0], pat_ref[0, 1]).astype(BF16)
    for k, out in enumerate((u_ref, zr_ref, gs_ref, gr_ref)):
        out[...] = jnp.dot(h, w_ref[0, :, IN_SPLITS[k]:IN_SPLITS[k + 1]], preferred_element_type=F32)


def _inproj(x, pat, w_in_bf16, layer, rows_per_trunk, tm=256):
    rows = x.shape[0]
    d = D_MODEL
    tpt = rows_per_trunk // tm
    widths = [IN_SPLITS[k + 1] - IN_SPLITS[k] for k in range(4)]
    rowblk = lambda n: pl.BlockSpec((tm, n), lambda i: (i, 0))
    return pl.pallas_call(
        _inproj_kernel,
        grid=(rows // tm,),
        in_specs=[rowblk(d),
                  pl.BlockSpec((1, 6, SUBLANES, d), lambda i: (i // tpt, 0, 0, 0)),
                  pl.BlockSpec((1, d, IN_SPLITS[-1]), lambda i: (layer, 0, 0))],
        out_specs=[rowblk(n) for n in widths],
        out_shape=[jax.ShapeDtypeStruct((rows, n), F32) for n in widths],
        compiler_params=_params(("arbitrary",)),
        name="in_proj",
    )(x, pat, w_in_bf16)


def _s5_disc_kernel(are_ref, aim_ref, ldt_ref, bre_ref, bim_ref, cre_ref, cim_ref,
                    abre_ref, abim_ref, wbre_ref, wbim_ref, wc_ref):
    a_re = jnp.minimum(are_ref[0], -1e-4)
    a_im = aim_ref[0]
    dt = jnp.exp(ldt_ref[0])
    mag = jnp.exp(a_re * dt)
    ab_re = mag * jnp.cos(a_im * dt)
    ab_im = mag * jnp.sin(a_im * dt)
    den = a_re * a_re + a_im * a_im
    nr = ab_re - 1.0
    q_re = (nr * a_re + ab_im * a_im) / den
    q_im = (ab_im * a_re - nr * a_im) / den
    abre_ref[0] = ab_re
    abim_ref[0] = ab_im
    b_re = bre_ref[0]
    b_im = bim_ref[0]
    bb_re = (q_re[:, None, :] * b_re - q_im[:, None, :] * b_im).reshape(D_S5, N_S5)
    bb_im = (q_re[:, None, :] * b_im + q_im[:, None, :] * b_re).reshape(D_S5, N_S5)

    def spread(x, n_rep, row_group, col_group):
        rows, w = x.shape
        tiled_eye = (lax.broadcasted_iota(jnp.int32, (w, n_rep * w), 0)
                     == lax.broadcasted_iota(jnp.int32, (w, n_rep * w), 1) % w).astype(BF16)
        rep = jnp.dot(x.astype(BF16), tiled_eye, preferred_element_type=F32)
        keep = (lax.broadcasted_iota(jnp.int32, rep.shape, 0) // row_group
                == lax.broadcasted_iota(jnp.int32, rep.shape, 1) // col_group)
        return jnp.where(keep, rep, 0.0).astype(BF16)

    wbre_ref[0] = spread(bb_re, G_S5, S5_GROUP, N_S5)
    wbim_ref[0] = spread(bb_im, G_S5, S5_GROUP, N_S5)
    wc_ref[0, :S5_STATE] = spread(cre_ref[0], G_S5, N_S5, S5_GROUP)
    wc_ref[0, S5_STATE:] = spread(-cim_ref[0], G_S5, N_S5, S5_GROUP)


def _s5_weights(a_re, a_im, log_dt, b_re, b_im, c_re, c_im):
    m = a_re.shape[0]
    s = S5_STATE
    blk = lambda *shape: pl.BlockSpec((1,) + shape, lambda i: (i,) + (0,) * len(shape))
    return pl.pallas_call(
        _s5_disc_kernel,
        grid=(m,),
        in_specs=[blk(G_S5, N_S5), blk(G_S5, N_S5), blk(G_S5, 1), blk(G_S5, S5_GROUP, N_S5),
                  blk(G_S5, S5_GROUP, N_S5), blk(s, S5_GROUP), blk(s, S5_GROUP)],
        out_specs=[blk(G_S5, N_S5), blk(G_S5, N_S5), blk(D_S5, s), blk(D_S5, s), blk(2 * s, D_S5)],
        out_shape=[jax.ShapeDtypeStruct((m, G_S5, N_S5), F32), jax.ShapeDtypeStruct((m, G_S5, N_S5), F32),
                   jax.ShapeDtypeStruct((m, D_S5, s), BF16), jax.ShapeDtypeStruct((m, D_S5, s), BF16),
                   jax.ShapeDtypeStruct((m, 2 * s, D_S5), BF16)],
        compiler_params=_params(("arbitrary",)),
        name="s5_weights",
    )(a_re, a_im, log_dt, b_re, b_im, c_re, c_im)


def _s5_scan_kernel(u_ref, wbre_ref, wbim_ref, abre_ref, abim_ref, wc_ref, h0_ref,
                    y_ref, hfin_ref, bure_ref, buim_ref, hre_ref, him_ref, *, nb, lane_w):
    d = pl.program_id(0)
    c = pl.program_id(1)
    nc = pl.num_programs(1)
    rows = u_ref.shape[0]

    @pl.when(c == 0)
    def _():
        hre_ref[...] = h0_ref[0, 0]
        him_ref[...] = h0_ref[0, 1]

    ub = u_ref[...].astype(BF16)
    ct = 2 * LANES
    for j in range(S5_STATE // ct):
        first_channel = j * ct * S5_GROUP // N_S5
        ks = slice(first_channel // ct * ct, first_channel // ct * ct + ct)
        js = slice(ct * j, ct * (j + 1))
        bure_ref[:, js] = jnp.dot(ub[:, ks], wbre_ref[0, ks, js], preferred_element_type=F32)
        buim_ref[:, js] = jnp.dot(ub[:, ks], wbim_ref[0, ks, js], preferred_element_type=F32)

    fwd = d == 0
    if nb >= SUBLANES:
        steps = rows // nb
        for lc in range(S5_STATE // lane_w):
            ls = slice(lc * lane_w, (lc + 1) * lane_w)
            ar = jnp.broadcast_to(abre_ref[0, :, ls], (nb, lane_w))
            ai = jnp.broadcast_to(abim_ref[0, :, ls], (nb, lane_w))

            def body(s, carry, ls=ls, ar=ar, ai=ai):
                hr, hi = carry
                t = jnp.where(fwd, s, steps - 1 - s)
                r0 = pl.multiple_of(t * nb, nb)
                br = bure_ref[pl.ds(r0, nb), ls]
                bi = buim_ref[pl.ds(r0, nb), ls]
                nr = ar * hr - ai * hi + br
                ni = ar * hi + ai * hr + bi
                bure_ref[pl.ds(r0, nb), ls] = nr
                buim_ref[pl.ds(r0, nb), ls] = ni
                return nr, ni

            hr, hi = lax.fori_loop(0, steps, body, (hre_ref[:, ls], him_ref[:, ls]))
            hre_ref[:, ls] = hr
            him_ref[:, ls] = hi
    else:
        per = SUBLANES // nb
        groups = rows // SUBLANES
        row_id = lax.broadcasted_iota(jnp.int32, (SUBLANES, lane_w), 0) // nb
        shift = jnp.where(fwd, nb, SUBLANES - nb)
        for lc in range(S5_STATE // lane_w):
            ls = slice(lc * lane_w, (lc + 1) * lane_w)
            ar = jnp.broadcast_to(abre_ref[0, :, ls], (SUBLANES, lane_w))
            ai = jnp.broadcast_to(abim_ref[0, :, ls], (SUBLANES, lane_w))
            h0r = hre_ref[:, ls]
            h0i = him_ref[:, ls]

            def body(gidx, carry, ls=ls, ar=ar, ai=ai):
                tr, ti = carry
                g = jnp.where(fwd, gidx, groups - 1 - gidx)
                r0 = pl.multiple_of(g * SUBLANES, SUBLANES)
                br = bure_ref[pl.ds(r0, SUBLANES), ls]
                bi = buim_ref[pl.ds(r0, SUBLANES), ls]
                for k in range(per):
                    pr = pltpu.roll(tr, shift, 0)
                    pi = pltpu.roll(ti, shift, 0)
                    nr = ar * pr - ai * pi + br
                    ni = ar * pi + ai * pr + bi
                    tgt = jnp.where(fwd, k, per - 1 - k)
                    sel = row_id == tgt
                    tr = jnp.where(sel, nr, tr)
                    ti = jnp.where(sel, ni, ti)
                bure_ref[pl.ds(r0, SUBLANES), ls] = tr
                buim_ref[pl.ds(r0, SUBLANES), ls] = ti
                return tr, ti

            tr, ti = lax.fori_loop(0, groups, body, (h0r, h0i))
            hre_ref[:, ls] = tr
            him_ref[:, ls] = ti

    for q in range(D_S5 // ct):
        qs = slice(ct * q, ct * (q + 1))
        ss = slice(q * ct * N_S5 // S5_GROUP, (q + 1) * ct * N_S5 // S5_GROUP)
        si = slice(S5_STATE + ss.start, S5_STATE + ss.stop)
        y_ref[0, :, qs] = (jnp.dot(bure_ref[:, ss].astype(BF16), wc_ref[0, ss, qs], preferred_element_type=F32)
                           + jnp.dot(buim_ref[:, ss].astype(BF16), wc_ref[0, si, qs],
                                     preferred_element_type=F32))

    @pl.when(c == nc - 1)
    def _():
        hfin_ref[0, 0] = hre_ref[...]
        hfin_ref[0, 1] = him_ref[...]


def _s5_scan(u, wb_re, wb_im, ab_re, ab_im, wc, h0, nb, row0, rows, chunk_rows=512):
    if nb < SUBLANES:
        h0 = jnp.tile(h0, (1, 1, SUBLANES // nb, 1))
    srows = max(nb, SUBLANES)
    cr = min(chunk_rows, rows)
    nc = rows // cr
    c0 = row0 // cr
    s = S5_STATE
    lane_w = 256 if nb >= SUBLANES else 512
    chunk = lambda d, c: c + d * (nc - 1 - 2 * c)
    y, hfin = pl.pallas_call(
        functools.partial(_s5_scan_kernel, nb=nb, lane_w=lane_w),
        grid=(2, nc),
        in_specs=[pl.BlockSpec((cr, D_S5), lambda d, c: (c0 + chunk(d, c), 0)),
                  pl.BlockSpec((1, D_S5, s), lambda d, c: (d, 0, 0)),
                  pl.BlockSpec((1, D_S5, s), lambda d, c: (d, 0, 0)),
                  pl.BlockSpec((1, 1, s), lambda d, c: (d, 0, 0)),
                  pl.BlockSpec((1, 1, s), lambda d, c: (d, 0, 0)),
                  pl.BlockSpec((1, 2 * s, D_S5), lambda d, c: (d, 0, 0)),
                  pl.BlockSpec((1, 2, srows, s), lambda d, c: (d, 0, 0, 0))],
        out_specs=[pl.BlockSpec((1, cr, D_S5), lambda d, c: (d, chunk(d, c), 0)),
                   pl.BlockSpec((1, 2, srows, s), lambda d, c: (d, 0, 0, 0))],
        out_shape=[jax.ShapeDtypeStruct((2, rows, D_S5), F32),
                   jax.ShapeDtypeStruct((2, 2, srows, s), F32)],
        scratch_shapes=[pltpu.VMEM((cr, s), F32), pltpu.VMEM((cr, s), F32),
                        pltpu.VMEM((srows, s), F32), pltpu.VMEM((srows, s), F32)],
        compiler_params=_params(("arbitrary", "arbitrary")),
        name="s5_scan",
    )(u, wb_re, wb_im, ab_re, ab_im, wc, h0)
    if nb < SUBLANES:
        hfin = jnp.stack([hfin[0, :, srows - nb:], hfin[1, :, :nb]])
    return y, hfin


def _rwkv_prep_kernel(prev_ref, cur_ref, next_ref, mu_ref, ones_ref, g2_ref, kk_ref, ka_ref, rk_ref,
                      w0_ref, w2_ref, a0_ref, a2_ref,
                      r_ref, v_ref, nkk_ref, bonus_ref, g_ref, w_ref, kd_ref, b_ref, *, nb, seq, grid_shift):
    tm = cur_ref.shape[0]
    i = pl.program_id(0)
    cur = cur_ref[...]
    prv = prev_ref[...]
    nxt = next_ref[...]
    row = lax.broadcasted_iota(jnp.int32, (tm, RWKV_COLS), 0)
    lane = lax.broadcasted_iota(jnp.int32, (tm, RWKV_COLS), 1)
    t = (i * tm + row) // nb

    halo = prv.shape[0]

    def rows_before(s):
        if s == tm:
            return prv
        if s == halo:
            return jnp.concatenate([prv, cur[:tm - s]], axis=0)
        return jnp.where(row < s, pltpu.roll(prv, s, 0), pltpu.roll(cur, s, 0))

    def rows_after(s):
        if s == tm:
            return nxt
        if s == halo:
            return jnp.concatenate([cur[s:], nxt], axis=0)
        return jnp.where(row >= tm - s, pltpu.roll(nxt, tm - s, 0), pltpu.roll(cur, tm - s, 0))

    if grid_shift:
        tw = t % GRID_W
        left = jnp.where(tw == 0, 0.0, rows_before(nb))
        right = jnp.where(tw == GRID_W - 1, 0.0, rows_after(nb))
        up = jnp.where(t < GRID_W, 0.0, rows_before(nb * GRID_W))
        down = jnp.where(t >= seq - GRID_W, 0.0, rows_after(nb * GRID_W))
        m4 = lane % 4
        sh = jnp.where(m4 == 0, left, jnp.where(m4 == 1, right, jnp.where(m4 == 2, up, down)))
    else:
        before = jnp.where(t == 0, 0.0, rows_before(nb))
        after = jnp.where(t == seq - 1, 0.0, rows_after(nb))
        sh = jnp.where(lane % 2 == 0, before, after)

    z = cur + (sh - cur) * mu_ref[...]
    r = z[:, 0:D_RWKV]
    k = z[:, D_RWKV:2 * D_RWKV]
    v = z[:, 2 * D_RWKV:3 * D_RWKV]
    xw = z[:, 3 * D_RWKV:3 * D_RWKV + LORA_W]
    xa = z[:, 3 * D_RWKV + LORA_W:3 * D_RWKV + LORA_W + LORA_A]
    xg = z[:, 3 * D_RWKV + LORA_W + LORA_A:]
    ones = ones_ref[...]

    g_ref[...] = _dot(_sigmoid(xg), g2_ref[...])
    kk = k * kk_ref[...]
    nrm = jnp.sqrt(_dot_split(kk * kk, ones))
    kk = kk / jnp.maximum(nrm, 1e-12)
    r_ref[...] = r
    v_ref[...] = v
    nkk_ref[...] = -kk
    bonus_ref[...] = _dot_split(r * k * rk_ref[...], ones) * v
    txw = jnp.tanh(xw)
    for d in range(2):
        zw = -(w0_ref[d] + _dot(txw, w2_ref[d]))
        softplus = jnp.maximum(zw, 0.0) + jnp.log(1.0 + jnp.exp(-jnp.abs(zw)))
        w_log = -softplus - 0.5
        w_ref[d] = jnp.exp(-jnp.exp(w_log))
        a = _sigmoid(a0_ref[d] + _dot(xa, a2_ref[d]))
        kd_ref[d] = k * (1.0 + (a - 1.0) * ka_ref[...])
        b_ref[d] = kk * a


def _rwkv_prep(zr, p, nb, seq, grid_shift, row0, tm=128):
    rows = seq * nb
    nt = rows // tm
    t0 = row0 // tm
    if grid_shift:
        assert tm == nb * GRID_W, "one tile must be one grid row of the latent grid"
    else:
        assert tm % nb == 0 and tm >= nb
    kern = functools.partial(_rwkv_prep_kernel, nb=nb, seq=seq, grid_shift=grid_shift)
    c = D_RWKV
    vec = lambda n: pl.BlockSpec((1, n), lambda i: (0, 0))
    mat = lambda a, b: pl.BlockSpec((a, b), lambda i: (0, 0))
    row_out = pl.BlockSpec((tm, c), lambda i: (i, 0))
    dir_out = pl.BlockSpec((2, tm, c), lambda i: (0, i, 0))
    head_ones = jnp.kron(jnp.eye(H_RWKV, dtype=F32), jnp.ones((RWKV_HEAD, RWKV_HEAD), F32)).astype(BF16)
    halo = tm if grid_shift else nb
    assert halo % SUBLANES == 0
    hpt = tm // halo
    h0 = t0 * hpt
    in_specs = [pl.BlockSpec((halo, RWKV_COLS), lambda i: (h0 + jnp.maximum(i * hpt - 1, 0), 0)),
                pl.BlockSpec((tm, RWKV_COLS), lambda i: (t0 + i, 0)),
                pl.BlockSpec((halo, RWKV_COLS), lambda i: (h0 + jnp.minimum((i + 1) * hpt, nt * hpt - 1), 0)),
                vec(RWKV_COLS), mat(c, c), mat(LORA_G, c), vec(c), vec(c), vec(c),
                pl.BlockSpec((2, 1, c), lambda i: (0, 0, 0)),
                pl.BlockSpec((2, LORA_W, c), lambda i: (0, 0, 0)),
                pl.BlockSpec((2, 1, c), lambda i: (0, 0, 0)),
                pl.BlockSpec((2, LORA_A, c), lambda i: (0, 0, 0))]
    args = [zr, zr, zr, p['rw_mu'].reshape(1, -1), head_ones, p['rw_g2'], p['rw_k_k'].reshape(1, c),
            p['rw_k_a'].reshape(1, c), p['rw_r_k'].reshape(1, c), p['rw_w0'].reshape(2, 1, c), p['rw_w2'],
            p['rw_a0'].reshape(2, 1, c), p['rw_a2']]
    part = jax.ShapeDtypeStruct((rows, c), F32)
    return pl.pallas_call(
        kern,
        grid=(nt,),
        in_specs=in_specs,
        out_specs=[row_out] * 5 + [dir_out] * 3,
        out_shape=[part] * 5 + [jax.ShapeDtypeStruct((2, rows, c), F32)] * 3,
        compiler_params=_params(("arbitrary",)),
        name="rwkv_prep",
    )(*args)


def _rwkv_scan_kernel(a_ref, w_ref, b_ref, k_ref, r_ref, v_ref, s0_ref, y_ref, sfin_ref, s_ref, *,
                      nacc, row_blk, folded):
    d = pl.program_id(0)
    c = pl.program_id(1)
    nc = pl.num_programs(1)
    tc = a_ref.shape[1]
    nj = RWKV_HEAD
    j_unroll = 16 if row_blk >= 32 else nj
    rev = (c >= nc // 2) if folded else (d == 1)

    @pl.when(c == 0)
    def _():
        s_ref[...] = s0_ref[0]

    def step(s, carry):
        t = jnp.where(rev, tc - 1 - s, s)
        zero = jnp.zeros((row_blk, a_ref.shape[3]), F32)
        for rb in range(s_ref.shape[1] // row_blk):
            rows = pl.ds(rb * row_blk, row_blk)

            def pass1(jo, acc):
                acc = list(acc)
                for ji in range(j_unroll):
                    j = jo * j_unroll + ji
                    acc[ji % nacc] = acc[ji % nacc] + s_ref[j, rows, :] * a_ref[0, t, pl.ds(j, 1), :]
                return tuple(acc)

            sa = functools.reduce(lambda x, y: x + y, lax.fori_loop(0, nj // j_unroll, pass1, (zero,) * nacc))
            vv = v_ref[0, t, rows, :]

            def pass2(jo, acc):
                acc = list(acc)
                for ji in range(j_unroll):
                    j = jo * j_unroll + ji
                    s_new = (s_ref[j, rows, :] * w_ref[0, t, pl.ds(j, 1), :]
                             + sa * b_ref[0, t, pl.ds(j, 1), :]
                             + vv * k_ref[0, t, pl.ds(j, 1), :])
                    s_ref[j, rows, :] = s_new
                    acc[ji % nacc] = acc[ji % nacc] + s_new * r_ref[0, t, pl.ds(j, 1), :]
                return tuple(acc)

            y_ref[0, t, rows, :] = functools.reduce(
                lambda x, y: x + y, lax.fori_loop(0, nj // j_unroll, pass2, (zero,) * nacc))
        return carry

    lax.fori_loop(0, tc, step, 0)

    @pl.when(c == nc - 1)
    def _():
        sfin_ref[0] = s_ref[...]


def _rwkv_scan(a, w, b, k, r, v, s0, tc, folded):
    _, tp, _, nl = w.shape
    iv = v.shape[2]
    row_blk = min(iv, 32)
    nacc = 4 if row_blk <= 16 else 2
    nc_half = tp // tc
    if folded:
        nd, nc = 1, 2 * nc_half
        tmap = lambda d, c: (c // nc_half, jnp.where(c < nc_half, c, nc - 1 - c), 0, 0)
        shared = per_dir = tmap
    else:
        nd, nc = 2, nc_half
        chunk = lambda d, c: c + d * (nc - 1 - 2 * c)
        shared = lambda d, c: (0, chunk(d, c), 0, 0)
        per_dir = lambda d, c: (d, chunk(d, c), 0, 0)
    key = lambda m: pl.BlockSpec((1, tc, RWKV_HEAD, nl), m)
    val = lambda m: pl.BlockSpec((1, tc, iv, nl), m)
    sspec = pl.BlockSpec((1, RWKV_HEAD, iv, nl), lambda d, c: (d, 0, 0, 0))
    return pl.pallas_call(
        functools.partial(_rwkv_scan_kernel, nacc=nacc, row_blk=row_blk, folded=folded),
        grid=(nd, nc),
        in_specs=[key(shared), key(per_dir), key(per_dir), key(per_dir), key(shared), val(shared), sspec],
        out_specs=[val(per_dir), sspec],
        out_shape=[jax.ShapeDtypeStruct((2, tp, iv, nl), F32),
                   jax.ShapeDtypeStruct((nd, RWKV_HEAD, iv, nl), F32)],
        scratch_shapes=[pltpu.VMEM((RWKV_HEAD, iv, nl), F32)],
        compiler_params=_params(("arbitrary", "arbitrary")),
        name="rwkv_scan",
    )(a, w, b, k, r, v, s0)


SMP_REP = 4
SMP_ROWS = RWKV_HEAD // SMP_REP


def _block_transpose(tiles, bs):
    tiles = list(tiles)
    lane = lax.broadcasted_iota(jnp.int32, tiles[0].shape, 1)
    for kbit in range(3):
        sft = bs << kbit
        bit = (lane // sft) % 2 == 1
        for r in range(8):
            if (r >> kbit) & 1:
                continue
            r2 = r | (1 << kbit)
            lo, hi = tiles[r], tiles[r2]
            tiles[r] = jnp.where(bit, pltpu.roll(hi, sft, 1), lo)
            tiles[r2] = jnp.where(bit, hi, pltpu.roll(lo, LANES - sft, 1))
    return tiles


def _head_tiles(x):
    xt = x.T
    return [xt[h * RWKV_HEAD:(h + 1) * RWKV_HEAD, :] for h in range(H_RWKV)]


def _from_head_tiles(tiles):
    return jnp.concatenate(tiles, axis=0).T


def _relayout_ctx_kernel(r_ref, v_ref, nkk_ref, w_ref, kd_ref, b_ref, a_o, r_o, v_o, w_o, b_o, k_o, *, nb):
    steps = a_o.shape[1]
    for src, dst in ((nkk_ref, a_o), (r_ref, r_o), (v_ref, v_o)):
        tiles = _block_transpose(_head_tiles(src[...]), nb)
        for t in range(steps):
            dst[0, t] = tiles[t]
    for src, dst in ((w_ref, w_o), (b_ref, b_o), (kd_ref, k_o)):
        for d in range(2):
            tiles = _block_transpose(_head_tiles(src[d]), nb)
            for t in range(steps):
                dst[d, t] = tiles[t]


def _relayout_ctx(prep, nb, seq):
    r, v, nkk, _, _, w, kd, bb = prep
    steps = LANES // nb
    assert steps == H_RWKV and nb * H_RWKV == LANES
    tm = steps * nb
    nt = seq // steps
    c = D_RWKV
    nat = pl.BlockSpec((tm, c), lambda i: (i, 0))
    nat2 = pl.BlockSpec((2, tm, c), lambda i: (0, i, 0))
    out1 = pl.BlockSpec((1, steps, RWKV_HEAD, LANES), lambda i: (0, i, 0, 0))
    out2 = pl.BlockSpec((2, steps, RWKV_HEAD, LANES), lambda i: (0, i, 0, 0))
    s1 = jax.ShapeDtypeStruct((1, seq, RWKV_HEAD, LANES), F32)
    s2 = jax.ShapeDtypeStruct((2, seq, RWKV_HEAD, LANES), F32)
    a, r_, v_, w_, b_, k_ = pl.pallas_call(
        functools.partial(_relayout_ctx_kernel, nb=nb),
        grid=(nt,),
        in_specs=[nat, nat, nat, nat2, nat2, nat2],
        out_specs=[out1, out1, out1, out2, out2, out2],
        out_shape=[s1, s1, s1, s2, s2, s2],
        compiler_params=_params(("arbitrary",)),
        name="rwkv_relayout_ctx",
    )(r, v, nkk, w, kd, bb)
    return a, w_, b_, k_, r_, v_


def _unmix_ctx_kernel(y_ref, o_ref, *, nb):
    steps = y_ref.shape[1]
    tiles = [y_ref[0, t] + y_ref[1, t] for t in range(steps)]
    o_ref[...] = _from_head_tiles(_block_transpose(tiles, nb))


def _unmix_ctx(y, nb, seq):
    steps = LANES // nb
    tm = steps * nb
    return pl.pallas_call(
        functools.partial(_unmix_ctx_kernel, nb=nb),
        grid=(seq // steps,),
        in_specs=[pl.BlockSpec((2, steps, RWKV_HEAD, LANES), lambda i: (0, i, 0, 0))],
        out_specs=pl.BlockSpec((tm, D_RWKV), lambda i: (i, 0)),
        out_shape=jax.ShapeDtypeStruct((seq * nb, D_RWKV), F32),
        compiler_params=_params(("arbitrary",)),
        name="rwkv_unmix_ctx",
    )(y)


def _relayout_smp_kernel(ra, rb, va, vb, na, nb_, wa, wb, ka, kb, ba, bb, sel_ref,
                         a_o, r_o, v_o, w_o, b_o, k_o):
    steps = a_o.shape[1]
    lane_v = lax.broadcasted_iota(jnp.int32, (SMP_ROWS, LANES), 1)

    def conv(xa, xb):
        tiles = _block_transpose(_head_tiles(jnp.concatenate([xa, xb], axis=0)), 2)
        parts = []
        for x in tiles:
            hi = x.astype(BF16)
            r1 = x - hi.astype(F32)
            mid = r1.astype(BF16)
            lo = (r1 - mid.astype(F32)).astype(BF16)
            parts.append((hi, mid, lo))
        return parts

    def expand(o0, o1, half):
        order0 = range(8) if half == 0 else range(7, -1, -1)
        order1 = range(7, -1, -1) if half == 0 else range(8)
        cols = []
        for term in range(3):
            cols.append(jnp.concatenate([o0[s][term] for s in order0], axis=0))
            cols.append(jnp.concatenate([o1[s][term] for s in order1], axis=0))
        lhs = jnp.concatenate(cols, axis=1)
        out = [None] * steps
        for pair in range(steps // 16):
            res = jnp.dot(lhs, sel_ref[half, pair], preferred_element_type=F32)
            for q in range(2):
                for s_lo in range(8):
                    out[8 * (2 * pair + q) + s_lo] = res[64 * s_lo:64 * (s_lo + 1), LANES * q:LANES * (q + 1)]
        return out

    def value_rows(full):
        blk = lane_v // (LANES // SMP_REP)
        rows = [full[k * SMP_ROWS:(k + 1) * SMP_ROWS] for k in range(SMP_REP)]
        return jnp.where(blk == 0, rows[0], jnp.where(blk == 1, rows[1], jnp.where(blk == 2, rows[2], rows[3])))

    for srca, srcb, dst in ((na, nb_, a_o), (ra, rb, r_o), (va, vb, v_o)):
        o = conv(srca[...], srcb[...])
        for half in range(2):
            tiles = expand(o, o, half)
            for s in range(steps):
                dst[half, s] = value_rows(tiles[s]) if dst is v_o else tiles[s]
    for srca, srcb, dst in ((wa, wb, w_o), (ba, bb, b_o), (ka, kb, k_o)):
        o0 = conv(srca[0], srcb[0])
        o1 = conv(srca[1], srcb[1])
        for half in range(2):
            tiles = expand(o0, o1, half)
            for s in range(steps):
                dst[half, s] = tiles[s]


def _smp_selection():
    sel = np.zeros((2, 2, 6, LANES, 2 * LANES), np.float32)
    for half in range(2):
        for pair in range(2):
            for q in range(2):
                s_hi = 2 * pair + q
                grp_a = 0 * 64 + 16 * s_hi
                grp_b = 1 * 64 + 16 * (3 - s_hi)
                src = (grp_a, grp_b) if half == 0 else (grp_b, grp_a)
                for d in range(2):
                    for blk in range(SMP_REP):
                        for hb in range(16):
                            col = q * LANES + blk * 32 + d * 16 + hb
                            for term in range(3):
                                sel[half, pair, 2 * term + d, src[d] + hb, col] = 1.0
    return jnp.asarray(sel.reshape(2, 2, 6 * LANES, 2 * LANES), BF16)


def _relayout_smp(prep, nb, seq, row0):
    r, v, nkk, _, _, w, kd, bb = prep
    assert 2 * nb * H_RWKV * SMP_REP == LANES
    steps = 32
    tm = steps * nb
    nt = seq // steps
    half_t = seq // 2
    c = D_RWKV
    t0 = row0 // tm
    blk_a = pl.BlockSpec((tm, c), lambda g: (t0 + g, 0))
    blk_b = pl.BlockSpec((tm, c), lambda g: (t0 + nt - 1 - g, 0))
    blk2_a = pl.BlockSpec((2, tm, c), lambda g: (0, t0 + g, 0))
    blk2_b = pl.BlockSpec((2, tm, c), lambda g: (0, t0 + nt - 1 - g, 0))
    okey = pl.BlockSpec((2, steps, RWKV_HEAD, LANES), lambda g: (0, g, 0, 0))
    oval = pl.BlockSpec((2, steps, SMP_ROWS, LANES), lambda g: (0, g, 0, 0))
    skey = jax.ShapeDtypeStruct((2, half_t, RWKV_HEAD, LANES), F32)
    sval = jax.ShapeDtypeStruct((2, half_t, SMP_ROWS, LANES), F32)
    a, r_, v_, w_, b_, k_ = pl.pallas_call(
        _relayout_smp_kernel,
        grid=(nt // 2,),
        in_specs=[blk_a, blk_b, blk_a, blk_b, blk_a, blk_b, blk2_a, blk2_b, blk2_a, blk2_b, blk2_a, blk2_b,
                  pl.BlockSpec((2, 2, 6 * LANES, 2 * LANES), lambda g: (0, 0, 0, 0))],
        out_specs=[okey, okey, oval, okey, okey, okey],
        out_shape=[skey, skey, sval, skey, skey, skey],
        compiler_params=_params(("arbitrary",)),
        name="rwkv_relayout_smp",
    )(r, r, v, v, nkk, nkk, w, w, kd, kd, bb, bb, _smp_selection())
    return a, w_, b_, k_, r_, v_


def _unmix_smp_kernel(y_ref, o_ref, *, half_tiles):
    q = pl.program_id(0)
    steps = y_ref.shape[1]
    low = q < half_tiles
    h0 = jnp.where(low, 0, 1)
    lane = lax.broadcasted_iota(jnp.int32, (SMP_ROWS, LANES), 1)
    tiles = []
    for s_lo in range(8):
        blocks = []
        for blk in range(SMP_REP):
            acc = jnp.zeros((SMP_ROWS, LANES), F32)
            for s_hi in range(steps // 8):
                s = 8 * s_hi + s_lo
                u = jnp.where(low, s, steps - 1 - s)
                d0 = y_ref[h0, u]
                d1 = y_ref[1 - h0, u]
                sh0 = (16 * s_hi - 32 * blk) % LANES
                sh1 = (16 * s_hi - 32 * blk - 16) % LANES
                t0 = d0 if sh0 == 0 else pltpu.roll(d0, sh0, 1)
                t1 = d1 if sh1 == 0 else pltpu.roll(d1, sh1, 1)
                acc = jnp.where(lane // 16 == s_hi, t0 + t1, acc)
            blocks.append(acc)
        tiles.append(jnp.concatenate(blocks, axis=0))
    o_ref[...] = _from_head_tiles(_block_transpose(tiles, 2))


def _unmix_smp(y, nb, seq):
    steps = 64
    tm = steps * nb
    assert tm == LANES
    nq = seq // steps
    half_tiles = nq // 2
    return pl.pallas_call(
        functools.partial(_unmix_smp_kernel, half_tiles=half_tiles),
        grid=(nq,),
        in_specs=[pl.BlockSpec((2, steps, SMP_ROWS, LANES),
                               lambda q: (0, jnp.where(q < half_tiles, q, nq - 1 - q), 0, 0))],
        out_specs=pl.BlockSpec((tm, D_RWKV), lambda q: (q, 0)),
        out_shape=jax.ShapeDtypeStruct((seq * nb, D_RWKV), F32),
        compiler_params=_params(("arbitrary",)),
        name="rwkv_unmix_smp",
    )(y)


def _rwkv_state_smp(state_l, nb):
    s = state_l.reshape(nb, 2, H_RWKV, SMP_REP, SMP_ROWS, RWKV_HEAD)
    s = s.transpose(5, 4, 3, 1, 2, 0)
    return s.reshape(1, RWKV_HEAD, SMP_ROWS, LANES)


def _rwkv_state_out_ctx(sfin, nb):
    s = sfin.reshape(2, RWKV_HEAD, RWKV_HEAD, H_RWKV, nb)
    return s.transpose(4, 0, 3, 2, 1)


def _mix_out_kernel(x_ref, pat_ref, u_ref, gs_ref, gr_ref,
                    y5c_ref, y5s_ref, yrwc_ref, yrws_ref, bonc_ref, bons_ref, gc_ref, gs2_ref,
                    d_ref, wglu_ref, ones_ref, lnxw_ref, lnxb_ref, wups_ref, wupr_ref, wout_ref,
                    ln1g_ref, ln1b_ref, o_ref, *, ctx_tiles):
    is_ctx = pl.program_id(0) < ctx_tiles
    pick = lambda a, b: jnp.where(is_ctx, a, b)
    y5 = u_ref[...] * d_ref[...] + pick(y5c_ref[0] + y5c_ref[1], y5s_ref[0] + y5s_ref[1])
    y5 = jax.nn.gelu(y5)
    y5 = y5 * _sigmoid(_dot(y5, wglu_ref[...]))
    ones = ones_ref[...]
    yr = pick(yrwc_ref[...], yrws_ref[...])
    inv_n = 1.0 / RWKV_HEAD
    mu = _dot_split(yr, ones) * inv_n
    yc = yr - mu
    var = _dot_split(yc * yc, ones) * inv_n
    yr = yc * lax.rsqrt(var + GN_EPS) * lnxw_ref[...] + lnxb_ref[...]
    yr = (yr + pick(bonc_ref[...], bons_ref[...])) * pick(gc_ref[...], gs2_ref[...])
    merged = (_sigmoid(gs_ref[...]) * _dot(y5, wups_ref[...])
              + _sigmoid(gr_ref[...]) * _dot(yr, wupr_ref[...]))
    mix = _dot(merged, wout_ref[...])
    x3 = _modulate(mix, jnp.zeros_like(pat_ref[0, 2]), pat_ref[0, 2] - 1.0)
    o_ref[...] = _layer_norm(ALPHA * x_ref[...] + x3, ln1g_ref[...], ln1b_ref[...])


def _mix_out(x, pat, u, gs, gr, y5, yrw, bonus, g, p, rows_per_trunk, tm=256):
    rows = x.shape[0]
    d = D_MODEL
    c = D_RWKV
    tpt = rows_per_trunk // tm
    rowblk = lambda n: pl.BlockSpec((tm, n), lambda i: (i, 0))
    ctx_i = lambda i: jnp.minimum(i, tpt - 1)
    smp_i = lambda i: jnp.maximum(i - tpt, 0)
    pair = lambda spec_of: [spec_of(ctx_i), spec_of(smp_i)]
    row_t = lambda f: pl.BlockSpec((tm, c), lambda i: (f(i), 0))
    dir_t = lambda f: pl.BlockSpec((2, tm, c), lambda i: (0, f(i), 0))
    vec = lambda n: pl.BlockSpec((1, n), lambda i: (0, 0))
    mat = lambda a, b: pl.BlockSpec((a, b), lambda i: (0, 0))
    head_ones = jnp.kron(jnp.eye(H_RWKV, dtype=F32), jnp.ones((RWKV_HEAD, RWKV_HEAD), F32)).astype(BF16)
    return pl.pallas_call(
        functools.partial(_mix_out_kernel, ctx_tiles=tpt),
        grid=(rows // tm,),
        in_specs=[rowblk(d),
                  pl.BlockSpec((1, 6, SUBLANES, d), lambda i: (i // tpt, 0, 0, 0)),
                  rowblk(c), rowblk(d), rowblk(d)]
                 + pair(dir_t) + pair(row_t) + pair(row_t) + pair(row_t)
                 + [vec(c), mat(c, c), mat(c, c), vec(c), vec(c), mat(c, d), mat(c, d), mat(d, d),
                    vec(d), vec(d)],
        out_specs=rowblk(d),
        out_shape=jax.ShapeDtypeStruct((rows, d), F32),
        compiler_params=_params(("arbitrary",)),
        name="mix_out",
    )(x, pat, u, gs, gr, *y5, *yrw, *bonus, *g,
      p['s5_d'].reshape(1, c), p['s5_w_glu'].astype(BF16), head_ones,
      p['rw_lnx_w'].reshape(1, c), p['rw_lnx_b'].reshape(1, c),
      p['w_up_s5'].astype(BF16), p['w_up_rwkv'].astype(BF16), p['w_out'].astype(BF16),
      p['ln1_g'].reshape(1, d), p['ln1_b'].reshape(1, d))


def _first_max(x, lane, valid):
    xm = jnp.where(valid, x, -jnp.inf)
    m = jnp.max(xm, -1, keepdims=True)
    idx = jnp.min(jnp.where(xm == m, lane, float(N_EXPERTS)), -1, keepdims=True)
    return m, idx


def _moe_kernel(x_ref, pat_ref, wr_ref, br_ref, wg_ref, wu_ref, wd_ref, ln2g_ref, ln2b_ref, o_ref,
                hp_ref, cp_ref, acc_ref, pos_ref, bounds_ref):
    e = pl.program_id(1)
    ne = pl.num_programs(1)
    tm = x_ref.shape[0]

    @pl.when(e == 0)
    def _():
        h = _modulate(x_ref[...], pat_ref[0, 3], pat_ref[0, 4])
        wr = wr_ref[...]
        w_hi = wr.astype(BF16)
        w_lo = (wr - w_hi.astype(F32)).astype(BF16)
        h_hi = h.astype(BF16)
        h_lo = (h - h_hi.astype(F32)).astype(BF16)
        logits = (jnp.dot(h_hi, w_hi, preferred_element_type=F32)
                  + jnp.dot(h_hi, w_lo, preferred_element_type=F32)
                  + jnp.dot(h_lo, w_hi, preferred_element_type=F32)) + br_ref[...]
        logits = logits - jnp.max(logits, -1, keepdims=True)
        ex = jnp.exp(logits)
        probs = ex / jnp.sum(ex, -1, keepdims=True)
        lane_i = lax.broadcasted_iota(jnp.int32, (tm, N_EXPERTS), 1)
        lane = lane_i.astype(F32)
        grp = (lane_i // EXPERTS_PER_GROUP).astype(F32)
        best_score = jnp.full((tm, 1), -jnp.inf, F32)
        best_grp = jnp.zeros((tm, 1), F32)
        for gi in range(N_GROUPS):
            in_g = grp == float(gi)
            m1, i1 = _first_max(probs, lane, in_g)
            m2, _ = _first_max(probs, lane, in_g & (lane != i1))
            score = m1 + m2
            better = score > best_score
            best_score = jnp.where(better, score, best_score)
            best_grp = jnp.where(better, float(gi), best_grp)
        in_best = grp == best_grp
        m1, i1 = _first_max(probs, lane, in_best)
        m2, i2 = _first_max(probs, lane, in_best & (lane != i1))
        tot = m1 + m2
        comb = jnp.where(lane == i1, m1 / tot, 0.0) + jnp.where(lane == i2, m2 / tot, 0.0)

        glane = lax.broadcasted_iota(jnp.int32, (tm, LANES), 1).astype(F32)
        onehot = (glane == best_grp).astype(BF16)
        r_id = lax.broadcasted_iota(jnp.int32, (tm, tm), 0)
        c_id = lax.broadcasted_iota(jnp.int32, (tm, tm), 1)
        prefix = jnp.dot((r_id >= c_id).astype(BF16), onehot, preferred_element_type=F32)
        counts = prefix[tm - 1:tm, :]
        lane1 = lax.broadcasted_iota(jnp.int32, (1, LANES), 1)
        start = jnp.zeros((1, LANES), F32)
        run = jnp.zeros((1, 1), F32)
        bounds_ref[0] = 0
        for gi in range(N_GROUPS):
            start = start + jnp.where(lane1 == gi, run, 0.0)
            run = run + jnp.sum(jnp.where(lane1 == gi, counts, 0.0), -1, keepdims=True)
            bounds_ref[gi + 1] = run[0, 0].astype(jnp.int32)
        pos = jnp.sum(onehot.astype(F32) * (start + prefix - 1.0), -1, keepdims=True)
        pos_row = jnp.transpose(jnp.broadcast_to(pos, (tm, LANES)))[0:1, :]
        perm = (r_id.astype(F32) == pos_row).astype(BF16)
        hp_ref[...] = jnp.dot(perm, h.astype(BF16), preferred_element_type=F32).astype(BF16)
        cp_ref[...] = _permute_rows(perm, comb)
        pos_ref[...] = pos
        acc_ref[...] = jnp.zeros_like(acc_ref)

    wg = wg_ref[0, 0]
    wu = wu_ref[0, 0]
    wd = wd_ref[0, 0]
    sub = min(tm, MOE_SUB_ROWS)
    grp_first = bounds_ref[e // EXPERTS_PER_GROUP]
    grp_end = bounds_ref[e // EXPERTS_PER_GROUP + 1]

    for rb in range(tm // sub):
        @pl.when((grp_first < (rb + 1) * sub) & (grp_end > rb * sub))
        def _(rb=rb):
            rs = pl.ds(rb * sub, sub)
            hb = hp_ref[rs, :]
            lane = lax.broadcasted_iota(jnp.int32, (sub, N_EXPERTS), 1)
            ce = jnp.sum(jnp.where(lane == e, cp_ref[rs, :], 0.0), -1, keepdims=True)
            hid = (_silu(jnp.dot(hb, wg, preferred_element_type=F32))
                   * jnp.dot(hb, wu, preferred_element_type=F32))
            acc_ref[rs, :] += jnp.dot((hid * ce).astype(BF16), wd, preferred_element_type=F32)

    @pl.when(e == ne - 1)
    def _():
        c_lane = lax.broadcasted_iota(jnp.int32, (tm, tm), 1).astype(F32)
        unperm = (c_lane == pos_ref[...]).astype(BF16)
        ffn = _permute_rows(unperm, acc_ref[...])
        ffn = _modulate(ffn, jnp.zeros_like(pat_ref[0, 5]), pat_ref[0, 5] - 1.0)
        o_ref[...] = _layer_norm(ALPHA * x_ref[...] + ffn, ln2g_ref[...], ln2b_ref[...])


MOE_SUB_ROWS = 256


def _moe(x, pat, w_router, b_router, wg, wu, wd, layer, ln2_g, ln2_b, rows_per_trunk, tm=1024):
    rows = x.shape[0]
    tm = min(tm, rows_per_trunk)
    d = D_MODEL
    tpt = rows_per_trunk // tm
    rowblk = pl.BlockSpec((tm, d), lambda i, e: (i, 0))
    vec = lambda n: pl.BlockSpec((1, n), lambda i, e: (0, 0))
    return pl.pallas_call(
        _moe_kernel,
        grid=(rows // tm, N_EXPERTS),
        in_specs=[rowblk,
                  pl.BlockSpec((1, 6, SUBLANES, d), lambda i, e: (i // tpt, 0, 0, 0)),
                  pl.BlockSpec((d, N_EXPERTS), lambda i, e: (0, 0)), vec(N_EXPERTS),
                  pl.BlockSpec((1, 1, d, D_EXPERT), lambda i, e: (layer, e, 0, 0)),
                  pl.BlockSpec((1, 1, d, D_EXPERT), lambda i, e: (layer, e, 0, 0)),
                  pl.BlockSpec((1, 1, D_EXPERT, d), lambda i, e: (layer, e, 0, 0)),
                  vec(d), vec(d)],
        out_specs=rowblk,
        out_shape=jax.ShapeDtypeStruct((rows, d), F32),
        scratch_shapes=[pltpu.VMEM((tm, d), BF16), pltpu.VMEM((tm, N_EXPERTS), F32),
                        pltpu.VMEM((tm, d), F32), pltpu.VMEM((tm, 1), F32),
                        pltpu.SMEM((SUBLANES,), jnp.int32)],
        compiler_params=_params(("arbitrary", "arbitrary")),
        name="moe",
    )(x, pat, w_router, b_router.reshape(1, N_EXPERTS), wg, wu, wd,
      ln2_g.reshape(1, d), ln2_b.reshape(1, d))


def kernel(x_prompt, x_sample, state_s5, state_rwkv, c, c_ctx, w_ada, b_ada, w_in, s5_a_re, s5_a_im, s5_log_dt, s5_b_re, s5_b_im, s5_c_re, s5_c_im, s5_d, s5_w_glu, rw_mu, rw_w0, rw_w2, rw_a0, rw_a2, rw_g2, rw_k_k, rw_k_a, rw_r_k, rw_lnx_w, rw_lnx_b, w_up_s5, w_up_rwkv, w_out, ln1_g, ln1_b, ln2_g, ln2_b, w_router, b_router, w_exp_gate, w_exp_up, w_exp_down):
    nbc, tc_len, d = x_prompt.shape
    nbs, ts_len, _ = x_sample.shape
    nl = w_ada.shape[0]
    rc = nbc * tc_len
    rs = nbs * ts_len
    assert rc == rs, "both trunks are processed as equal halves of one row-major token matrix"
    assert SUBLANES % nbs == 0 and nbc % SUBLANES == 0

    x = _to_time_major(x_prompt, x_sample)

    cond = jnp.concatenate([c_ctx[None], c], axis=0)
    cond8 = jnp.zeros((SUBLANES, d), F32).at[:cond.shape[0]].set(cond)
    mod = _ada_mod(cond8, w_ada, b_ada).reshape(nl, SUBLANES, 6, d)
    ctx_rows = jnp.zeros((SUBLANES,), jnp.int32)
    smp_rows = 1 + jnp.arange(SUBLANES, dtype=jnp.int32) % nbs
    pat_idx = jnp.stack([ctx_rows, smp_rows])
    pats = mod[:, pat_idx]
    pats = pats.transpose(0, 1, 3, 2, 4)

    m = nl * 2
    ab_re, ab_im, wb_re, wb_im, wc = _s5_weights(
        s5_a_re.reshape(m, G_S5, N_S5), s5_a_im.reshape(m, G_S5, N_S5), s5_log_dt.reshape(m, G_S5, 1),
        s5_b_re.reshape(m, G_S5, N_S5, S5_GROUP).transpose(0, 1, 3, 2),
        s5_b_im.reshape(m, G_S5, N_S5, S5_GROUP).transpose(0, 1, 3, 2),
        s5_c_re.reshape(m, G_S5, S5_GROUP, N_S5).transpose(0, 1, 3, 2).reshape(m, S5_STATE, S5_GROUP),
        s5_c_im.reshape(m, G_S5, S5_GROUP, N_S5).transpose(0, 1, 3, 2).reshape(m, S5_STATE, S5_GROUP))
    ab_re = ab_re.reshape(nl, 2, 1, S5_STATE)
    ab_im = ab_im.reshape(nl, 2, 1, S5_STATE)
    wb_re = wb_re.reshape(nl, 2, D_S5, S5_STATE)
    wb_im = wb_im.reshape(nl, 2, D_S5, S5_STATE)
    wc = wc.reshape(nl, 2, 2 * S5_STATE, D_S5)

    w_in_bf16 = w_in.astype(BF16)
    wg_bf16, wu_bf16, wd_bf16 = (w.astype(BF16) for w in (w_exp_gate, w_exp_up, w_exp_down))
    zero_s5 = jnp.zeros((2, 2, nbc, S5_STATE), F32)
    zero_rw = jnp.zeros((2, RWKV_HEAD, RWKV_HEAD, nbc * H_RWKV), F32)
    s5_out, rw_out = [], []
    for l in range(nl):
        p = dict(s5_d=s5_d[l], s5_w_glu=s5_w_glu[l], rw_mu=rw_mu[l], rw_w0=rw_w0[l], rw_w2=rw_w2[l],
                 rw_a0=rw_a0[l], rw_a2=rw_a2[l], rw_g2=rw_g2[l], rw_k_k=rw_k_k[l], rw_k_a=rw_k_a[l],
                 rw_r_k=rw_r_k[l], rw_lnx_w=rw_lnx_w[l], rw_lnx_b=rw_lnx_b[l], w_up_s5=w_up_s5[l],
                 w_up_rwkv=w_up_rwkv[l], w_out=w_out[l], ln1_g=ln1_g[l], ln1_b=ln1_b[l])
        pat = pats[l]
        u, zr, gs, gr = _inproj(x, pat, w_in_bf16, l, rc)

        s5w = (wb_re[l], wb_im[l], ab_re[l], ab_im[l], wc[l])
        y5c, hfin = _s5_scan(u, *s5w, zero_s5, nbc, 0, rc)
        h0s = state_s5[:, l].reshape(nbs, 2, 2, S5_STATE).transpose(1, 2, 0, 3)
        y5s, _ = _s5_scan(u, *s5w, h0s, nbs, rc, rs)
        s5_out.append(hfin.transpose(2, 0, 1, 3).reshape(nbc, 2, 2, G_S5, N_S5))

        prep_c = _rwkv_prep(zr, p, nbc, tc_len, False, 0)
        prep_s = _rwkv_prep(zr, p, nbs, ts_len, True, rc)
        yc, sfin = _rwkv_scan(*_relayout_ctx(prep_c, nbc, tc_len), zero_rw, tc=min(32, tc_len), folded=False)
        ys, _ = _rwkv_scan(*_relayout_smp(prep_s, nbs, ts_len, 0), _rwkv_state_smp(state_rwkv[:, l], nbs),
                           tc=min(64, ts_len // 2), folded=True)
        yrw = (_unmix_ctx(yc, nbc, tc_len), _unmix_smp(ys, nbs, ts_len))
        rw_out.append(_rwkv_state_out_ctx(sfin, nbc))

        x = _mix_out(x, pat, u, gs, gr, (y5c, y5s), yrw, (prep_c[3], prep_s[3]), (prep_c[4], prep_s[4]), p, rc)
        x = _moe(x, pat, w_router, b_router, wg_bf16, wu_bf16, wd_bf16, l, ln2_g[l], ln2_b[l], rc)

    y_prompt = _from_time_major(x, nbc, tc_len, 0)
    y_sample = _from_time_major(x, nbs, ts_len, rc)
    return (y_prompt, y_sample, jnp.stack(s5_out, 1), jnp.stack(rw_out, 1))
```

```python
import functools

import numpy as np
import jax
import jax.numpy as jnp
from jax import lax
from jax.experimental import pallas as pl
from jax.experimental.pallas import tpu as pltpu

D_MODEL = 1024
DEPTH = 2
GRID_W = 64
D_S5 = 512
S5_GROUP = 16
G_S5 = 32
N_S5 = 64
S5_STATE = G_S5 * N_S5
D_RWKV = 512
RWKV_HEAD = 64
H_RWKV = 8
LORA_W = 64
LORA_A = 64
LORA_G = 128
RWKV_COLS = 3 * D_RWKV + LORA_W + LORA_A + LORA_G
N_EXPERTS = 16
N_GROUPS = 4
EXPERTS_PER_GROUP = 4
D_EXPERT = 512
ALPHA = (2 * DEPTH) ** 0.25
LN_EPS = 1e-5
GN_EPS = 64e-5

SUBLANES = 8
LANES = 128
VMEM_LIMIT = 56 * 1024 * 1024

F32 = jnp.float32
BF16 = jnp.bfloat16


def _params(sem):
    return pltpu.CompilerParams(dimension_semantics=sem, vmem_limit_bytes=VMEM_LIMIT)


def _dot(a, b):
    return jnp.dot(a.astype(BF16), b.astype(BF16), preferred_element_type=F32)


def _dot_split(x, w_exact):
    hi = x.astype(BF16)
    lo = (x - hi.astype(F32)).astype(BF16)
    return (jnp.dot(hi, w_exact, preferred_element_type=F32)
            + jnp.dot(lo, w_exact, preferred_element_type=F32))


def _sigmoid(x):
    return 1.0 / (1.0 + jnp.exp(-x))


def _silu(x):
    return x * _sigmoid(x)


def _layer_norm(x, g, b):
    mu = jnp.mean(x, -1, keepdims=True)
    xc = x - mu
    var = jnp.mean(xc * xc, -1, keepdims=True)
    return xc * lax.rsqrt(var + LN_EPS) * g + b


def _modulate(x, shift8, scale8):
    rows, d = x.shape
    x3 = x.reshape(rows // SUBLANES, SUBLANES, d)
    return (x3 * (1.0 + scale8)[None] + shift8[None]).reshape(rows, d)


def _to_time_major_kernel(xc_ref, xs_ref, o_ref, *, ctx_tiles):
    i = pl.program_id(0)

    @pl.when(i < ctx_tiles)
    def _():
        nb, steps, _ = xc_ref.shape
        for t in range(steps):
            o_ref[pl.ds(t * nb, nb), :] = xc_ref[:, t, :]

    @pl.when(i >= ctx_tiles)
    def _():
        nb, steps, d = xs_ref.shape
        rows = nb * steps
        r = lax.broadcasted_iota(jnp.int32, (rows, rows), 0)
        c = lax.broadcasted_iota(jnp.int32, (rows, rows), 1)
        perm = ((r % nb) * steps + r // nb == c).astype(BF16)
        o_ref[...] = _permute_rows(perm, xs_ref[...].reshape(rows, d))


def _permute_rows(perm, x):
    hi = x.astype(BF16)
    r1 = x - hi.astype(F32)
    mid = r1.astype(BF16)
    lo = (r1 - mid.astype(F32)).astype(BF16)
    return (jnp.dot(perm, hi, preferred_element_type=F32) + jnp.dot(perm, mid, preferred_element_type=F32)
            + jnp.dot(perm, lo, preferred_element_type=F32))


def _to_time_major(x_ctx, x_smp, tm=256):
    nbc, tcl, d = x_ctx.shape
    nbs, tsl, _ = x_smp.shape
    ct = nbc * tcl // tm
    st = nbs * tsl // tm
    return pl.pallas_call(
        functools.partial(_to_time_major_kernel, ctx_tiles=ct),
        grid=(ct + st,),
        in_specs=[pl.BlockSpec((nbc, tm // nbc, d), lambda i: (0, jnp.minimum(i, ct - 1), 0)),
                  pl.BlockSpec((nbs, tm // nbs, d), lambda i: (0, jnp.maximum(i - ct, 0), 0))],
        out_specs=pl.BlockSpec((tm, d), lambda i: (i, 0)),
        out_shape=jax.ShapeDtypeStruct((nbc * tcl + nbs * tsl, d), F32),
        compiler_params=_params(("arbitrary",)),
        name="to_time_major",
    )(x_ctx, x_smp)


def _from_time_major_kernel(x_ref, o_ref):
    nb, steps, _ = o_ref.shape
    if nb >= SUBLANES:
        for t in range(steps):
            o_ref[:, t, :] = x_ref[pl.ds(t * nb, nb), :]
    else:
        rows = nb * steps
        r = lax.broadcasted_iota(jnp.int32, (rows, rows), 0)
        c = lax.broadcasted_iota(jnp.int32, (rows, rows), 1)
        perm = ((r % steps) * nb + r // steps == c).astype(BF16)
        o_ref[...] = _permute_rows(perm, x_ref[...]).reshape(o_ref.shape)


def _from_time_major(x, nb, seq, row0, tm=256):
    d = x.shape[1]
    t0 = row0 // tm
    return pl.pallas_call(
        _from_time_major_kernel,
        grid=(seq * nb // tm,),
        in_specs=[pl.BlockSpec((tm, d), lambda i: (t0 + i, 0))],
        out_specs=pl.BlockSpec((nb, tm // nb, d), lambda i: (0, i, 0)),
        out_shape=jax.ShapeDtypeStruct((nb, seq, d), F32),
        compiler_params=_params(("arbitrary",)),
        name="from_time_major",
    )(x)


def _ada_kernel(cond_ref, w_ref, b_ref, o_ref):
    c = cond_ref[...]
    o_ref[0] = _dot(_silu(c), w_ref[0]) + b_ref[0]


def _ada_mod(cond8, w_ada, b_ada):
    nl = w_ada.shape[0]
    d = D_MODEL
    return pl.pallas_call(
        _ada_kernel,
        grid=(nl, 6),
        in_specs=[pl.BlockSpec((SUBLANES, d), lambda l, k: (0, 0)),
                  pl.BlockSpec((1, d, d), lambda l, k: (l, 0, k)),
                  pl.BlockSpec((1, 1, d), lambda l, k: (l, 0, k))],
        out_specs=pl.BlockSpec((1, SUBLANES, d), lambda l, k: (l, 0, k)),
        out_shape=jax.ShapeDtypeStruct((nl, SUBLANES, 6 * d), F32),
        compiler_params=_params(("arbitrary", "arbitrary")),
        name="ada_mod",
    )(cond8, w_ada, b_ada.reshape(nl, 1, 6 * d))


IN_SPLITS = (0, D_S5, D_S5 + RWKV_COLS, D_S5 + RWKV_COLS + D_MODEL, D_S5 + RWKV_COLS + 2 * D_MODEL)


def _inproj_kernel(x_ref, pat_ref, w_ref, u_ref, zr_ref, gs_ref, gr_ref):
    h = _modulate(x_ref[...], pat_ref[0, 0], pat_ref[0, 1]).astype(BF16)
    for k, out in enumerate((u_ref, zr_ref, gs_ref, gr_ref)):
        out[...] = jnp.dot(h, w_ref[0, :, IN_SPLITS[k]:IN_SPLITS[k + 1]], preferred_element_type=F32)


def _inproj(x, pat, w_in_bf16, layer, rows_per_trunk, tm=256):
    rows = x.shape[0]
    d = D_MODEL
    tpt = rows_per_trunk // tm
    widths = [IN_SPLITS[k + 1] - IN_SPLITS[k] for k in range(4)]
    rowblk = lambda n: pl.BlockSpec((tm, n), lambda i: (i, 0))
    return pl.pallas_call(
        _inproj_kernel,
        grid=(rows // tm,),
        in_specs=[rowblk(d),
                  pl.BlockSpec((1, 6, SUBLANES, d), lambda i: (i // tpt, 0, 0, 0)),
                  pl.BlockSpec((1, d, IN_SPLITS[-1]), lambda i: (layer, 0, 0))],
        out_specs=[rowblk(n) for n in widths],
        out_shape=[jax.ShapeDtypeStruct((rows, n), F32) for n in widths],
        compiler_params=_params(("arbitrary",)),
        name="in_proj",
    )(x, pat, w_in_bf16)


def _s5_disc_kernel(are_ref, aim_ref, ldt_ref, bre_ref, bim_ref, cre_ref, cim_ref,
                    abre_ref, abim_ref, wbre_ref, wbim_ref, wc_ref):
    a_re = jnp.minimum(are_ref[0], -1e-4)
    a_im = aim_ref[0]
    dt = jnp.exp(ldt_ref[0])
    mag = jnp.exp(a_re * dt)
    ab_re = mag * jnp.cos(a_im * dt)
    ab_im = mag * jnp.sin(a_im * dt)
    den = a_re * a_re + a_im * a_im
    nr = ab_re - 1.0
    q_re = (nr * a_re + ab_im * a_im) / den
    q_im = (ab_im * a_re - nr * a_im) / den
    abre_ref[0] = ab_re
    abim_ref[0] = ab_im
    b_re = bre_ref[0]
    b_im = bim_ref[0]
    bb_re = (q_re[:, None, :] * b_re - q_im[:, None, :] * b_im).reshape(D_S5, N_S5)
    bb_im = (q_re[:, None, :] * b_im + q_im[:, None, :] * b_re).reshape(D_S5, N_S5)

    def spread(x, n_rep, row_group, col_group):
        rows, w = x.shape
        tiled_eye = (lax.broadcasted_iota(jnp.int32, (w, n_rep * w), 0)
                     == lax.broadcasted_iota(jnp.int32, (w, n_rep * w), 1) % w).astype(BF16)
        rep = jnp.dot(x.astype(BF16), tiled_eye, preferred_element_type=F32)
        keep = (lax.broadcasted_iota(jnp.int32, rep.shape, 0) // row_group
                == lax.broadcasted_iota(jnp.int32, rep.shape, 1) // col_group)
        return jnp.where(keep, rep, 0.0).astype(BF16)

    wbre_ref[0] = spread(bb_re, G_S5, S5_GROUP, N_S5)
    wbim_ref[0] = spread(bb_im, G_S5, S5_GROUP, N_S5)
    wc_ref[0, :S5_STATE] = spread(cre_ref[0], G_S5, N_S5, S5_GROUP)
    wc_ref[0, S5_STATE:] = spread(-cim_ref[0], G_S5, N_S5, S5_GROUP)


def _s5_weights(a_re, a_im, log_dt, b_re, b_im, c_re, c_im):
    m = a_re.shape[0]
    s = S5_STATE
    blk = lambda *shape: pl.BlockSpec((1,) + shape, lambda i: (i,) + (0,) * len(shape))
    return pl.pallas_call(
        _s5_disc_kernel,
        grid=(m,),
        in_specs=[blk(G_S5, N_S5), blk(G_S5, N_S5), blk(G_S5, 1), blk(G_S5, S5_GROUP, N_S5),
                  blk(G_S5, S5_GROUP, N_S5), blk(s, S5_GROUP), blk(s, S5_GROUP)],
        out_specs=[blk(G_S5, N_S5), blk(G_S5, N_S5), blk(D_S5, s), blk(D_S5, s), blk(2 * s, D_S5)],
        out_shape=[jax.ShapeDtypeStruct((m, G_S5, N_S5), F32), jax.ShapeDtypeStruct((m, G_S5, N_S5), F32),
                   jax.ShapeDtypeStruct((m, D_S5, s), BF16), jax.ShapeDtypeStruct((m, D_S5, s), BF16),
                   jax.ShapeDtypeStruct((m, 2 * s, D_S5), BF16)],
        compiler_params=_params(("arbitrary",)),
        name="s5_weights",
    )(a_re, a_im, log_dt, b_re, b_im, c_re, c_im)


def _s5_scan_kernel(u_ref, wbre_ref, wbim_ref, abre_ref, abim_ref, wc_ref, h0_ref,
                    y_ref, hfin_ref, bure_ref, buim_ref, hre_ref, him_ref, *, nb, lane_w):
    d = pl.program_id(0)
    c = pl.program_id(1)
    nc = pl.num_programs(1)
    rows = u_ref.shape[0]

    @pl.when(c == 0)
    def _():
        hre_ref[...] = h0_ref[0, 0]
        him_ref[...] = h0_ref[0, 1]

    ub = u_ref[...].astype(BF16)
    ct = 2 * LANES
    for j in range(S5_STATE // ct):
        first_channel = j * ct * S5_GROUP // N_S5
        ks = slice(first_channel // ct * ct, first_channel // ct * ct + ct)
        js = slice(ct * j, ct * (j + 1))
        bure_ref[:, js] = jnp.dot(ub[:, ks], wbre_ref[0, ks, js], preferred_element_type=F32)
        buim_ref[:, js] = jnp.dot(ub[:, ks], wbim_ref[0, ks, js], preferred_element_type=F32)

    fwd = d == 0
    if nb >= SUBLANES:
        steps = rows // nb
        for lc in range(S5_STATE // lane_w):
            ls = slice(lc * lane_w, (lc + 1) * lane_w)
            ar = jnp.broadcast_to(abre_ref[0, :, ls], (nb, lane_w))
            ai = jnp.broadcast_to(abim_ref[0, :, ls], (nb, lane_w))

            def body(s, carry, ls=ls, ar=ar, ai=ai):
                hr, hi = carry
                t = jnp.where(fwd, s, steps - 1 - s)
                r0 = pl.multiple_of(t * nb, nb)
                br = bure_ref[pl.ds(r0, nb), ls]
                bi = buim_ref[pl.ds(r0, nb), ls]
                nr = ar * hr - ai * hi + br
                ni = ar * hi + ai * hr + bi
                bure_ref[pl.ds(r0, nb), ls] = nr
                buim_ref[pl.ds(r0, nb), ls] = ni
                return nr, ni

            hr, hi = lax.fori_loop(0, steps, body, (hre_ref[:, ls], him_ref[:, ls]))
            hre_ref[:, ls] = hr
            him_ref[:, ls] = hi
    else:
        per = SUBLANES // nb
        groups = rows // SUBLANES
        row_id = lax.broadcasted_iota(jnp.int32, (SUBLANES, lane_w), 0) // nb
        shift = jnp.where(fwd, nb, SUBLANES - nb)
        for lc in range(S5_STATE // lane_w):
            ls = slice(lc * lane_w, (lc + 1) * lane_w)
            ar = jnp.broadcast_to(abre_ref[0, :, ls], (SUBLANES, lane_w))
            ai = jnp.broadcast_to(abim_ref[0, :, ls], (SUBLANES, lane_w))
            h0r = hre_ref[:, ls]
            h0i = him_ref[:, ls]

            def body(gidx, carry, ls=ls, ar=ar, ai=ai):
                tr, ti = carry
                g = jnp.where(fwd, gidx, groups - 1 - gidx)
                r0 = pl.multiple_of(g * SUBLANES, SUBLANES)
                br = bure_ref[pl.ds(r0, SUBLANES), ls]
                bi = buim_ref[pl.ds(r0, SUBLANES), ls]
                for k in range(per):
                    pr = pltpu.roll(tr, shift, 0)
                    pi = pltpu.roll(ti, shift, 0)
                    nr = ar * pr - ai * pi + br
                    ni = ar * pi + ai * pr + bi
                    tgt = jnp.where(fwd, k, per - 1 - k)
                    sel = row_id == tgt
                    tr = jnp.where(sel, nr, tr)
                    ti = jnp.where(sel, ni, ti)
                bure_ref[pl.ds(r0, SUBLANES), ls] = tr
                buim_ref[pl.ds(r0, SUBLANES), ls] = ti
                return tr, ti

            tr, ti = lax.fori_loop(0, groups, body, (h0r, h0i))
            hre_ref[:, ls] = tr
            him_ref[:, ls] = ti

    for q in range(D_S5 // ct):
        qs = slice(ct * q, ct * (q + 1))
        ss = slice(q * ct * N_S5 // S5_GROUP, (q + 1) * ct * N_S5 // S5_GROUP)
        si = slice(S5_STATE + ss.start, S5_STATE + ss.stop)
        y_ref[0, :, qs] = (jnp.dot(bure_ref[:, ss].astype(BF16), wc_ref[0, ss, qs], preferred_element_type=F32)
                           + jnp.dot(buim_ref[:, ss].astype(BF16), wc_ref[0, si, qs],
                                     preferred_element_type=F32))

    @pl.when(c == nc - 1)
    def _():
        hfin_ref[0, 0] = hre_ref[...]
        hfin_ref[0, 1] = him_ref[...]


def _s5_scan(u, wb_re, wb_im, ab_re, ab_im, wc, h0, nb, row0, rows, chunk_rows=512):
    if nb < SUBLANES:
        h0 = jnp.tile(h0, (1, 1, SUBLANES // nb, 1))
    srows = max(nb, SUBLANES)
    cr = min(chunk_rows, rows)
    nc = rows // cr
    c0 = row0 // cr
    s = S5_STATE
    lane_w = 256 if nb >= SUBLANES else 512
    chunk = lambda d, c: c + d * (nc - 1 - 2 * c)
    y, hfin = pl.pallas_call(
        functools.partial(_s5_scan_kernel, nb=nb, lane_w=lane_w),
        grid=(2, nc),
        in_specs=[pl.BlockSpec((cr, D_S5), lambda d, c: (c0 + chunk(d, c), 0)),
                  pl.BlockSpec((1, D_S5, s), lambda d, c: (d, 0, 0)),
                  pl.BlockSpec((1, D_S5, s), lambda d, c: (d, 0, 0)),
                  pl.BlockSpec((1, 1, s), lambda d, c: (d, 0, 0)),
                  pl.BlockSpec((1, 1, s), lambda d, c: (d, 0, 0)),
                  pl.BlockSpec((1, 2 * s, D_S5), lambda d, c: (d, 0, 0)),
                  pl.BlockSpec((1, 2, srows, s), lambda d, c: (d, 0, 0, 0))],
        out_specs=[pl.BlockSpec((1, cr, D_S5), lambda d, c: (d, chunk(d, c), 0)),
                   pl.BlockSpec((1, 2, srows, s), lambda d, c: (d, 0, 0, 0))],
        out_shape=[jax.ShapeDtypeStruct((2, rows, D_S5), F32),
                   jax.ShapeDtypeStruct((2, 2, srows, s), F32)],
        scratch_shapes=[pltpu.VMEM((cr, s), F32), pltpu.VMEM((cr, s), F32),
                        pltpu.VMEM((srows, s), F32), pltpu.VMEM((srows, s), F32)],
        compiler_params=_params(("arbitrary", "arbitrary")),
        name="s5_scan",
    )(u, wb_re, wb_im, ab_re, ab_im, wc, h0)
    if nb < SUBLANES:
        hfin = jnp.stack([hfin[0, :, srows - nb:], hfin[1, :, :nb]])
    return y, hfin


def _rwkv_prep_kernel(prev_ref, cur_ref, next_ref, mu_ref, ones_ref, g2_ref, kk_ref, ka_ref, rk_ref,
                      w0_ref, w2_ref, a0_ref, a2_ref,
                      r_ref, v_ref, nkk_ref, bonus_ref, g_ref, w_ref, kd_ref, b_ref, *, nb, seq, grid_shift):
    tm = cur_ref.shape[0]
    i = pl.program_id(0)
    cur = cur_ref[...]
    prv = prev_ref[...]
    nxt = next_ref[...]
    row = lax.broadcasted_iota(jnp.int32, (tm, RWKV_COLS), 0)
    lane = lax.broadcasted_iota(jnp.int32, (tm, RWKV_COLS), 1)
    t = (i * tm + row) // nb

    halo = prv.shape[0]

    def rows_before(s):
        if s == tm:
            return prv
        if s == halo:
            return jnp.concatenate([prv, cur[:tm - s]], axis=0)
        return jnp.where(row < s, pltpu.roll(prv, s, 0), pltpu.roll(cur, s, 0))

    def rows_after(s):
        if s == tm:
            return nxt
        if s == halo:
            return jnp.concatenate([cur[s:], nxt], axis=0)
        return jnp.where(row >= tm - s, pltpu.roll(nxt, tm - s, 0), pltpu.roll(cur, tm - s, 0))

    if grid_shift:
        tw = t % GRID_W
        left = jnp.where(tw == 0, 0.0, rows_before(nb))
        right = jnp.where(tw == GRID_W - 1, 0.0, rows_after(nb))
        up = jnp.where(t < GRID_W, 0.0, rows_before(nb * GRID_W))
        down = jnp.where(t >= seq - GRID_W, 0.0, rows_after(nb * GRID_W))
        m4 = lane % 4
        sh = jnp.where(m4 == 0, left, jnp.where(m4 == 1, right, jnp.where(m4 == 2, up, down)))
    else:
        before = jnp.where(t == 0, 0.0, rows_before(nb))
        after = jnp.where(t == seq - 1, 0.0, rows_after(nb))
        sh = jnp.where(lane % 2 == 0, before, after)

    z = cur + (sh - cur) * mu_ref[...]
    r = z[:, 0:D_RWKV]
    k = z[:, D_RWKV:2 * D_RWKV]
    v = z[:, 2 * D_RWKV:3 * D_RWKV]
    xw = z[:, 3 * D_RWKV:3 * D_RWKV + LORA_W]
    xa = z[:, 3 * D_RWKV + LORA_W:3 * D_RWKV + LORA_W + LORA_A]
    xg = z[:, 3 * D_RWKV + LORA_W + LORA_A:]
    ones = ones_ref[...]

    g_ref[...] = _dot(_sigmoid(xg), g2_ref[...])
    kk = k * kk_ref[...]
    nrm = jnp.sqrt(_dot_split(kk * kk, ones))
    kk = kk / jnp.maximum(nrm, 1e-12)
    r_ref[...] = r
    v_ref[...] = v
    nkk_ref[...] = -kk
    bonus_ref[...] = _dot_split(r * k * rk_ref[...], ones) * v
    txw = jnp.tanh(xw)
    for d in range(2):
        zw = -(w0_ref[d] + _dot(txw, w2_ref[d]))
        softplus = jnp.maximum(zw, 0.0) + jnp.log(1.0 + jnp.exp(-jnp.abs(zw)))
        w_log = -softplus - 0.5
        w_ref[d] = jnp.exp(-jnp.exp(w_log))
        a = _sigmoid(a0_ref[d] + _dot(xa, a2_ref[d]))
        kd_ref[d] = k * (1.0 + (a - 1.0) * ka_ref[...])
        b_ref[d] = kk * a


def _rwkv_prep(zr, p, nb, seq, grid_shift, row0, tm=128):
    rows = seq * nb
    nt = rows // tm
    t0 = row0 // tm
    if grid_shift:
        assert tm == nb * GRID_W, "one tile must be one grid row of the latent grid"
    else:
        assert tm % nb == 0 and tm >= nb
    kern = functools.partial(_rwkv_prep_kernel, nb=nb, seq=seq, grid_shift=grid_shift)
    c = D_RWKV
    vec = lambda n: pl.BlockSpec((1, n), lambda i: (0, 0))
    mat = lambda a, b: pl.BlockSpec((a, b), lambda i: (0, 0))
    row_out = pl.BlockSpec((tm, c), lambda i: (i, 0))
    dir_out = pl.BlockSpec((2, tm, c), lambda i: (0, i, 0))
    head_ones = jnp.kron(jnp.eye(H_RWKV, dtype=F32), jnp.ones((RWKV_HEAD, RWKV_HEAD), F32)).astype(BF16)
    halo = tm if grid_shift else nb
    assert halo % SUBLANES == 0
    hpt = tm // halo
    h0 = t0 * hpt
    in_specs = [pl.BlockSpec((halo, RWKV_COLS), lambda i: (h0 + jnp.maximum(i * hpt - 1, 0), 0)),
                pl.BlockSpec((tm, RWKV_COLS), lambda i: (t0 + i, 0)),
                pl.BlockSpec((halo, RWKV_COLS), lambda i: (h0 + jnp.minimum((i + 1) * hpt, nt * hpt - 1), 0)),
                vec(RWKV_COLS), mat(c, c), mat(LORA_G, c), vec(c), vec(c), vec(c),
                pl.BlockSpec((2, 1, c), lambda i: (0, 0, 0)),
                pl.BlockSpec((2, LORA_W, c), lambda i: (0, 0, 0)),
                pl.BlockSpec((2, 1, c), lambda i: (0, 0, 0)),
                pl.BlockSpec((2, LORA_A, c), lambda i: (0, 0, 0))]
    args = [zr, zr, zr, p['rw_mu'].reshape(1, -1), head_ones, p['rw_g2'], p['rw_k_k'].reshape(1, c),
            p['rw_k_a'].reshape(1, c), p['rw_r_k'].reshape(1, c), p['rw_w0'].reshape(2, 1, c), p['rw_w2'],
            p['rw_a0'].reshape(2, 1, c), p['rw_a2']]
    part = jax.ShapeDtypeStruct((rows, c), F32)
    return pl.pallas_call(
        kern,
        grid=(nt,),
        in_specs=in_specs,
        out_specs=[row_out] * 5 + [dir_out] * 3,
        out_shape=[part] * 5 + [jax.ShapeDtypeStruct((2, rows, c), F32)] * 3,
        compiler_params=_params(("arbitrary",)),
        name="rwkv_prep",
    )(*args)


def _rwkv_scan_kernel(a_ref, w_ref, b_ref, k_ref, r_ref, v_ref, s0_ref, y_ref, sfin_ref, s_ref, *,
                      nacc, row_blk, folded):
    d = pl.program_id(0)
    c = pl.program_id(1)
    nc = pl.num_programs(1)
    tc = a_ref.shape[1]
    nj = RWKV_HEAD
    j_unroll = 32 if row_blk >= 32 else nj
    rev = (c >= nc // 2) if folded else (d == 1)

    @pl.when(c == 0)
    def _():
        s_ref[...] = s0_ref[0]

    def step(s, carry):
        t = jnp.where(rev, tc - 1 - s, s)
        zero = jnp.zeros((row_blk, a_ref.shape[3]), F32)
        for rb in range(s_ref.shape[1] // row_blk):
            rows = pl.ds(rb * row_blk, row_blk)

            def pass1(jo, acc):
                acc = list(acc)
                for ji in range(j_unroll):
                    j = jo * j_unroll + ji
                    acc[ji % nacc] = acc[ji % nacc] + s_ref[j, rows, :] * a_ref[0, t, pl.ds(j, 1), :]
                return tuple(acc)

            sa = functools.reduce(lambda x, y: x + y, lax.fori_loop(0, nj // j_unroll, pass1, (zero,) * nacc))
            vv = v_ref[0, t, rows, :]

            def pass2(jo, acc):
                acc = list(acc)
                for ji in range(j_unroll):
                    j = jo * j_unroll + ji
                    s_new = (s_ref[j, rows, :] * w_ref[0, t, pl.ds(j, 1), :]
                             + sa * b_ref[0, t, pl.ds(j, 1), :]
                             + vv * k_ref[0, t, pl.ds(j, 1), :])
                    s_ref[j, rows, :] = s_new
                    acc[ji % nacc] = acc[ji % nacc] + s_new * r_ref[0, t, pl.ds(j, 1), :]
                return tuple(acc)

            y_ref[0, t, rows, :] = functools.reduce(
                lambda x, y: x + y, lax.fori_loop(0, nj // j_unroll, pass2, (zero,) * nacc))
        return carry

    lax.fori_loop(0, tc, step, 0)

    @pl.when(c == nc - 1)
    def _():
        sfin_ref[0] = s_ref[...]


def _rwkv_scan(a, w, b, k, r, v, s0, tc, folded):
    _, tp, _, nl = w.shape
    iv = v.shape[2]
    row_blk = min(iv, 32)
    nacc = 4 if row_blk <= 16 else 2
    nc_half = tp // tc
    if folded:
        nd, nc = 1, 2 * nc_half
        tmap = lambda d, c: (c // nc_half, jnp.where(c < nc_half, c, nc - 1 - c), 0, 0)
        shared = per_dir = tmap
    else:
        nd, nc = 2, nc_half
        chunk = lambda d, c: c + d * (nc - 1 - 2 * c)
        shared = lambda d, c: (0, chunk(d, c), 0, 0)
        per_dir = lambda d, c: (d, chunk(d, c), 0, 0)
    key = lambda m: pl.BlockSpec((1, tc, RWKV_HEAD, nl), m)
    val = lambda m: pl.BlockSpec((1, tc, iv, nl), m)
    sspec = pl.BlockSpec((1, RWKV_HEAD, iv, nl), lambda d, c: (d, 0, 0, 0))
    return pl.pallas_call(
        functools.partial(_rwkv_scan_kernel, nacc=nacc, row_blk=row_blk, folded=folded),
        grid=(nd, nc),
        in_specs=[key(shared), key(per_dir), key(per_dir), key(per_dir), key(shared), val(shared), sspec],
        out_specs=[val(per_dir), sspec],
        out_shape=[jax.ShapeDtypeStruct((2, tp, iv, nl), F32),
                   jax.ShapeDtypeStruct((nd, RWKV_HEAD, iv, nl), F32)],
        scratch_shapes=[pltpu.VMEM((RWKV_HEAD, iv, nl), F32)],
        compiler_params=_params(("arbitrary", "arbitrary")),
        name="rwkv_scan",
    )(a, w, b, k, r, v, s0)


SMP_REP = 4
SMP_ROWS = RWKV_HEAD // SMP_REP


def _block_transpose(tiles, bs):
    tiles = list(tiles)
    lane = lax.broadcasted_iota(jnp.int32, tiles[0].shape, 1)
    for kbit in range(3):
        sft = bs << kbit
        bit = (lane // sft) % 2 == 1
        for r in range(8):
            if (r >> kbit) & 1:
                continue
            r2 = r | (1 << kbit)
            lo, hi = tiles[r], tiles[r2]
            tiles[r] = jnp.where(bit, pltpu.roll(hi, sft, 1), lo)
            tiles[r2] = jnp.where(bit, hi, pltpu.roll(lo, LANES - sft, 1))
    return tiles


def _head_tiles(x):
    xt = x.T
    return [xt[h * RWKV_HEAD:(h + 1) * RWKV_HEAD, :] for h in range(H_RWKV)]


def _from_head_tiles(tiles):
    return jnp.concatenate(tiles, axis=0).T


def _relayout_ctx_kernel(r_ref, v_ref, nkk_ref, w_ref, kd_ref, b_ref, a_o, r_o, v_o, w_o, b_o, k_o, *, nb):
    steps = a_o.shape[1]
    for src, dst in ((nkk_ref, a_o), (r_ref, r_o), (v_ref, v_o)):
        tiles = _block_transpose(_head_tiles(src[...]), nb)
        for t in range(steps):
            dst[0, t] = tiles[t]
    for src, dst in ((w_ref, w_o), (b_ref, b_o), (kd_ref, k_o)):
        for d in range(2):
            tiles = _block_transpose(_head_tiles(src[d]), nb)
            for t in range(steps):
                dst[d, t] = tiles[t]


def _relayout_ctx(prep, nb, seq):
    r, v, nkk, _, _, w, kd, bb = prep
    steps = LANES // nb
    assert steps == H_RWKV and nb * H_RWKV == LANES
    tm = steps * nb
    nt = seq // steps
    c = D_RWKV
    nat = pl.BlockSpec((tm, c), lambda i: (i, 0))
    nat2 = pl.BlockSpec((2, tm, c), lambda i: (0, i, 0))
    out1 = pl.BlockSpec((1, steps, RWKV_HEAD, LANES), lambda i: (0, i, 0, 0))
    out2 = pl.BlockSpec((2, steps, RWKV_HEAD, LANES), lambda i: (0, i, 0, 0))
    s1 = jax.ShapeDtypeStruct((1, seq, RWKV_HEAD, LANES), F32)
    s2 = jax.ShapeDtypeStruct((2, seq, RWKV_HEAD, LANES), F32)
    a, r_, v_, w_, b_, k_ = pl.pallas_call(
        functools.partial(_relayout_ctx_kernel, nb=nb),
        grid=(nt,),
        in_specs=[nat, nat, nat, nat2, nat2, nat2],
        out_specs=[out1, out1, out1, out2, out2, out2],
        out_shape=[s1, s1, s1, s2, s2, s2],
        compiler_params=_params(("arbitrary",)),
        name="rwkv_relayout_ctx",
    )(r, v, nkk, w, kd, bb)
    return a, w_, b_, k_, r_, v_


def _unmix_ctx_kernel(y_ref, o_ref, *, nb):
    steps = y_ref.shape[1]
    tiles = [y_ref[0, t] + y_ref[1, t] for t in range(steps)]
    o_ref[...] = _from_head_tiles(_block_transpose(tiles, nb))


def _unmix_ctx(y, nb, seq):
    steps = LANES // nb
    tm = steps * nb
    return pl.pallas_call(
        functools.partial(_unmix_ctx_kernel, nb=nb),
        grid=(seq // steps,),
        in_specs=[pl.BlockSpec((2, steps, RWKV_HEAD, LANES), lambda i: (0, i, 0, 0))],
        out_specs=pl.BlockSpec((tm, D_RWKV), lambda i: (i, 0)),
        out_shape=jax.ShapeDtypeStruct((seq * nb, D_RWKV), F32),
        compiler_params=_params(("arbitrary",)),
        name="rwkv_unmix_ctx",
    )(y)


def _relayout_smp_kernel(ra, rb, va, vb, na, nb_, wa, wb, ka, kb, ba, bb, sel_ref,
                         a_o, r_o, v_o, w_o, b_o, k_o):
    steps = a_o.shape[1]
    lane_v = lax.broadcasted_iota(jnp.int32, (SMP_ROWS, LANES), 1)

    def conv(xa, xb):
        tiles = _block_transpose(_head_tiles(jnp.concatenate([xa, xb], axis=0)), 2)
        parts = []
        for x in tiles:
            hi = x.astype(BF16)
            r1 = x - hi.astype(F32)
            mid = r1.astype(BF16)
            lo = (r1 - mid.astype(F32)).astype(BF16)
            parts.append((hi, mid, lo))
        return parts

    def expand(o0, o1, half):
        order0 = range(8) if half == 0 else range(7, -1, -1)
        order1 = range(7, -1, -1) if half == 0 else range(8)
        cols = []
        for term in range(3):
            cols.append(jnp.concatenate([o0[s][term] for s in order0], axis=0))
            cols.append(jnp.concatenate([o1[s][term] for s in order1], axis=0))
        lhs = jnp.concatenate(cols, axis=1)
        out = [None] * steps
        for pair in range(steps // 16):
            res = jnp.dot(lhs, sel_ref[half, pair], preferred_element_type=F32)
            for q in range(2):
                for s_lo in range(8):
                    out[8 * (2 * pair + q) + s_lo] = res[64 * s_lo:64 * (s_lo + 1), LANES * q:LANES * (q + 1)]
        return out

    def value_rows(full):
        blk = lane_v // (LANES // SMP_REP)
        rows = [full[k * SMP_ROWS:(k + 1) * SMP_ROWS] for k in range(SMP_REP)]
        return jnp.where(blk == 0, rows[0], jnp.where(blk == 1, rows[1], jnp.where(blk == 2, rows[2], rows[3])))

    for srca, srcb, dst in ((na, nb_, a_o), (ra, rb, r_o), (va, vb, v_o)):
        o = conv(srca[...], srcb[...])
        for half in range(2):
            tiles = expand(o, o, half)
            for s in range(steps):
                dst[half, s] = value_rows(tiles[s]) if dst is v_o else tiles[s]
    for srca, srcb, dst in ((wa, wb, w_o), (ba, bb, b_o), (ka, kb, k_o)):
        o0 = conv(srca[0], srcb[0])
        o1 = conv(srca[1], srcb[1])
        for half in range(2):
            tiles = expand(o0, o1, half)
            for s in range(steps):
                dst[half, s] = tiles[s]


def _smp_selection():
    sel = np.zeros((2, 2, 6, LANES, 2 * LANES), np.float32)
    for half in range(2):
        for pair in range(2):
            for q in range(2):
                s_hi = 2 * pair + q
                grp_a = 0 * 64 + 16 * s_hi
                grp_b = 1 * 64 + 16 * (3 - s_hi)
                src = (grp_a, grp_b) if half == 0 else (grp_b, grp_a)
                for d in range(2):
                    for blk in range(SMP_REP):
                        for hb in range(16):
                            col = q * LANES + blk * 32 + d * 16 + hb
                            for term in range(3):
                                sel[half, pair, 2 * term + d, src[d] + hb, col] = 1.0
    return jnp.asarray(sel.reshape(2, 2, 6 * LANES, 2 * LANES), BF16)


def _relayout_smp(prep, nb, seq, row0):
    r, v, nkk, _, _, w, kd, bb = prep
    assert 2 * nb * H_RWKV * SMP_REP == LANES
    steps = 32
    tm = steps * nb
    nt = seq // steps
    half_t = seq // 2
    c = D_RWKV
    t0 = row0 // tm
    blk_a = pl.BlockSpec((tm, c), lambda g: (t0 + g, 0))
    blk_b = pl.BlockSpec((tm, c), lambda g: (t0 + nt - 1 - g, 0))
    blk2_a = pl.BlockSpec((2, tm, c), lambda g: (0, t0 + g, 0))
    blk2_b = pl.BlockSpec((2, tm, c), lambda g: (0, t0 + nt - 1 - g, 0))
    okey = pl.BlockSpec((2, steps, RWKV_HEAD, LANES), lambda g: (0, g, 0, 0))
    oval = pl.BlockSpec((2, steps, SMP_ROWS, LANES), lambda g: (0, g, 0, 0))
    skey = jax.ShapeDtypeStruct((2, half_t, RWKV_HEAD, LANES), F32)
    sval = jax.ShapeDtypeStruct((2, half_t, SMP_ROWS, LANES), F32)
    a, r_, v_, w_, b_, k_ = pl.pallas_call(
        _relayout_smp_kernel,
        grid=(nt // 2,),
        in_specs=[blk_a, blk_b, blk_a, blk_b, blk_a, blk_b, blk2_a, blk2_b, blk2_a, blk2_b, blk2_a, blk2_b,
                  pl.BlockSpec((2, 2, 6 * LANES, 2 * LANES), lambda g: (0, 0, 0, 0))],
        out_specs=[okey, okey, oval, okey, okey, okey],
        out_shape=[skey, skey, sval, skey, skey, skey],
        compiler_params=_params(("arbitrary",)),
        name="rwkv_relayout_smp",
    )(r, r, v, v, nkk, nkk, w, w, kd, kd, bb, bb, _smp_selection())
    return a, w_, b_, k_, r_, v_


def _unmix_smp_kernel(y_ref, o_ref, *, half_tiles):
    q = pl.program_id(0)
    steps = y_ref.shape[1]
    low = q < half_tiles
    h0 = jnp.where(low, 0, 1)
    lane = lax.broadcasted_iota(jnp.int32, (SMP_ROWS, LANES), 1)
    tiles = []
    for s_lo in range(8):
        blocks = []
        for blk in range(SMP_REP):
            acc = jnp.zeros((SMP_ROWS, LANES), F32)
            for s_hi in range(steps // 8):
                s = 8 * s_hi + s_lo
                u = jnp.where(low, s, steps - 1 - s)
                d0 = y_ref[h0, u]
                d1 = y_ref[1 - h0, u]
                sh0 = (16 * s_hi - 32 * blk) % LANES
                sh1 = (16 * s_hi - 32 * blk - 16) % LANES
                t0 = d0 if sh0 == 0 else pltpu.roll(d0, sh0, 1)
                t1 = d1 if sh1 == 0 else pltpu.roll(d1, sh1, 1)
                acc = jnp.where(lane // 16 == s_hi, t0 + t1, acc)
            blocks.append(acc)
        tiles.append(jnp.concatenate(blocks, axis=0))
    o_ref[...] = _from_head_tiles(_block_transpose(tiles, 2))


def _unmix_smp(y, nb, seq):
    steps = 64
    tm = steps * nb
    assert tm == LANES
    nq = seq // steps
    half_tiles = nq // 2
    return pl.pallas_call(
        functools.partial(_unmix_smp_kernel, half_tiles=half_tiles),
        grid=(nq,),
        in_specs=[pl.BlockSpec((2, steps, SMP_ROWS, LANES),
                               lambda q: (0, jnp.where(q < half_tiles, q, nq - 1 - q), 0, 0))],
        out_specs=pl.BlockSpec((tm, D_RWKV), lambda q: (q, 0)),
        out_shape=jax.ShapeDtypeStruct((seq * nb, D_RWKV), F32),
        compiler_params=_params(("arbitrary",)),
        name="rwkv_unmix_smp",
    )(y)


def _rwkv_state_smp(state_l, nb):
    s = state_l.reshape(nb, 2, H_RWKV, SMP_REP, SMP_ROWS, RWKV_HEAD)
    s = s.transpose(5, 4, 3, 1, 2, 0)
    return s.reshape(1, RWKV_HEAD, SMP_ROWS, LANES)


def _rwkv_state_out_ctx(sfin, nb):
    s = sfin.reshape(2, RWKV_HEAD, RWKV_HEAD, H_RWKV, nb)
    return s.transpose(4, 0, 3, 2, 1)


def _mix_out_kernel(x_ref, pat_ref, u_ref, gs_ref, gr_ref,
                    y5c_ref, y5s_ref, yrwc_ref, yrws_ref, bonc_ref, bons_ref, gc_ref, gs2_ref,
                    d_ref, wglu_ref, ones_ref, lnxw_ref, lnxb_ref, wups_ref, wupr_ref, wout_ref,
                    ln1g_ref, ln1b_ref, o_ref, *, ctx_tiles):
    is_ctx = pl.program_id(0) < ctx_tiles
    pick = lambda a, b: jnp.where(is_ctx, a, b)
    y5 = u_ref[...] * d_ref[...] + pick(y5c_ref[0] + y5c_ref[1], y5s_ref[0] + y5s_ref[1])
    y5 = jax.nn.gelu(y5)
    y5 = y5 * _sigmoid(_dot(y5, wglu_ref[...]))
    ones = ones_ref[...]
    yr = pick(yrwc_ref[...], yrws_ref[...])
    inv_n = 1.0 / RWKV_HEAD
    mu = _dot_split(yr, ones) * inv_n
    yc = yr - mu
    var = _dot_split(yc * yc, ones) * inv_n
    yr = yc * lax.rsqrt(var + GN_EPS) * lnxw_ref[...] + lnxb_ref[...]
    yr = (yr + pick(bonc_ref[...], bons_ref[...])) * pick(gc_ref[...], gs2_ref[...])
    merged = (_sigmoid(gs_ref[...]) * _dot(y5, wups_ref[...])
              + _sigmoid(gr_ref[...]) * _dot(yr, wupr_ref[...]))
    mix = _dot(merged, wout_ref[...])
    x3 = _modulate(mix, jnp.zeros_like(pat_ref[0, 2]), pat_ref[0, 2] - 1.0)
    o_ref[...] = _layer_norm(ALPHA * x_ref[...] + x3, ln1g_ref[...], ln1b_ref[...])


def _mix_out(x, pat, u, gs, gr, y5, yrw, bonus, g, p, rows_per_trunk, tm=256):
    rows = x.shape[0]
    d = D_MODEL
    c = D_RWKV
    tpt = rows_per_trunk // tm
    rowblk = lambda n: pl.BlockSpec((tm, n), lambda i: (i, 0))
    ctx_i = lambda i: jnp.minimum(i, tpt - 1)
    smp_i = lambda i: jnp.maximum(i - tpt, 0)
    pair = lambda spec_of: [spec_of(ctx_i), spec_of(smp_i)]
    row_t = lambda f: pl.BlockSpec((tm, c), lambda i: (f(i), 0))
    dir_t = lambda f: pl.BlockSpec((2, tm, c), lambda i: (0, f(i), 0))
    vec = lambda n: pl.BlockSpec((1, n), lambda i: (0, 0))
    mat = lambda a, b: pl.BlockSpec((a, b), lambda i: (0, 0))
    head_ones = jnp.kron(jnp.eye(H_RWKV, dtype=F32), jnp.ones((RWKV_HEAD, RWKV_HEAD), F32)).astype(BF16)
    return pl.pallas_call(
        functools.partial(_mix_out_kernel, ctx_tiles=tpt),
        grid=(rows // tm,),
        in_specs=[rowblk(d),
                  pl.BlockSpec((1, 6, SUBLANES, d), lambda i: (i // tpt, 0, 0, 0)),
                  rowblk(c), rowblk(d), rowblk(d)]
                 + pair(dir_t) + pair(row_t) + pair(row_t) + pair(row_t)
                 + [vec(c), mat(c, c), mat(c, c), vec(c), vec(c), mat(c, d), mat(c, d), mat(d, d),
                    vec(d), vec(d)],
        out_specs=rowblk(d),
        out_shape=jax.ShapeDtypeStruct((rows, d), F32),
        compiler_params=_params(("arbitrary",)),
        name="mix_out",
    )(x, pat, u, gs, gr, *y5, *yrw, *bonus, *g,
      p['s5_d'].reshape(1, c), p['s5_w_glu'].astype(BF16), head_ones,
      p['rw_lnx_w'].reshape(1, c), p['rw_lnx_b'].reshape(1, c),
      p['w_up_s5'].astype(BF16), p['w_up_rwkv'].astype(BF16), p['w_out'].astype(BF16),
      p['ln1_g'].reshape(1, d), p['ln1_b'].reshape(1, d))


def _first_max(x, lane, valid):
    xm = jnp.where(valid, x, -jnp.inf)
    m = jnp.max(xm, -1, keepdims=True)
    idx = jnp.min(jnp.where(xm == m, lane, float(N_EXPERTS)), -1, keepdims=True)
    return m, idx


def _moe_kernel(x_ref, pat_ref, wr_ref, br_ref, wg_ref, wu_ref, wd_ref, ln2g_ref, ln2b_ref, o_ref,
                hp_ref, cp_ref, acc_ref, pos_ref, bounds_ref):
    e = pl.program_id(1)
    ne = pl.num_programs(1)
    tm = x_ref.shape[0]

    @pl.when(e == 0)
    def _():
        h = _modulate(x_ref[...], pat_ref[0, 3], pat_ref[0, 4])
        wr = wr_ref[...]
        w_hi = wr.astype(BF16)
        w_lo = (wr - w_hi.astype(F32)).astype(BF16)
        h_hi = h.astype(BF16)
        h_lo = (h - h_hi.astype(F32)).astype(BF16)
        logits = (jnp.dot(h_hi, w_hi, preferred_element_type=F32)
                  + jnp.dot(h_hi, w_lo, preferred_element_type=F32)
                  + jnp.dot(h_lo, w_hi, preferred_element_type=F32)) + br_ref[...]
        logits = logits - jnp.max(logits, -1, keepdims=True)
        ex = jnp.exp(logits)
        probs = ex / jnp.sum(ex, -1, keepdims=True)
        lane_i = lax.broadcasted_iota(jnp.int32, (tm, N_EXPERTS), 1)
        lane = lane_i.astype(F32)
        grp = (lane_i // EXPERTS_PER_GROUP).astype(F32)
        best_score = jnp.full((tm, 1), -jnp.inf, F32)
        best_grp = jnp.zeros((tm, 1), F32)
        for gi in range(N_GROUPS):
            in_g = grp == float(gi)
            m1, i1 = _first_max(probs, lane, in_g)
            m2, _ = _first_max(probs, lane, in_g & (lane != i1))
            score = m1 + m2
            better = score > best_score
            best_score = jnp.where(better, score, best_score)
            best_grp = jnp.where(better, float(gi), best_grp)
        in_best = grp == best_grp
        m1, i1 = _first_max(probs, lane, in_best)
        m2, i2 = _first_max(probs, lane, in_best & (lane != i1))
        tot = m1 + m2
        comb = jnp.where(lane == i1, m1 / tot, 0.0) + jnp.where(lane == i2, m2 / tot, 0.0)

        glane = lax.broadcasted_iota(jnp.int32, (tm, LANES), 1).astype(F32)
        onehot = (glane == best_grp).astype(BF16)
        r_id = lax.broadcasted_iota(jnp.int32, (tm, tm), 0)
        c_id = lax.broadcasted_iota(jnp.int32, (tm, tm), 1)
        prefix = jnp.dot((r_id >= c_id).astype(BF16), onehot, preferred_element_type=F32)
        counts = prefix[tm - 1:tm, :]
        lane1 = lax.broadcasted_iota(jnp.int32, (1, LANES), 1)
        start = jnp.zeros((1, LANES), F32)
        run = jnp.zeros((1, 1), F32)
        bounds_ref[0] = 0
        for gi in range(N_GROUPS):
            start = start + jnp.where(lane1 == gi, run, 0.0)
            run = run + jnp.sum(jnp.where(lane1 == gi, counts, 0.0), -1, keepdims=True)
            bounds_ref[gi + 1] = run[0, 0].astype(jnp.int32)
        pos = jnp.sum(onehot.astype(F32) * (start + prefix - 1.0), -1, keepdims=True)
        pos_row = jnp.transpose(jnp.broadcast_to(pos, (tm, LANES)))[0:1, :]
        perm = (r_id.astype(F32) == pos_row).astype(BF16)
        hp_ref[...] = jnp.dot(perm, h.astype(BF16), preferred_element_type=F32).astype(BF16)
        cp_ref[...] = _permute_rows(perm, comb)
        pos_ref[...] = pos
        acc_ref[...] = jnp.zeros_like(acc_ref)

    wg = wg_ref[0, 0].astype(BF16)
    wu = wu_ref[0, 0].astype(BF16)
    wd = wd_ref[0, 0].astype(BF16)
    sub = min(tm, MOE_SUB_ROWS)
    grp_first = bounds_ref[e // EXPERTS_PER_GROUP]
    grp_end = bounds_ref[e // EXPERTS_PER_GROUP + 1]

    for rb in range(tm // sub):
        @pl.when((grp_first < (rb + 1) * sub) & (grp_end > rb * sub))
        def _(rb=rb):
            rs = pl.ds(rb * sub, sub)
            hb = hp_ref[rs, :]
            lane = lax.broadcasted_iota(jnp.int32, (sub, N_EXPERTS), 1)
            ce = jnp.sum(jnp.where(lane == e, cp_ref[rs, :], 0.0), -1, keepdims=True)
            hid = (_silu(jnp.dot(hb, wg, preferred_element_type=F32))
                   * jnp.dot(hb, wu, preferred_element_type=F32))
            acc_ref[rs, :] += jnp.dot((hid * ce).astype(BF16), wd, preferred_element_type=F32)

    @pl.when(e == ne - 1)
    def _():
        c_lane = lax.broadcasted_iota(jnp.int32, (tm, tm), 1).astype(F32)
        unperm = (c_lane == pos_ref[...]).astype(BF16)
        ffn = _permute_rows(unperm, acc_ref[...])
        ffn = _modulate(ffn, jnp.zeros_like(pat_ref[0, 5]), pat_ref[0, 5] - 1.0)
        o_ref[...] = _layer_norm(ALPHA * x_ref[...] + ffn, ln2g_ref[...], ln2b_ref[...])


MOE_SUB_ROWS = 256


def _moe(x, pat, w_router, b_router, wg, wu, wd, layer, ln2_g, ln2_b, rows_per_trunk, tm=1024):
    rows = x.shape[0]
    tm = min(tm, rows_per_trunk)
    d = D_MODEL
    tpt = rows_per_trunk // tm
    rowblk = pl.BlockSpec((tm, d), lambda i, e: (i, 0))
    vec = lambda n: pl.BlockSpec((1, n), lambda i, e: (0, 0))
    return pl.pallas_call(
        _moe_kernel,
        grid=(rows // tm, N_EXPERTS),
        in_specs=[rowblk,
                  pl.BlockSpec((1, 6, SUBLANES, d), lambda i, e: (i // tpt, 0, 0, 0)),
                  pl.BlockSpec((d, N_EXPERTS), lambda i, e: (0, 0)), vec(N_EXPERTS),
                  pl.BlockSpec((1, 1, d, D_EXPERT), lambda i, e: (layer, e, 0, 0)),
                  pl.BlockSpec((1, 1, d, D_EXPERT), lambda i, e: (layer, e, 0, 0)),
                  pl.BlockSpec((1, 1, D_EXPERT, d), lambda i, e: (layer, e, 0, 0)),
                  vec(d), vec(d)],
        out_specs=rowblk,
        out_shape=jax.ShapeDtypeStruct((rows, d), F32),
        scratch_shapes=[pltpu.VMEM((tm, d), BF16), pltpu.VMEM((tm, N_EXPERTS), F32),
                        pltpu.VMEM((tm, d), F32), pltpu.VMEM((tm, 1), F32),
                        pltpu.SMEM((SUBLANES,), jnp.int32)],
        compiler_params=_params(("arbitrary", "arbitrary")),
        name="moe",
    )(x, pat, w_router, b_router.reshape(1, N_EXPERTS), wg, wu, wd,
      ln2_g.reshape(1, d), ln2_b.reshape(1, d))


def kernel(x_prompt, x_sample, state_s5, state_rwkv, c, c_ctx, w_ada, b_ada, w_in, s5_a_re, s5_a_im, s5_log_dt, s5_b_re, s5_b_im, s5_c_re, s5_c_im, s5_d, s5_w_glu, rw_mu, rw_w0, rw_w2, rw_a0, rw_a2, rw_g2, rw_k_k, rw_k_a, rw_r_k, rw_lnx_w, rw_lnx_b, w_up_s5, w_up_rwkv, w_out, ln1_g, ln1_b, ln2_g, ln2_b, w_router, b_router, w_exp_gate, w_exp_up, w_exp_down):
    nbc, tc_len, d = x_prompt.shape
    nbs, ts_len, _ = x_sample.shape
    nl = w_ada.shape[0]
    rc = nbc * tc_len
    rs = nbs * ts_len
    assert rc == rs, "both trunks are processed as equal halves of one row-major token matrix"
    assert SUBLANES % nbs == 0 and nbc % SUBLANES == 0

    x = _to_time_major(x_prompt, x_sample)

    cond = jnp.concatenate([c_ctx[None], c], axis=0)
    cond8 = jnp.zeros((SUBLANES, d), F32).at[:cond.shape[0]].set(cond)
    mod = _ada_mod(cond8, w_ada, b_ada).reshape(nl, SUBLANES, 6, d)
    ctx_rows = jnp.zeros((SUBLANES,), jnp.int32)
    smp_rows = 1 + jnp.arange(SUBLANES, dtype=jnp.int32) % nbs
    pat_idx = jnp.stack([ctx_rows, smp_rows])
    pats = mod[:, pat_idx]
    pats = pats.transpose(0, 1, 3, 2, 4)

    m = nl * 2
    ab_re, ab_im, wb_re, wb_im, wc = _s5_weights(
        s5_a_re.reshape(m, G_S5, N_S5), s5_a_im.reshape(m, G_S5, N_S5), s5_log_dt.reshape(m, G_S5, 1),
        s5_b_re.reshape(m, G_S5, N_S5, S5_GROUP).transpose(0, 1, 3, 2),
        s5_b_im.reshape(m, G_S5, N_S5, S5_GROUP).transpose(0, 1, 3, 2),
        s5_c_re.reshape(m, G_S5, S5_GROUP, N_S5).transpose(0, 1, 3, 2).reshape(m, S5_STATE, S5_GROUP),
        s5_c_im.reshape(m, G_S5, S5_GROUP, N_S5).transpose(0, 1, 3, 2).reshape(m, S5_STATE, S5_GROUP))
    ab_re = ab_re.reshape(nl, 2, 1, S5_STATE)
    ab_im = ab_im.reshape(nl, 2, 1, S5_STATE)
    wb_re = wb_re.reshape(nl, 2, D_S5, S5_STATE)
    wb_im = wb_im.reshape(nl, 2, D_S5, S5_STATE)
    wc = wc.reshape(nl, 2, 2 * S5_STATE, D_S5)

    w_in_bf16 = w_in.astype(BF16)
    zero_s5 = jnp.zeros((2, 2, nbc, S5_STATE), F32)
    zero_rw = jnp.zeros((2, RWKV_HEAD, RWKV_HEAD, nbc * H_RWKV), F32)
    s5_out, rw_out = [], []
    for l in range(nl):
        p = dict(s5_d=s5_d[l], s5_w_glu=s5_w_glu[l], rw_mu=rw_mu[l], rw_w0=rw_w0[l], rw_w2=rw_w2[l],
                 rw_a0=rw_a0[l], rw_a2=rw_a2[l], rw_g2=rw_g2[l], rw_k_k=rw_k_k[l], rw_k_a=rw_k_a[l],
                 rw_r_k=rw_r_k[l], rw_lnx_w=rw_lnx_w[l], rw_lnx_b=rw_lnx_b[l], w_up_s5=w_up_s5[l],
                 w_up_rwkv=w_up_rwkv[l], w_out=w_out[l], ln1_g=ln1_g[l], ln1_b=ln1_b[l])
        pat = pats[l]
        u, zr, gs, gr = _inproj(x, pat, w_in_bf16, l, rc)

        s5w = (wb_re[l], wb_im[l], ab_re[l], ab_im[l], wc[l])
        y5c, hfin = _s5_scan(u, *s5w, zero_s5, nbc, 0, rc)
        h0s = state_s5[:, l].reshape(nbs, 2, 2, S5_STATE).transpose(1, 2, 0, 3)
        y5s, _ = _s5_scan(u, *s5w, h0s, nbs, rc, rs)
        s5_out.append(hfin.transpose(2, 0, 1, 3).reshape(nbc, 2, 2, G_S5, N_S5))

        prep_c = _rwkv_prep(zr, p, nbc, tc_len, False, 0)
        prep_s = _rwkv_prep(zr, p, nbs, ts_len, True, rc)
        yc, sfin = _rwkv_scan(*_relayout_ctx(prep_c, nbc, tc_len), zero_rw, tc=min(32, tc_len), folded=False)
        ys, _ = _rwkv_scan(*_relayout_smp(prep_s, nbs, ts_len, 0), _rwkv_state_smp(state_rwkv[:, l], nbs),
                           tc=min(64, ts_len // 2), folded=True)
        yrw = (_unmix_ctx(yc, nbc, tc_len), _unmix_smp(ys, nbs, ts_len))
        rw_out.append(_rwkv_state_out_ctx(sfin, nbc))

        x = _mix_out(x, pat, u, gs, gr, (y5c, y5s), yrw, (prep_c[3], prep_s[3]), (prep_c[4], prep_s[4]), p, rc)
        x = _moe(x, pat, w_router, b_router, w_exp_gate, w_exp_up, w_exp_down, l, ln2_g[l], ln2_b[l], rc)

    y_prompt = _from_time_major(x, nbc, tc_len, 0)
    y_sample = _from_time_major(x, nbs, ts_len, rc)
    return (y_prompt, y_sample, jnp.stack(s5_out, 1), jnp.stack(rw_out, 1))
```

```python
import functools

import numpy as np
import jax
import jax.numpy as jnp
from jax import lax
from jax.experimental import pallas as pl
from jax.experimental.pallas import tpu as pltpu

D_MODEL = 1024
DEPTH = 2
GRID_W = 64
D_S5 = 512
S5_GROUP = 16
G_S5 = 32
N_S5 = 64
S5_STATE = G_S5 * N_S5
D_RWKV = 512
RWKV_HEAD = 64
H_RWKV = 8
LORA_W = 64
LORA_A = 64
LORA_G = 128
RWKV_COLS = 3 * D_RWKV + LORA_W + LORA_A + LORA_G
N_EXPERTS = 16
N_GROUPS = 4
EXPERTS_PER_GROUP = 4
D_EXPERT = 512
ALPHA = (2 * DEPTH) ** 0.25
LN_EPS = 1e-5
GN_EPS = 64e-5

SUBLANES = 8
LANES = 128
VMEM_LIMIT = 56 * 1024 * 1024

F32 = jnp.float32
BF16 = jnp.bfloat16


def _params(sem):
    return pltpu.CompilerParams(dimension_semantics=sem, vmem_limit_bytes=VMEM_LIMIT)


def _dot(a, b):
    return jnp.dot(a.astype(BF16), b.astype(BF16), preferred_element_type=F32)


def _dot_split(x, w_exact):
    hi = x.astype(BF16)
    lo = (x - hi.astype(F32)).astype(BF16)
    return (jnp.dot(hi, w_exact, preferred_element_type=F32)
            + jnp.dot(lo, w_exact, preferred_element_type=F32))


def _sigmoid(x):
    return 1.0 / (1.0 + jnp.exp(-x))


def _silu(x):
    return x * _sigmoid(x)


def _layer_norm(x, g, b):
    mu = jnp.mean(x, -1, keepdims=True)
    xc = x - mu
    var = jnp.mean(xc * xc, -1, keepdims=True)
    return xc * lax.rsqrt(var + LN_EPS) * g + b


def _modulate(x, shift8, scale8):
    rows, d = x.shape
    x3 = x.reshape(rows // SUBLANES, SUBLANES, d)
    return (x3 * (1.0 + scale8)[None] + shift8[None]).reshape(rows, d)


def _to_time_major_kernel(xc_ref, xs_ref, o_ref, *, ctx_tiles):
    i = pl.program_id(0)

    @pl.when(i < ctx_tiles)
    def _():
        nb, steps, _ = xc_ref.shape
        for t in range(steps):
            o_ref[pl.ds(t * nb, nb), :] = xc_ref[:, t, :]

    @pl.when(i >= ctx_tiles)
    def _():
        nb, steps, d = xs_ref.shape
        rows = nb * steps
        r = lax.broadcasted_iota(jnp.int32, (rows, rows), 0)
        c = lax.broadcasted_iota(jnp.int32, (rows, rows), 1)
        perm = ((r % nb) * steps + r // nb == c).astype(BF16)
        o_ref[...] = _permute_rows(perm, xs_ref[...].reshape(rows, d))


def _permute_rows(perm, x):
    hi = x.astype(BF16)
    r1 = x - hi.astype(F32)
    mid = r1.astype(BF16)
    lo = (r1 - mid.astype(F32)).astype(BF16)
    return (jnp.dot(perm, hi, preferred_element_type=F32) + jnp.dot(perm, mid, preferred_element_type=F32)
            + jnp.dot(perm, lo, preferred_element_type=F32))


def _to_time_major(x_ctx, x_smp, tm=256):
    nbc, tcl, d = x_ctx.shape
    nbs, tsl, _ = x_smp.shape
    ct = nbc * tcl // tm
    st = nbs * tsl // tm
    return pl.pallas_call(
        functools.partial(_to_time_major_kernel, ctx_tiles=ct),
        grid=(ct + st,),
        in_specs=[pl.BlockSpec((nbc, tm // nbc, d), lambda i: (0, jnp.minimum(i, ct - 1), 0)),
                  pl.BlockSpec((nbs, tm // nbs, d), lambda i: (0, jnp.maximum(i - ct, 0), 0))],
        out_specs=pl.BlockSpec((tm, d), lambda i: (i, 0)),
        out_shape=jax.ShapeDtypeStruct((nbc * tcl + nbs * tsl, d), F32),
        compiler_params=_params(("arbitrary",)),
        name="to_time_major",
    )(x_ctx, x_smp)


def _from_time_major_kernel(x_ref, o_ref):
    nb, steps, _ = o_ref.shape
    if nb >= SUBLANES:
        for t in range(steps):
            o_ref[:, t, :] = x_ref[pl.ds(t * nb, nb), :]
    else:
        rows = nb * steps
        r = lax.broadcasted_iota(jnp.int32, (rows, rows), 0)
        c = lax.broadcasted_iota(jnp.int32, (rows, rows), 1)
        perm = ((r % steps) * nb + r // steps == c).astype(BF16)
        o_ref[...] = _permute_rows(perm, x_ref[...]).reshape(o_ref.shape)


def _from_time_major(x, nb, seq, row0, tm=256):
    d = x.shape[1]
    t0 = row0 // tm
    return pl.pallas_call(
        _from_time_major_kernel,
        grid=(seq * nb // tm,),
        in_specs=[pl.BlockSpec((tm, d), lambda i: (t0 + i, 0))],
        out_specs=pl.BlockSpec((nb, tm // nb, d), lambda i: (0, i, 0)),
        out_shape=jax.ShapeDtypeStruct((nb, seq, d), F32),
        compiler_params=_params(("arbitrary",)),
        name="from_time_major",
    )(x)


def _ada_kernel(cond_ref, w_ref, b_ref, o_ref):
    c = cond_ref[...]
    o_ref[0] = _dot(_silu(c), w_ref[0]) + b_ref[0]


def _ada_mod(cond8, w_ada, b_ada):
    nl = w_ada.shape[0]
    d = D_MODEL
    return pl.pallas_call(
        _ada_kernel,
        grid=(nl, 6),
        in_specs=[pl.BlockSpec((SUBLANES, d), lambda l, k: (0, 0)),
                  pl.BlockSpec((1, d, d), lambda l, k: (l, 0, k)),
                  pl.BlockSpec((1, 1, d), lambda l, k: (l, 0, k))],
        out_specs=pl.BlockSpec((1, SUBLANES, d), lambda l, k: (l, 0, k)),
        out_shape=jax.ShapeDtypeStruct((nl, SUBLANES, 6 * d), F32),
        compiler_params=_params(("arbitrary", "arbitrary")),
        name="ada_mod",
    )(cond8, w_ada, b_ada.reshape(nl, 1, 6 * d))


IN_SPLITS = (0, D_S5, D_S5 + RWKV_COLS, D_S5 + RWKV_COLS + D_MODEL, D_S5 + RWKV_COLS + 2 * D_MODEL)


def _inproj_kernel(x_ref, pat_ref, w_ref, u_ref, zr_ref, gs_ref, gr_ref):
    h = _modulate(x_ref[...], pat_ref[0, 0], pat_ref[0, 1]).astype(BF16)
    for k, out in enumerate((u_ref, zr_ref, gs_ref, gr_ref)):
        out[...] = jnp.dot(h, w_ref[0, :, IN_SPLITS[k]:IN_SPLITS[k + 1]], preferred_element_type=F32)


def _inproj(x, pat, w_in_bf16, layer, rows_per_trunk, tm=512):
    rows = x.shape[0]
    tm = min(tm, rows_per_trunk)
    d = D_MODEL
    tpt = rows_per_trunk // tm
    widths = [IN_SPLITS[k + 1] - IN_SPLITS[k] for k in range(4)]
    rowblk = lambda n: pl.BlockSpec((tm, n), lambda i: (i, 0))
    return pl.pallas_call(
        _inproj_kernel,
        grid=(rows // tm,),
        in_specs=[rowblk(d),
                  pl.BlockSpec((1, 6, SUBLANES, d), lambda i: (i // tpt, 0, 0, 0)),
                  pl.BlockSpec((1, d, IN_SPLITS[-1]), lambda i: (layer, 0, 0))],
        out_specs=[rowblk(n) for n in widths],
        out_shape=[jax.ShapeDtypeStruct((rows, n), F32) for n in widths],
        compiler_params=_params(("arbitrary",)),
        name="in_proj",
    )(x, pat, w_in_bf16)


def _s5_disc_kernel(are_ref, aim_ref, ldt_ref, bre_ref, bim_ref, cre_ref, cim_ref,
                    abre_ref, abim_ref, wbre_ref, wbim_ref, wc_ref):
    a_re = jnp.minimum(are_ref[0], -1e-4)
    a_im = aim_ref[0]
    dt = jnp.exp(ldt_ref[0])
    mag = jnp.exp(a_re * dt)
    ab_re = mag * jnp.cos(a_im * dt)
    ab_im = mag * jnp.sin(a_im * dt)
    den = a_re * a_re + a_im * a_im
    nr = ab_re - 1.0
    q_re = (nr * a_re + ab_im * a_im) / den
    q_im = (ab_im * a_re - nr * a_im) / den
    abre_ref[0] = ab_re
    abim_ref[0] = ab_im
    b_re = bre_ref[0]
    b_im = bim_ref[0]
    bb_re = (q_re[:, None, :] * b_re - q_im[:, None, :] * b_im).reshape(D_S5, N_S5)
    bb_im = (q_re[:, None, :] * b_im + q_im[:, None, :] * b_re).reshape(D_S5, N_S5)

    def spread(x, n_rep, row_group, col_group):
        rows, w = x.shape
        tiled_eye = (lax.broadcasted_iota(jnp.int32, (w, n_rep * w), 0)
                     == lax.broadcasted_iota(jnp.int32, (w, n_rep * w), 1) % w).astype(BF16)
        rep = jnp.dot(x.astype(BF16), tiled_eye, preferred_element_type=F32)
        keep = (lax.broadcasted_iota(jnp.int32, rep.shape, 0) // row_group
                == lax.broadcasted_iota(jnp.int32, rep.shape, 1) // col_group)
        return jnp.where(keep, rep, 0.0).astype(BF16)

    wbre_ref[0] = spread(bb_re, G_S5, S5_GROUP, N_S5)
    wbim_ref[0] = spread(bb_im, G_S5, S5_GROUP, N_S5)
    wc_ref[0, :S5_STATE] = spread(cre_ref[0], G_S5, N_S5, S5_GROUP)
    wc_ref[0, S5_STATE:] = spread(-cim_ref[0], G_S5, N_S5, S5_GROUP)


def _s5_weights(a_re, a_im, log_dt, b_re, b_im, c_re, c_im):
    m = a_re.shape[0]
    s = S5_STATE
    blk = lambda *shape: pl.BlockSpec((1,) + shape, lambda i: (i,) + (0,) * len(shape))
    return pl.pallas_call(
        _s5_disc_kernel,
        grid=(m,),
        in_specs=[blk(G_S5, N_S5), blk(G_S5, N_S5), blk(G_S5, 1), blk(G_S5, S5_GROUP, N_S5),
                  blk(G_S5, S5_GROUP, N_S5), blk(s, S5_GROUP), blk(s, S5_GROUP)],
        out_specs=[blk(G_S5, N_S5), blk(G_S5, N_S5), blk(D_S5, s), blk(D_S5, s), blk(2 * s, D_S5)],
        out_shape=[jax.ShapeDtypeStruct((m, G_S5, N_S5), F32), jax.ShapeDtypeStruct((m, G_S5, N_S5), F32),
                   jax.ShapeDtypeStruct((m, D_S5, s), BF16), jax.ShapeDtypeStruct((m, D_S5, s), BF16),
                   jax.ShapeDtypeStruct((m, 2 * s, D_S5), BF16)],
        compiler_params=_params(("arbitrary",)),
        name="s5_weights",
    )(a_re, a_im, log_dt, b_re, b_im, c_re, c_im)


def _s5_scan_kernel(u_ref, wbre_ref, wbim_ref, abre_ref, abim_ref, wc_ref, h0_ref,
                    y_ref, hfin_ref, bure_ref, buim_ref, hre_ref, him_ref, *, nb, lane_w):
    d = pl.program_id(0)
    c = pl.program_id(1)
    nc = pl.num_programs(1)
    rows = u_ref.shape[0]

    @pl.when(c == 0)
    def _():
        hre_ref[...] = h0_ref[0, 0]
        him_ref[...] = h0_ref[0, 1]

    ub = u_ref[...].astype(BF16)
    ct = 2 * LANES
    for j in range(S5_STATE // ct):
        first_channel = j * ct * S5_GROUP // N_S5
        ks = slice(first_channel // ct * ct, first_channel // ct * ct + ct)
        js = slice(ct * j, ct * (j + 1))
        bure_ref[:, js] = jnp.dot(ub[:, ks], wbre_ref[0, ks, js], preferred_element_type=F32)
        buim_ref[:, js] = jnp.dot(ub[:, ks], wbim_ref[0, ks, js], preferred_element_type=F32)

    fwd = d == 0
    if nb >= SUBLANES:
        steps = rows // nb
        for lc in range(S5_STATE // lane_w):
            ls = slice(lc * lane_w, (lc + 1) * lane_w)
            ar = jnp.broadcast_to(abre_ref[0, :, ls], (nb, lane_w))
            ai = jnp.broadcast_to(abim_ref[0, :, ls], (nb, lane_w))

            def body(s, carry, ls=ls, ar=ar, ai=ai):
                hr, hi = carry
                t = jnp.where(fwd, s, steps - 1 - s)
                r0 = pl.multiple_of(t * nb, nb)
                br = bure_ref[pl.ds(r0, nb), ls]
                bi = buim_ref[pl.ds(r0, nb), ls]
                nr = ar * hr - ai * hi + br
                ni = ar * hi + ai * hr + bi
                bure_ref[pl.ds(r0, nb), ls] = nr
                buim_ref[pl.ds(r0, nb), ls] = ni
                return nr, ni

            hr, hi = lax.fori_loop(0, steps, body, (hre_ref[:, ls], him_ref[:, ls]))
            hre_ref[:, ls] = hr
            him_ref[:, ls] = hi
    else:
        per = SUBLANES // nb
        groups = rows // SUBLANES
        row_id = lax.broadcasted_iota(jnp.int32, (SUBLANES, lane_w), 0) // nb
        shift = jnp.where(fwd, nb, SUBLANES - nb)
        for lc in range(S5_STATE // lane_w):
            ls = slice(lc * lane_w, (lc + 1) * lane_w)
            ar = jnp.broadcast_to(abre_ref[0, :, ls], (SUBLANES, lane_w))
            ai = jnp.broadcast_to(abim_ref[0, :, ls], (SUBLANES, lane_w))
            h0r = hre_ref[:, ls]
            h0i = him_ref[:, ls]

            def body(gidx, carry, ls=ls, ar=ar, ai=ai):
                tr, ti = carry
                g = jnp.where(fwd, gidx, groups - 1 - gidx)
                r0 = pl.multiple_of(g * SUBLANES, SUBLANES)
                br = bure_ref[pl.ds(r0, SUBLANES), ls]
                bi = buim_ref[pl.ds(r0, SUBLANES), ls]
                for k in range(per):
                    pr = pltpu.roll(tr, shift, 0)
                    pi = pltpu.roll(ti, shift, 0)
                    nr = ar * pr - ai * pi + br
                    ni = ar * pi + ai * pr + bi
                    tgt = jnp.where(fwd, k, per - 1 - k)
                    sel = row_id == tgt
                    tr = jnp.where(sel, nr, tr)
                    ti = jnp.where(sel, ni, ti)
                bure_ref[pl.ds(r0, SUBLANES), ls] = tr
                buim_ref[pl.ds(r0, SUBLANES), ls] = ti
                return tr, ti

            tr, ti = lax.fori_loop(0, groups, body, (h0r, h0i))
            hre_ref[:, ls] = tr
            him_ref[:, ls] = ti

    for q in range(D_S5 // ct):
        qs = slice(ct * q, ct * (q + 1))
        ss = slice(q * ct * N_S5 // S5_GROUP, (q + 1) * ct * N_S5 // S5_GROUP)
        si = slice(S5_STATE + ss.start, S5_STATE + ss.stop)
        y_ref[0, :, qs] = (jnp.dot(bure_ref[:, ss].astype(BF16), wc_ref[0, ss, qs], preferred_element_type=F32)
                           + jnp.dot(buim_ref[:, ss].astype(BF16), wc_ref[0, si, qs],
                                     preferred_element_type=F32))

    @pl.when(c == nc - 1)
    def _():
        hfin_ref[0, 0] = hre_ref[...]
        hfin_ref[0, 1] = him_ref[...]


def _s5_scan(u, wb_re, wb_im, ab_re, ab_im, wc, h0, nb, row0, rows, chunk_rows=1024):
    if nb < SUBLANES:
        h0 = jnp.tile(h0, (1, 1, SUBLANES // nb, 1))
    srows = max(nb, SUBLANES)
    cr = min(chunk_rows, rows)
    nc = rows // cr
    c0 = row0 // cr
    s = S5_STATE
    lane_w = 256 if nb >= SUBLANES else 512
    chunk = lambda d, c: c + d * (nc - 1 - 2 * c)
    y, hfin = pl.pallas_call(
        functools.partial(_s5_scan_kernel, nb=nb, lane_w=lane_w),
        grid=(2, nc),
        in_specs=[pl.BlockSpec((cr, D_S5), lambda d, c: (c0 + chunk(d, c), 0)),
                  pl.BlockSpec((1, D_S5, s), lambda d, c: (d, 0, 0)),
                  pl.BlockSpec((1, D_S5, s), lambda d, c: (d, 0, 0)),
                  pl.BlockSpec((1, 1, s), lambda d, c: (d, 0, 0)),
                  pl.BlockSpec((1, 1, s), lambda d, c: (d, 0, 0)),
                  pl.BlockSpec((1, 2 * s, D_S5), lambda d, c: (d, 0, 0)),
                  pl.BlockSpec((1, 2, srows, s), lambda d, c: (d, 0, 0, 0))],
        out_specs=[pl.BlockSpec((1, cr, D_S5), lambda d, c: (d, chunk(d, c), 0)),
                   pl.BlockSpec((1, 2, srows, s), lambda d, c: (d, 0, 0, 0))],
        out_shape=[jax.ShapeDtypeStruct((2, rows, D_S5), F32),
                   jax.ShapeDtypeStruct((2, 2, srows, s), F32)],
        scratch_shapes=[pltpu.VMEM((cr, s), F32), pltpu.VMEM((cr, s), F32),
                        pltpu.VMEM((srows, s), F32), pltpu.VMEM((srows, s), F32)],
        compiler_params=_params(("arbitrary", "arbitrary")),
        name="s5_scan",
    )(u, wb_re, wb_im, ab_re, ab_im, wc, h0)
    if nb < SUBLANES:
        hfin = jnp.stack([hfin[0, :, srows - nb:], hfin[1, :, :nb]])
    return y, hfin


def _rwkv_prep_kernel(prev_ref, cur_ref, next_ref, mu_ref, ones_ref, g2_ref, kk_ref, ka_ref, rk_ref,
                      w0_ref, w2_ref, a0_ref, a2_ref,
                      r_ref, v_ref, nkk_ref, bonus_ref, g_ref, w_ref, kd_ref, b_ref, *, nb, seq, grid_shift):
    tm = cur_ref.shape[0]
    i = pl.program_id(0)
    cur = cur_ref[...]
    prv = prev_ref[...]
    nxt = next_ref[...]
    row = lax.broadcasted_iota(jnp.int32, (tm, RWKV_COLS), 0)
    lane = lax.broadcasted_iota(jnp.int32, (tm, RWKV_COLS), 1)
    t = (i * tm + row) // nb

    halo = prv.shape[0]

    def rows_before(s):
        if s == tm:
            return prv
        if s == halo:
            return jnp.concatenate([prv, cur[:tm - s]], axis=0)
        return jnp.where(row < s, pltpu.roll(prv, s, 0), pltpu.roll(cur, s, 0))

    def rows_after(s):
        if s == tm:
            return nxt
        if s == halo:
            return jnp.concatenate([cur[s:], nxt], axis=0)
        return jnp.where(row >= tm - s, pltpu.roll(nxt, tm - s, 0), pltpu.roll(cur, tm - s, 0))

    if grid_shift:
        tw = t % GRID_W
        left = jnp.where(tw == 0, 0.0, rows_before(nb))
        right = jnp.where(tw == GRID_W - 1, 0.0, rows_after(nb))
        up = jnp.where(t < GRID_W, 0.0, rows_before(nb * GRID_W))
        down = jnp.where(t >= seq - GRID_W, 0.0, rows_after(nb * GRID_W))
        m4 = lane % 4
        sh = jnp.where(m4 == 0, left, jnp.where(m4 == 1, right, jnp.where(m4 == 2, up, down)))
    else:
        before = jnp.where(t == 0, 0.0, rows_before(nb))
        after = jnp.where(t == seq - 1, 0.0, rows_after(nb))
        sh = jnp.where(lane % 2 == 0, before, after)

    z = cur + (sh - cur) * mu_ref[...]
    r = z[:, 0:D_RWKV]
    k = z[:, D_RWKV:2 * D_RWKV]
    v = z[:, 2 * D_RWKV:3 * D_RWKV]
    xw = z[:, 3 * D_RWKV:3 * D_RWKV + LORA_W]
    xa = z[:, 3 * D_RWKV + LORA_W:3 * D_RWKV + LORA_W + LORA_A]
    xg = z[:, 3 * D_RWKV + LORA_W + LORA_A:]
    ones = ones_ref[...]

    g_ref[...] = _dot(_sigmoid(xg), g2_ref[...])
    kk = k * kk_ref[...]
    nrm = jnp.sqrt(_dot_split(kk * kk, ones))
    kk = kk / jnp.maximum(nrm, 1e-12)
    r_ref[...] = r
    v_ref[...] = v
    nkk_ref[...] = -kk
    bonus_ref[...] = _dot_split(r * k * rk_ref[...], ones) * v
    txw = jnp.tanh(xw)
    for d in range(2):
        zw = -(w0_ref[d] + _dot(txw, w2_ref[d]))
        softplus = jnp.maximum(zw, 0.0) + jnp.log(1.0 + jnp.exp(-jnp.abs(zw)))
        w_log = -softplus - 0.5
        w_ref[d] = jnp.exp(-jnp.exp(w_log))
        a = _sigmoid(a0_ref[d] + _dot(xa, a2_ref[d]))
        kd_ref[d] = k * (1.0 + (a - 1.0) * ka_ref[...])
        b_ref[d] = kk * a


def _rwkv_prep(zr, p, nb, seq, grid_shift, row0, tm=128):
    rows = seq * nb
    nt = rows // tm
    t0 = row0 // tm
    if grid_shift:
        assert tm == nb * GRID_W, "one tile must be one grid row of the latent grid"
    else:
        assert tm % nb == 0 and tm >= nb
    kern = functools.partial(_rwkv_prep_kernel, nb=nb, seq=seq, grid_shift=grid_shift)
    c = D_RWKV
    vec = lambda n: pl.BlockSpec((1, n), lambda i: (0, 0))
    mat = lambda a, b: pl.BlockSpec((a, b), lambda i: (0, 0))
    row_out = pl.BlockSpec((tm, c), lambda i: (i, 0))
    dir_out = pl.BlockSpec((2, tm, c), lambda i: (0, i, 0))
    head_ones = jnp.kron(jnp.eye(H_RWKV, dtype=F32), jnp.ones((RWKV_HEAD, RWKV_HEAD), F32)).astype(BF16)
    halo = tm if grid_shift else nb
    assert halo % SUBLANES == 0
    hpt = tm // halo
    h0 = t0 * hpt
    in_specs = [pl.BlockSpec((halo, RWKV_COLS), lambda i: (h0 + jnp.maximum(i * hpt - 1, 0), 0)),
                pl.BlockSpec((tm, RWKV_COLS), lambda i: (t0 + i, 0)),
                pl.BlockSpec((halo, RWKV_COLS), lambda i: (h0 + jnp.minimum((i + 1) * hpt, nt * hpt - 1), 0)),
                vec(RWKV_COLS), mat(c, c), mat(LORA_G, c), vec(c), vec(c), vec(c),
                pl.BlockSpec((2, 1, c), lambda i: (0, 0, 0)),
                pl.BlockSpec((2, LORA_W, c), lambda i: (0, 0, 0)),
                pl.BlockSpec((2, 1, c), lambda i: (0, 0, 0)),
                pl.BlockSpec((2, LORA_A, c), lambda i: (0, 0, 0))]
    args = [zr, zr, zr, p['rw_mu'].reshape(1, -1), head_ones, p['rw_g2'], p['rw_k_k'].reshape(1, c),
            p['rw_k_a'].reshape(1, c), p['rw_r_k'].reshape(1, c), p['rw_w0'].reshape(2, 1, c), p['rw_w2'],
            p['rw_a0'].reshape(2, 1, c), p['rw_a2']]
    part = jax.ShapeDtypeStruct((rows, c), F32)
    return pl.pallas_call(
        kern,
        grid=(nt,),
        in_specs=in_specs,
        out_specs=[row_out] * 5 + [dir_out] * 3,
        out_shape=[part] * 5 + [jax.ShapeDtypeStruct((2, rows, c), F32)] * 3,
        compiler_params=_params(("arbitrary",)),
        name="rwkv_prep",
    )(*args)


def _rwkv_scan_kernel(a_ref, w_ref, b_ref, k_ref, r_ref, v_ref, s0_ref, y_ref, sfin_ref, s_ref, *,
                      nacc, row_blk, folded):
    d = pl.program_id(0)
    c = pl.program_id(1)
    nc = pl.num_programs(1)
    tc = a_ref.shape[1]
    nj = RWKV_HEAD
    j_unroll = 32 if row_blk >= 32 else nj
    rev = (c >= nc // 2) if folded else (d == 1)

    @pl.when(c == 0)
    def _():
        s_ref[...] = s0_ref[0]

    def step(s, carry):
        t = jnp.where(rev, tc - 1 - s, s)
        zero = jnp.zeros((row_blk, a_ref.shape[3]), F32)
        for rb in range(s_ref.shape[1] // row_blk):
            rows = pl.ds(rb * row_blk, row_blk)

            def pass1(jo, acc):
                acc = list(acc)
                for ji in range(j_unroll):
                    j = jo * j_unroll + ji
                    acc[ji % nacc] = acc[ji % nacc] + s_ref[j, rows, :] * a_ref[0, t, pl.ds(j, 1), :]
                return tuple(acc)

            sa = functools.reduce(lambda x, y: x + y, lax.fori_loop(0, nj // j_unroll, pass1, (zero,) * nacc))
            vv = v_ref[0, t, rows, :]

            def pass2(jo, acc):
                acc = list(acc)
                for ji in range(j_unroll):
                    j = jo * j_unroll + ji
                    s_new = (s_ref[j, rows, :] * w_ref[0, t, pl.ds(j, 1), :]
                             + sa * b_ref[0, t, pl.ds(j, 1), :]
                             + vv * k_ref[0, t, pl.ds(j, 1), :])
                    s_ref[j, rows, :] = s_new
                    acc[ji % nacc] = acc[ji % nacc] + s_new * r_ref[0, t, pl.ds(j, 1), :]
                return tuple(acc)

            y_ref[0, t, rows, :] = functools.reduce(
                lambda x, y: x + y, lax.fori_loop(0, nj // j_unroll, pass2, (zero,) * nacc))
        return carry

    lax.fori_loop(0, tc, step, 0)

    @pl.when(c == nc - 1)
    def _():
        sfin_ref[0] = s_ref[...]


def _rwkv_scan(a, w, b, k, r, v, s0, tc, folded):
    _, tp, _, nl = w.shape
    iv = v.shape[2]
    row_blk = min(iv, 32)
    nacc = 4 if row_blk <= 16 else 2
    nc_half = tp // tc
    if folded:
        nd, nc = 1, 2 * nc_half
        tmap = lambda d, c: (c // nc_half, jnp.where(c < nc_half, c, nc - 1 - c), 0, 0)
        shared = per_dir = tmap
    else:
        nd, nc = 2, nc_half
        chunk = lambda d, c: c + d * (nc - 1 - 2 * c)
        shared = lambda d, c: (0, chunk(d, c), 0, 0)
        per_dir = lambda d, c: (d, chunk(d, c), 0, 0)
    key = lambda m: pl.BlockSpec((1, tc, RWKV_HEAD, nl), m)
    val = lambda m: pl.BlockSpec((1, tc, iv, nl), m)
    sspec = pl.BlockSpec((1, RWKV_HEAD, iv, nl), lambda d, c: (d, 0, 0, 0))
    return pl.pallas_call(
        functools.partial(_rwkv_scan_kernel, nacc=nacc, row_blk=row_blk, folded=folded),
        grid=(nd, nc),
        in_specs=[key(shared), key(per_dir), key(per_dir), key(per_dir), key(shared), val(shared), sspec],
        out_specs=[val(per_dir), sspec],
        out_shape=[jax.ShapeDtypeStruct((2, tp, iv, nl), F32),
                   jax.ShapeDtypeStruct((nd, RWKV_HEAD, iv, nl), F32)],
        scratch_shapes=[pltpu.VMEM((RWKV_HEAD, iv, nl), F32)],
        compiler_params=_params(("arbitrary", "arbitrary")),
        name="rwkv_scan",
    )(a, w, b, k, r, v, s0)


SMP_REP = 4
SMP_ROWS = RWKV_HEAD // SMP_REP


def _block_transpose(tiles, bs):
    tiles = list(tiles)
    lane = lax.broadcasted_iota(jnp.int32, tiles[0].shape, 1)
    for kbit in range(3):
        sft = bs << kbit
        bit = (lane // sft) % 2 == 1
        for r in range(8):
            if (r >> kbit) & 1:
                continue
            r2 = r | (1 << kbit)
            lo, hi = tiles[r], tiles[r2]
            tiles[r] = jnp.where(bit, pltpu.roll(hi, sft, 1), lo)
            tiles[r2] = jnp.where(bit, hi, pltpu.roll(lo, LANES - sft, 1))
    return tiles


def _head_tiles(x):
    xt = x.T
    return [xt[h * RWKV_HEAD:(h + 1) * RWKV_HEAD, :] for h in range(H_RWKV)]


def _from_head_tiles(tiles):
    return jnp.concatenate(tiles, axis=0).T


def _relayout_ctx_kernel(r_ref, v_ref, nkk_ref, w_ref, kd_ref, b_ref, a_o, r_o, v_o, w_o, b_o, k_o, *, nb):
    steps = a_o.shape[1]
    for src, dst in ((nkk_ref, a_o), (r_ref, r_o), (v_ref, v_o)):
        tiles = _block_transpose(_head_tiles(src[...]), nb)
        for t in range(steps):
            dst[0, t] = tiles[t]
    for src, dst in ((w_ref, w_o), (b_ref, b_o), (kd_ref, k_o)):
        for d in range(2):
            tiles = _block_transpose(_head_tiles(src[d]), nb)
            for t in range(steps):
                dst[d, t] = tiles[t]


def _relayout_ctx(prep, nb, seq):
    r, v, nkk, _, _, w, kd, bb = prep
    steps = LANES // nb
    assert steps == H_RWKV and nb * H_RWKV == LANES
    tm = steps * nb
    nt = seq // steps
    c = D_RWKV
    nat = pl.BlockSpec((tm, c), lambda i: (i, 0))
    nat2 = pl.BlockSpec((2, tm, c), lambda i: (0, i, 0))
    out1 = pl.BlockSpec((1, steps, RWKV_HEAD, LANES), lambda i: (0, i, 0, 0))
    out2 = pl.BlockSpec((2, steps, RWKV_HEAD, LANES), lambda i: (0, i, 0, 0))
    s1 = jax.ShapeDtypeStruct((1, seq, RWKV_HEAD, LANES), F32)
    s2 = jax.ShapeDtypeStruct((2, seq, RWKV_HEAD, LANES), F32)
    a, r_, v_, w_, b_, k_ = pl.pallas_call(
        functools.partial(_relayout_ctx_kernel, nb=nb),
        grid=(nt,),
        in_specs=[nat, nat, nat, nat2, nat2, nat2],
        out_specs=[out1, out1, out1, out2, out2, out2],
        out_shape=[s1, s1, s1, s2, s2, s2],
        compiler_params=_params(("arbitrary",)),
        name="rwkv_relayout_ctx",
    )(r, v, nkk, w, kd, bb)
    return a, w_, b_, k_, r_, v_


def _unmix_ctx_kernel(y_ref, o_ref, *, nb):
    steps = y_ref.shape[1]
    tiles = [y_ref[0, t] + y_ref[1, t] for t in range(steps)]
    o_ref[...] = _from_head_tiles(_block_transpose(tiles, nb))


def _unmix_ctx(y, nb, seq):
    steps = LANES // nb
    tm = steps * nb
    return pl.pallas_call(
        functools.partial(_unmix_ctx_kernel, nb=nb),
        grid=(seq // steps,),
        in_specs=[pl.BlockSpec((2, steps, RWKV_HEAD, LANES), lambda i: (0, i, 0, 0))],
        out_specs=pl.BlockSpec((tm, D_RWKV), lambda i: (i, 0)),
        out_shape=jax.ShapeDtypeStruct((seq * nb, D_RWKV), F32),
        compiler_params=_params(("arbitrary",)),
        name="rwkv_unmix_ctx",
    )(y)


def _relayout_smp_kernel(ra, rb, va, vb, na, nb_, wa, wb, ka, kb, ba, bb, sel_ref,
                         a_o, r_o, v_o, w_o, b_o, k_o):
    steps = a_o.shape[1]
    lane_v = lax.broadcasted_iota(jnp.int32, (SMP_ROWS, LANES), 1)

    def conv(xa, xb):
        tiles = _block_transpose(_head_tiles(jnp.concatenate([xa, xb], axis=0)), 2)
        parts = []
        for x in tiles:
            hi = x.astype(BF16)
            r1 = x - hi.astype(F32)
            mid = r1.astype(BF16)
            lo = (r1 - mid.astype(F32)).astype(BF16)
            parts.append((hi, mid, lo))
        return parts

    def expand(o0, o1, half):
        order0 = range(8) if half == 0 else range(7, -1, -1)
        order1 = range(7, -1, -1) if half == 0 else range(8)
        cols = []
        for term in range(3):
            cols.append(jnp.concatenate([o0[s][term] for s in order0], axis=0))
            cols.append(jnp.concatenate([o1[s][term] for s in order1], axis=0))
        lhs = jnp.concatenate(cols, axis=1)
        out = [None] * steps
        for pair in range(steps // 16):
            res = jnp.dot(lhs, sel_ref[half, pair], preferred_element_type=F32)
            for q in range(2):
                for s_lo in range(8):
                    out[8 * (2 * pair + q) + s_lo] = res[64 * s_lo:64 * (s_lo + 1), LANES * q:LANES * (q + 1)]
        return out

    def value_rows(full):
        blk = lane_v // (LANES // SMP_REP)
        rows = [full[k * SMP_ROWS:(k + 1) * SMP_ROWS] for k in range(SMP_REP)]
        return jnp.where(blk == 0, rows[0], jnp.where(blk == 1, rows[1], jnp.where(blk == 2, rows[2], rows[3])))

    for srca, srcb, dst in ((na, nb_, a_o), (ra, rb, r_o), (va, vb, v_o)):
        o = conv(srca[...], srcb[...])
        for half in range(2):
            tiles = expand(o, o, half)
            for s in range(steps):
                dst[half, s] = value_rows(tiles[s]) if dst is v_o else tiles[s]
    for srca, srcb, dst in ((wa, wb, w_o), (ba, bb, b_o), (ka, kb, k_o)):
        o0 = conv(srca[0], srcb[0])
        o1 = conv(srca[1], srcb[1])
        for half in range(2):
            tiles = expand(o0, o1, half)
            for s in range(steps):
                dst[half, s] = tiles[s]


def _smp_selection():
    sel = np.zeros((2, 2, 6, LANES, 2 * LANES), np.float32)
    for half in range(2):
        for pair in range(2):
            for q in range(2):
                s_hi = 2 * pair + q
                grp_a = 0 * 64 + 16 * s_hi
                grp_b = 1 * 64 + 16 * (3 - s_hi)
                src = (grp_a, grp_b) if half == 0 else (grp_b, grp_a)
                for d in range(2):
                    for blk in range(SMP_REP):
                        for hb in range(16):
                            col = q * LANES + blk * 32 + d * 16 + hb
                            for term in range(3):
                                sel[half, pair, 2 * term + d, src[d] + hb, col] = 1.0
    return jnp.asarray(sel.reshape(2, 2, 6 * LANES, 2 * LANES), BF16)


def _relayout_smp(prep, nb, seq, row0):
    r, v, nkk, _, _, w, kd, bb = prep
    assert 2 * nb * H_RWKV * SMP_REP == LANES
    steps = 32
    tm = steps * nb
    nt = seq // steps
    half_t = seq // 2
    c = D_RWKV
    t0 = row0 // tm
    blk_a = pl.BlockSpec((tm, c), lambda g: (t0 + g, 0))
    blk_b = pl.BlockSpec((tm, c), lambda g: (t0 + nt - 1 - g, 0))
    blk2_a = pl.BlockSpec((2, tm, c), lambda g: (0, t0 + g, 0))
    blk2_b = pl.BlockSpec((2, tm, c), lambda g: (0, t0 + nt - 1 - g, 0))
    okey = pl.BlockSpec((2, steps, RWKV_HEAD, LANES), lambda g: (0, g, 0, 0))
    oval = pl.BlockSpec((2, steps, SMP_ROWS, LANES), lambda g: (0, g, 0, 0))
    skey = jax.ShapeDtypeStruct((2, half_t, RWKV_HEAD, LANES), F32)
    sval = jax.ShapeDtypeStruct((2, half_t, SMP_ROWS, LANES), F32)
    a, r_, v_, w_, b_, k_ = pl.pallas_call(
        _relayout_smp_kernel,
        grid=(nt // 2,),
        in_specs=[blk_a, blk_b, blk_a, blk_b, blk_a, blk_b, blk2_a, blk2_b, blk2_a, blk2_b, blk2_a, blk2_b,
                  pl.BlockSpec((2, 2, 6 * LANES, 2 * LANES), lambda g: (0, 0, 0, 0))],
        out_specs=[okey, okey, oval, okey, okey, okey],
        out_shape=[skey, skey, sval, skey, skey, skey],
        compiler_params=_params(("arbitrary",)),
        name="rwkv_relayout_smp",
    )(r, r, v, v, nkk, nkk, w, w, kd, kd, bb, bb, _smp_selection())
    return a, w_, b_, k_, r_, v_


def _unmix_smp_kernel(y_ref, o_ref, *, half_tiles):
    q = pl.program_id(0)
    steps = y_ref.shape[1]
    low = q < half_tiles
    h0 = jnp.where(low, 0, 1)
    lane = lax.broadcasted_iota(jnp.int32, (SMP_ROWS, LANES), 1)
    tiles = []
    for s_lo in range(8):
        blocks = []
        for blk in range(SMP_REP):
            acc = jnp.zeros((SMP_ROWS, LANES), F32)
            for s_hi in range(steps // 8):
                s = 8 * s_hi + s_lo
                u = jnp.where(low, s, steps - 1 - s)
                d0 = y_ref[h0, u]
                d1 = y_ref[1 - h0, u]
                sh0 = (16 * s_hi - 32 * blk) % LANES
                sh1 = (16 * s_hi - 32 * blk - 16) % LANES
                t0 = d0 if sh0 == 0 else pltpu.roll(d0, sh0, 1)
                t1 = d1 if sh1 == 0 else pltpu.roll(d1, sh1, 1)
                acc = jnp.where(lane // 16 == s_hi, t0 + t1, acc)
            blocks.append(acc)
        tiles.append(jnp.concatenate(blocks, axis=0))
    o_ref[...] = _from_head_tiles(_block_transpose(tiles, 2))


def _unmix_smp(y, nb, seq):
    steps = 64
    tm = steps * nb
    assert tm == LANES
    nq = seq // steps
    half_tiles = nq // 2
    return pl.pallas_call(
        functools.partial(_unmix_smp_kernel, half_tiles=half_tiles),
        grid=(nq,),
        in_specs=[pl.BlockSpec((2, steps, SMP_ROWS, LANES),
                               lambda q: (0, jnp.where(q < half_tiles, q, nq - 1 - q), 0, 0))],
        out_specs=pl.BlockSpec((tm, D_RWKV), lambda q: (q, 0)),
        out_shape=jax.ShapeDtypeStruct((seq * nb, D_RWKV), F32),
        compiler_params=_params(("arbitrary",)),
        name="rwkv_unmix_smp",
    )(y)


def _rwkv_state_smp(state_l, nb):
    s = state_l.reshape(nb, 2, H_RWKV, SMP_REP, SMP_ROWS, RWKV_HEAD)
    s = s.transpose(5, 4, 3, 1, 2, 0)
    return s.reshape(1, RWKV_HEAD, SMP_ROWS, LANES)


def _rwkv_state_out_ctx(sfin, nb):
    s = sfin.reshape(2, RWKV_HEAD, RWKV_HEAD, H_RWKV, nb)
    return s.transpose(4, 0, 3, 2, 1)


def _mix_out_kernel(x_ref, pat_ref, u_ref, gs_ref, gr_ref,
                    y5c_ref, y5s_ref, yrwc_ref, yrws_ref, bonc_ref, bons_ref, gc_ref, gs2_ref,
                    d_ref, wglu_ref, ones_ref, lnxw_ref, lnxb_ref, wups_ref, wupr_ref, wout_ref,
                    ln1g_ref, ln1b_ref, o_ref, *, ctx_tiles):
    is_ctx = pl.program_id(0) < ctx_tiles
    pick = lambda a, b: jnp.where(is_ctx, a, b)
    y5 = u_ref[...] * d_ref[...] + pick(y5c_ref[0] + y5c_ref[1], y5s_ref[0] + y5s_ref[1])
    y5 = jax.nn.gelu(y5)
    y5 = y5 * _sigmoid(_dot(y5, wglu_ref[...]))
    ones = ones_ref[...]
    yr = pick(yrwc_ref[...], yrws_ref[...])
    inv_n = 1.0 / RWKV_HEAD
    mu = _dot_split(yr, ones) * inv_n
    yc = yr - mu
    var = _dot_split(yc * yc, ones) * inv_n
    yr = yc * lax.rsqrt(var + GN_EPS) * lnxw_ref[...] + lnxb_ref[...]
    yr = (yr + pick(bonc_ref[...], bons_ref[...])) * pick(gc_ref[...], gs2_ref[...])
    merged = (_sigmoid(gs_ref[...]) * _dot(y5, wups_ref[...])
              + _sigmoid(gr_ref[...]) * _dot(yr, wupr_ref[...]))
    mix = _dot(merged, wout_ref[...])
    x3 = _modulate(mix, jnp.zeros_like(pat_ref[0, 2]), pat_ref[0, 2] - 1.0)
    o_ref[...] = _layer_norm(ALPHA * x_ref[...] + x3, ln1g_ref[...], ln1b_ref[...])


def _mix_out(x, pat, u, gs, gr, y5, yrw, bonus, g, p, rows_per_trunk, tm=512):
    rows = x.shape[0]
    tm = min(tm, rows_per_trunk)
    d = D_MODEL
    c = D_RWKV
    tpt = rows_per_trunk // tm
    rowblk = lambda n: pl.BlockSpec((tm, n), lambda i: (i, 0))
    ctx_i = lambda i: jnp.minimum(i, tpt - 1)
    smp_i = lambda i: jnp.maximum(i - tpt, 0)
    pair = lambda spec_of: [spec_of(ctx_i), spec_of(smp_i)]
    row_t = lambda f: pl.BlockSpec((tm, c), lambda i: (f(i), 0))
    dir_t = lambda f: pl.BlockSpec((2, tm, c), lambda i: (0, f(i), 0))
    vec = lambda n: pl.BlockSpec((1, n), lambda i: (0, 0))
    mat = lambda a, b: pl.BlockSpec((a, b), lambda i: (0, 0))
    head_ones = jnp.kron(jnp.eye(H_RWKV, dtype=F32), jnp.ones((RWKV_HEAD, RWKV_HEAD), F32)).astype(BF16)
    return pl.pallas_call(
        functools.partial(_mix_out_kernel, ctx_tiles=tpt),
        grid=(rows // tm,),
        in_specs=[rowblk(d),
                  pl.BlockSpec((1, 6, SUBLANES, d), lambda i: (i // tpt, 0, 0, 0)),
                  rowblk(c), rowblk(d), rowblk(d)]
                 + pair(dir_t) + pair(row_t) + pair(row_t) + pair(row_t)
                 + [vec(c), mat(c, c), mat(c, c), vec(c), vec(c), mat(c, d), mat(c, d), mat(d, d),
                    vec(d), vec(d)],
        out_specs=rowblk(d),
        out_shape=jax.ShapeDtypeStruct((rows, d), F32),
        compiler_params=_params(("arbitrary",)),
        name="mix_out",
    )(x, pat, u, gs, gr, *y5, *yrw, *bonus, *g,
      p['s5_d'].reshape(1, c), p['s5_w_glu'].astype(BF16), head_ones,
      p['rw_lnx_w'].reshape(1, c), p['rw_lnx_b'].reshape(1, c),
      p['w_up_s5'].astype(BF16), p['w_up_rwkv'].astype(BF16), p['w_out'].astype(BF16),
      p['ln1_g'].reshape(1, d), p['ln1_b'].reshape(1, d))


def _first_max(x, lane, valid):
    xm = jnp.where(valid, x, -jnp.inf)
    m = jnp.max(xm, -1, keepdims=True)
    idx = jnp.min(jnp.where(xm == m, lane, float(N_EXPERTS)), -1, keepdims=True)
    return m, idx


def _moe_kernel(x_ref, pat_ref, wr_ref, br_ref, wg_ref, wu_ref, wd_ref, ln2g_ref, ln2b_ref, o_ref,
                hp_ref, cp_ref, acc_ref, pos_ref, bounds_ref):
    e = pl.program_id(1)
    ne = pl.num_programs(1)
    tm = x_ref.shape[0]

    @pl.when(e == 0)
    def _():
        h = _modulate(x_ref[...], pat_ref[0, 3], pat_ref[0, 4])
        wr = wr_ref[...]
        w_hi = wr.astype(BF16)
        w_lo = (wr - w_hi.astype(F32)).astype(BF16)
        h_hi = h.astype(BF16)
        h_lo = (h - h_hi.astype(F32)).astype(BF16)
        logits = (jnp.dot(h_hi, w_hi, preferred_element_type=F32)
                  + jnp.dot(h_hi, w_lo, preferred_element_type=F32)
                  + jnp.dot(h_lo, w_hi, preferred_element_type=F32)) + br_ref[...]
        logits = logits - jnp.max(logits, -1, keepdims=True)
        ex = jnp.exp(logits)
        probs = ex / jnp.sum(ex, -1, keepdims=True)
        lane_i = lax.broadcasted_iota(jnp.int32, (tm, N_EXPERTS), 1)
        lane = lane_i.astype(F32)
        grp = (lane_i // EXPERTS_PER_GROUP).astype(F32)
        best_score = jnp.full((tm, 1), -jnp.inf, F32)
        best_grp = jnp.zeros((tm, 1), F32)
        for gi in range(N_GROUPS):
            in_g = grp == float(gi)
            m1, i1 = _first_max(probs, lane, in_g)
            m2, _ = _first_max(probs, lane, in_g & (lane != i1))
            score = m1 + m2
            better = score > best_score
            best_score = jnp.where(better, score, best_score)
            best_grp = jnp.where(better, float(gi), best_grp)
        in_best = grp == best_grp
        m1, i1 = _first_max(probs, lane, in_best)
        m2, i2 = _first_max(probs, lane, in_best & (lane != i1))
        tot = m1 + m2
        comb = jnp.where(lane == i1, m1 / tot, 0.0) + jnp.where(lane == i2, m2 / tot, 0.0)

        glane = lax.broadcasted_iota(jnp.int32, (tm, LANES), 1).astype(F32)
        onehot = (glane == best_grp).astype(BF16)
        r_id = lax.broadcasted_iota(jnp.int32, (tm, tm), 0)
        c_id = lax.broadcasted_iota(jnp.int32, (tm, tm), 1)
        prefix = jnp.dot((r_id >= c_id).astype(BF16), onehot, preferred_element_type=F32)
        counts = prefix[tm - 1:tm, :]
        lane1 = lax.broadcasted_iota(jnp.int32, (1, LANES), 1)
        start = jnp.zeros((1, LANES), F32)
        run = jnp.zeros((1, 1), F32)
        bounds_ref[0] = 0
        for gi in range(N_GROUPS):
            start = start + jnp.where(lane1 == gi, run, 0.0)
            run = run + jnp.sum(jnp.where(lane1 == gi, counts, 0.0), -1, keepdims=True)
            bounds_ref[gi + 1] = run[0, 0].astype(jnp.int32)
        pos = jnp.sum(onehot.astype(F32) * (start + prefix - 1.0), -1, keepdims=True)
        pos_row = jnp.transpose(jnp.broadcast_to(pos, (tm, LANES)))[0:1, :]
        perm = (r_id.astype(F32) == pos_row).astype(BF16)
        hp_ref[...] = jnp.dot(perm, h.astype(BF16), preferred_element_type=F32).astype(BF16)
        cp_ref[...] = _permute_rows(perm, comb)
        pos_ref[...] = pos
        acc_ref[...] = jnp.zeros_like(acc_ref)

    wg = wg_ref[0, 0].astype(BF16)
    wu = wu_ref[0, 0].astype(BF16)
    wd = wd_ref[0, 0].astype(BF16)
    sub = min(tm, MOE_SUB_ROWS)
    grp_first = bounds_ref[e // EXPERTS_PER_GROUP]
    grp_end = bounds_ref[e // EXPERTS_PER_GROUP + 1]

    for rb in range(tm // sub):
        @pl.when((grp_first < (rb + 1) * sub) & (grp_end > rb * sub))
        def _(rb=rb):
            rs = pl.ds(rb * sub, sub)
            hb = hp_ref[rs, :]
            lane = lax.broadcasted_iota(jnp.int32, (sub, N_EXPERTS), 1)
            ce = jnp.sum(jnp.where(lane == e, cp_ref[rs, :], 0.0), -1, keepdims=True)
            hid = (_silu(jnp.dot(hb, wg, preferred_element_type=F32))
                   * jnp.dot(hb, wu, preferred_element_type=F32))
            acc_ref[rs, :] += jnp.dot((hid * ce).astype(BF16), wd, preferred_element_type=F32)

    @pl.when(e == ne - 1)
    def _():
        c_lane = lax.broadcasted_iota(jnp.int32, (tm, tm), 1).astype(F32)
        unperm = (c_lane == pos_ref[...]).astype(BF16)
        ffn = _permute_rows(unperm, acc_ref[...])
        ffn = _modulate(ffn, jnp.zeros_like(pat_ref[0, 5]), pat_ref[0, 5] - 1.0)
        o_ref[...] = _layer_norm(ALPHA * x_ref[...] + ffn, ln2g_ref[...], ln2b_ref[...])


MOE_SUB_ROWS = 256


def _moe(x, pat, w_router, b_router, wg, wu, wd, layer, ln2_g, ln2_b, rows_per_trunk, tm=1024):
    rows = x.shape[0]
    tm = min(tm, rows_per_trunk)
    d = D_MODEL
    tpt = rows_per_trunk // tm
    rowblk = pl.BlockSpec((tm, d), lambda i, e: (i, 0))
    vec = lambda n: pl.BlockSpec((1, n), lambda i, e: (0, 0))
    return pl.pallas_call(
        _moe_kernel,
        grid=(rows // tm, N_EXPERTS),
        in_specs=[rowblk,
                  pl.BlockSpec((1, 6, SUBLANES, d), lambda i, e: (i // tpt, 0, 0, 0)),
                  pl.BlockSpec((d, N_EXPERTS), lambda i, e: (0, 0)), vec(N_EXPERTS),
                  pl.BlockSpec((1, 1, d, D_EXPERT), lambda i, e: (layer, e, 0, 0)),
                  pl.BlockSpec((1, 1, d, D_EXPERT), lambda i, e: (layer, e, 0, 0)),
                  pl.BlockSpec((1, 1, D_EXPERT, d), lambda i, e: (layer, e, 0, 0)),
                  vec(d), vec(d)],
        out_specs=rowblk,
        out_shape=jax.ShapeDtypeStruct((rows, d), F32),
        scratch_shapes=[pltpu.VMEM((tm, d), BF16), pltpu.VMEM((tm, N_EXPERTS), F32),
                        pltpu.VMEM((tm, d), F32), pltpu.VMEM((tm, 1), F32),
                        pltpu.SMEM((SUBLANES,), jnp.int32)],
        compiler_params=_params(("arbitrary", "arbitrary")),
        name="moe",
    )(x, pat, w_router, b_router.reshape(1, N_EXPERTS), wg, wu, wd,
      ln2_g.reshape(1, d), ln2_b.reshape(1, d))


def kernel(x_prompt, x_sample, state_s5, state_rwkv, c, c_ctx, w_ada, b_ada, w_in, s5_a_re, s5_a_im, s5_log_dt, s5_b_re, s5_b_im, s5_c_re, s5_c_im, s5_d, s5_w_glu, rw_mu, rw_w0, rw_w2, rw_a0, rw_a2, rw_g2, rw_k_k, rw_k_a, rw_r_k, rw_lnx_w, rw_lnx_b, w_up_s5, w_up_rwkv, w_out, ln1_g, ln1_b, ln2_g, ln2_b, w_router, b_router, w_exp_gate, w_exp_up, w_exp_down):
    nbc, tc_len, d = x_prompt.shape
    nbs, ts_len, _ = x_sample.shape
    nl = w_ada.shape[0]
    rc = nbc * tc_len
    rs = nbs * ts_len
    assert rc == rs, "both trunks are processed as equal halves of one row-major token matrix"
    assert SUBLANES % nbs == 0 and nbc % SUBLANES == 0

    x = _to_time_major(x_prompt, x_sample)

    cond = jnp.concatenate([c_ctx[None], c], axis=0)
    cond8 = jnp.zeros((SUBLANES, d), F32).at[:cond.shape[0]].set(cond)
    mod = _ada_mod(cond8, w_ada, b_ada).reshape(nl, SUBLANES, 6, d)
    ctx_rows = jnp.zeros((SUBLANES,), jnp.int32)
    smp_rows = 1 + jnp.arange(SUBLANES, dtype=jnp.int32) % nbs
    pat_idx = jnp.stack([ctx_rows, smp_rows])
    pats = mod[:, pat_idx]
    pats = pats.transpose(0, 1, 3, 2, 4)

    m = nl * 2
    ab_re, ab_im, wb_re, wb_im, wc = _s5_weights(
        s5_a_re.reshape(m, G_S5, N_S5), s5_a_im.reshape(m, G_S5, N_S5), s5_log_dt.reshape(m, G_S5, 1),
        s5_b_re.reshape(m, G_S5, N_S5, S5_GROUP).transpose(0, 1, 3, 2),
        s5_b_im.reshape(m, G_S5, N_S5, S5_GROUP).transpose(0, 1, 3, 2),
        s5_c_re.reshape(m, G_S5, S5_GROUP, N_S5).transpose(0, 1, 3, 2).reshape(m, S5_STATE, S5_GROUP),
        s5_c_im.reshape(m, G_S5, S5_GROUP, N_S5).transpose(0, 1, 3, 2).reshape(m, S5_STATE, S5_GROUP))
    ab_re = ab_re.reshape(nl, 2, 1, S5_STATE)
    ab_im = ab_im.reshape(nl, 2, 1, S5_STATE)
    wb_re = wb_re.reshape(nl, 2, D_S5, S5_STATE)
    wb_im = wb_im.reshape(nl, 2, D_S5, S5_STATE)
    wc = wc.reshape(nl, 2, 2 * S5_STATE, D_S5)

    w_in_bf16 = w_in.astype(BF16)
    zero_s5 = jnp.zeros((2, 2, nbc, S5_STATE), F32)
    zero_rw = jnp.zeros((2, RWKV_HEAD, RWKV_HEAD, nbc * H_RWKV), F32)
    s5_out, rw_out = [], []
    for l in range(nl):
        p = dict(s5_d=s5_d[l], s5_w_glu=s5_w_glu[l], rw_mu=rw_mu[l], rw_w0=rw_w0[l], rw_w2=rw_w2[l],
                 rw_a0=rw_a0[l], rw_a2=rw_a2[l], rw_g2=rw_g2[l], rw_k_k=rw_k_k[l], rw_k_a=rw_k_a[l],
                 rw_r_k=rw_r_k[l], rw_lnx_w=rw_lnx_w[l], rw_lnx_b=rw_lnx_b[l], w_up_s5=w_up_s5[l],
                 w_up_rwkv=w_up_rwkv[l], w_out=w_out[l], ln1_g=ln1_g[l], ln1_b=ln1_b[l])
        pat = pats[l]
        u, zr, gs, gr = _inproj(x, pat, w_in_bf16, l, rc)

        s5w = (wb_re[l], wb_im[l], ab_re[l], ab_im[l], wc[l])
        y5c, hfin = _s5_scan(u, *s5w, zero_s5, nbc, 0, rc)
        h0s = state_s5[:, l].reshape(nbs, 2, 2, S5_STATE).transpose(1, 2, 0, 3)
        y5s, _ = _s5_scan(u, *s5w, h0s, nbs, rc, rs)
        s5_out.append(hfin.transpose(2, 0, 1, 3).reshape(nbc, 2, 2, G_S5, N_S5))

        prep_c = _rwkv_prep(zr, p, nbc, tc_len, False, 0)
        prep_s = _rwkv_prep(zr, p, nbs, ts_len, True, rc)
        yc, sfin = _rwkv_scan(*_relayout_ctx(prep_c, nbc, tc_len), zero_rw, tc=min(64, tc_len), folded=False)
        ys, _ = _rwkv_scan(*_relayout_smp(prep_s, nbs, ts_len, 0), _rwkv_state_smp(state_rwkv[:, l], nbs),
                           tc=min(64, ts_len // 2), folded=True)
        yrw = (_unmix_ctx(yc, nbc, tc_len), _unmix_smp(ys, nbs, ts_len))
        rw_out.append(_rwkv_state_out_ctx(sfin, nbc))

        x = _mix_out(x, pat, u, gs, gr, (y5c, y5s), yrw, (prep_c[3], prep_s[3]), (prep_c[4], prep_s[4]), p, rc)
        x = _moe(x, pat, w_router, b_router, w_exp_gate, w_exp_up, w_exp_down, l, ln2_g[l], ln2_b[l], rc)

    y_prompt = _from_time_major(x, nbc, tc_len, 0)
    y_sample = _from_time_major(x, nbs, ts_len, rc)
    return (y_prompt, y_sample, jnp.stack(s5_out, 1), jnp.stack(rw_out, 1))
```

```python
import functools

import numpy as np
import jax
import jax.numpy as jnp
from jax import lax
from jax.experimental import pallas as pl
from jax.experimental.pallas import tpu as pltpu

D_MODEL = 1024
DEPTH = 2
GRID_W = 64
D_S5 = 512
S5_GROUP = 16
G_S5 = 32
N_S5 = 64
S5_STATE = G_S5 * N_S5
D_RWKV = 512
RWKV_HEAD = 64
H_RWKV = 8
LORA_W = 64
LORA_A = 64
LORA_G = 128
RWKV_COLS = 3 * D_RWKV + LORA_W + LORA_A + LORA_G
N_EXPERTS = 16
N_GROUPS = 4
EXPERTS_PER_GROUP = 4
D_EXPERT = 512
ALPHA = (2 * DEPTH) ** 0.25
LN_EPS = 1e-5
GN_EPS = 64e-5

SUBLANES = 8
LANES = 128
VMEM_LIMIT = 56 * 1024 * 1024

F32 = jnp.float32
BF16 = jnp.bfloat16


def _params(sem):
    return pltpu.CompilerParams(dimension_semantics=sem, vmem_limit_bytes=VMEM_LIMIT)


def _dot(a, b):
    return jnp.dot(a.astype(BF16), b.astype(BF16), preferred_element_type=F32)


def _dot_split(x, w_exact):
    hi = x.astype(BF16)
    lo = (x - hi.astype(F32)).astype(BF16)
    return (jnp.dot(hi, w_exact, preferred_element_type=F32)
            + jnp.dot(lo, w_exact, preferred_element_type=F32))


def _sigmoid(x):
    return 1.0 / (1.0 + jnp.exp(-x))


def _silu(x):
    return x * _sigmoid(x)


def _layer_norm(x, g, b):
    mu = jnp.mean(x, -1, keepdims=True)
    xc = x - mu
    var = jnp.mean(xc * xc, -1, keepdims=True)
    return xc * lax.rsqrt(var + LN_EPS) * g + b


def _modulate(x, shift8, scale8):
    rows, d = x.shape
    x3 = x.reshape(rows // SUBLANES, SUBLANES, d)
    return (x3 * (1.0 + scale8)[None] + shift8[None]).reshape(rows, d)


def _to_time_major_kernel(xc_ref, xs_ref, o_ref, *, ctx_tiles):
    i = pl.program_id(0)

    @pl.when(i < ctx_tiles)
    def _():
        nb, steps, _ = xc_ref.shape
        for t in range(steps):
            o_ref[pl.ds(t * nb, nb), :] = xc_ref[:, t, :]

    @pl.when(i >= ctx_tiles)
    def _():
        nb, steps, d = xs_ref.shape
        rows = nb * steps
        r = lax.broadcasted_iota(jnp.int32, (rows, rows), 0)
        c = lax.broadcasted_iota(jnp.int32, (rows, rows), 1)
        perm = ((r % nb) * steps + r // nb == c).astype(BF16)
        o_ref[...] = _permute_rows(perm, xs_ref[...].reshape(rows, d))


def _permute_rows(perm, x):
    hi = x.astype(BF16)
    r1 = x - hi.astype(F32)
    mid = r1.astype(BF16)
    lo = (r1 - mid.astype(F32)).astype(BF16)
    return (jnp.dot(perm, hi, preferred_element_type=F32) + jnp.dot(perm, mid, preferred_element_type=F32)
            + jnp.dot(perm, lo, preferred_element_type=F32))


def _to_time_major(x_ctx, x_smp, tm=256):
    nbc, tcl, d = x_ctx.shape
    nbs, tsl, _ = x_smp.shape
    ct = nbc * tcl // tm
    st = nbs * tsl // tm
    return pl.pallas_call(
        functools.partial(_to_time_major_kernel, ctx_tiles=ct),
        grid=(ct + st,),
        in_specs=[pl.BlockSpec((nbc, tm // nbc, d), lambda i: (0, jnp.minimum(i, ct - 1), 0)),
                  pl.BlockSpec((nbs, tm // nbs, d), lambda i: (0, jnp.maximum(i - ct, 0), 0))],
        out_specs=pl.BlockSpec((tm, d), lambda i: (i, 0)),
        out_shape=jax.ShapeDtypeStruct((nbc * tcl + nbs * tsl, d), F32),
        compiler_params=_params(("arbitrary",)),
        name="to_time_major",
    )(x_ctx, x_smp)


def _from_time_major_kernel(x_ref, o_ref):
    nb, steps, _ = o_ref.shape
    if nb >= SUBLANES:
        for t in range(steps):
            o_ref[:, t, :] = x_ref[pl.ds(t * nb, nb), :]
    else:
        rows = nb * steps
        r = lax.broadcasted_iota(jnp.int32, (rows, rows), 0)
        c = lax.broadcasted_iota(jnp.int32, (rows, rows), 1)
        perm = ((r % steps) * nb + r // steps == c).astype(BF16)
        o_ref[...] = _permute_rows(perm, x_ref[...]).reshape(o_ref.shape)


def _from_time_major(x, nb, seq, row0, tm=256):
    d = x.shape[1]
    t0 = row0 // tm
    return pl.pallas_call(
        _from_time_major_kernel,
        grid=(seq * nb // tm,),
        in_specs=[pl.BlockSpec((tm, d), lambda i: (t0 + i, 0))],
        out_specs=pl.BlockSpec((nb, tm // nb, d), lambda i: (0, i, 0)),
        out_shape=jax.ShapeDtypeStruct((nb, seq, d), F32),
        compiler_params=_params(("arbitrary",)),
        name="from_time_major",
    )(x)


def _ada_kernel(cond_ref, w_ref, b_ref, o_ref):
    c = cond_ref[...]
    o_ref[0] = _dot(_silu(c), w_ref[0]) + b_ref[0]


def _ada_mod(cond8, w_ada, b_ada):
    nl = w_ada.shape[0]
    d = D_MODEL
    return pl.pallas_call(
        _ada_kernel,
        grid=(nl, 6),
        in_specs=[pl.BlockSpec((SUBLANES, d), lambda l, k: (0, 0)),
                  pl.BlockSpec((1, d, d), lambda l, k: (l, 0, k)),
                  pl.BlockSpec((1, 1, d), lambda l, k: (l, 0, k))],
        out_specs=pl.BlockSpec((1, SUBLANES, d), lambda l, k: (l, 0, k)),
        out_shape=jax.ShapeDtypeStruct((nl, SUBLANES, 6 * d), F32),
        compiler_params=_params(("arbitrary", "arbitrary")),
        name="ada_mod",
    )(cond8, w_ada, b_ada.reshape(nl, 1, 6 * d))


IN_SPLITS = (0, D_S5, D_S5 + RWKV_COLS, D_S5 + RWKV_COLS + D_MODEL, D_S5 + RWKV_COLS + 2 * D_MODEL)


def _inproj_kernel(x_ref, pat_ref, w_ref, u_ref, zr_ref, gs_ref, gr_ref):
    h = _modulate(x_ref[...], pat_ref[0, 0], pat_ref[0, 1]).astype(BF16)
    for k, out in enumerate((u_ref, zr_ref, gs_ref, gr_ref)):
        out[...] = jnp.dot(h, w_ref[0, :, IN_SPLITS[k]:IN_SPLITS[k + 1]], preferred_element_type=F32)


def _inproj(x, pat, w_in_bf16, layer, rows_per_trunk, tm=512):
    rows = x.shape[0]
    tm = min(tm, rows_per_trunk)
    d = D_MODEL
    tpt = rows_per_trunk // tm
    widths = [IN_SPLITS[k + 1] - IN_SPLITS[k] for k in range(4)]
    rowblk = lambda n: pl.BlockSpec((tm, n), lambda i: (i, 0))
    return pl.pallas_call(
        _inproj_kernel,
        grid=(rows // tm,),
        in_specs=[rowblk(d),
                  pl.BlockSpec((1, 6, SUBLANES, d), lambda i: (i // tpt, 0, 0, 0)),
                  pl.BlockSpec((1, d, IN_SPLITS[-1]), lambda i: (layer, 0, 0))],
        out_specs=[rowblk(n) for n in widths],
        out_shape=[jax.ShapeDtypeStruct((rows, n), F32) for n in widths],
        compiler_params=_params(("arbitrary",)),
        name="in_proj",
    )(x, pat, w_in_bf16)


def _s5_disc_kernel(are_ref, aim_ref, ldt_ref, bre_ref, bim_ref, cre_ref, cim_ref,
                    abre_ref, abim_ref, wbre_ref, wbim_ref, wc_ref):
    a_re = jnp.minimum(are_ref[0], -1e-4)
    a_im = aim_ref[0]
    dt = jnp.exp(ldt_ref[0])
    mag = jnp.exp(a_re * dt)
    ab_re = mag * jnp.cos(a_im * dt)
    ab_im = mag * jnp.sin(a_im * dt)
    den = a_re * a_re + a_im * a_im
    nr = ab_re - 1.0
    q_re = (nr * a_re + ab_im * a_im) / den
    q_im = (ab_im * a_re - nr * a_im) / den
    abre_ref[0] = ab_re
    abim_ref[0] = ab_im
    b_re = bre_ref[0]
    b_im = bim_ref[0]
    bb_re = (q_re[:, None, :] * b_re - q_im[:, None, :] * b_im).reshape(D_S5, N_S5)
    bb_im = (q_re[:, None, :] * b_im + q_im[:, None, :] * b_re).reshape(D_S5, N_S5)

    def spread(x, n_rep, row_group, col_group):
        rows, w = x.shape
        tiled_eye = (lax.broadcasted_iota(jnp.int32, (w, n_rep * w), 0)
                     == lax.broadcasted_iota(jnp.int32, (w, n_rep * w), 1) % w).astype(BF16)
        rep = jnp.dot(x.astype(BF16), tiled_eye, preferred_element_type=F32)
        keep = (lax.broadcasted_iota(jnp.int32, rep.shape, 0) // row_group
                == lax.broadcasted_iota(jnp.int32, rep.shape, 1) // col_group)
        return jnp.where(keep, rep, 0.0).astype(BF16)

    wbre_ref[0] = spread(bb_re, G_S5, S5_GROUP, N_S5)
    wbim_ref[0] = spread(bb_im, G_S5, S5_GROUP, N_S5)
    wc_ref[0, :S5_STATE] = spread(cre_ref[0], G_S5, N_S5, S5_GROUP)
    wc_ref[0, S5_STATE:] = spread(-cim_ref[0], G_S5, N_S5, S5_GROUP)


def _s5_weights(a_re, a_im, log_dt, b_re, b_im, c_re, c_im):
    m = a_re.shape[0]
    s = S5_STATE
    blk = lambda *shape: pl.BlockSpec((1,) + shape, lambda i: (i,) + (0,) * len(shape))
    return pl.pallas_call(
        _s5_disc_kernel,
        grid=(m,),
        in_specs=[blk(G_S5, N_S5), blk(G_S5, N_S5), blk(G_S5, 1), blk(G_S5, S5_GROUP, N_S5),
                  blk(G_S5, S5_GROUP, N_S5), blk(s, S5_GROUP), blk(s, S5_GROUP)],
        out_specs=[blk(G_S5, N_S5), blk(G_S5, N_S5), blk(D_S5, s), blk(D_S5, s), blk(2 * s, D_S5)],
        out_shape=[jax.ShapeDtypeStruct((m, G_S5, N_S5), F32), jax.ShapeDtypeStruct((m, G_S5, N_S5), F32),
                   jax.ShapeDtypeStruct((m, D_S5, s), BF16), jax.ShapeDtypeStruct((m, D_S5, s), BF16),
                   jax.ShapeDtypeStruct((m, 2 * s, D_S5), BF16)],
        compiler_params=_params(("arbitrary",)),
        name="s5_weights",
    )(a_re, a_im, log_dt, b_re, b_im, c_re, c_im)


def _s5_scan_kernel(u_ref, wbre_ref, wbim_ref, abre_ref, abim_ref, wc_ref, h0_ref,
                    y_ref, hfin_ref, bure_ref, buim_ref, hre_ref, him_ref, *, nb, lane_w):
    d = pl.program_id(0)
    c = pl.program_id(1)
    nc = pl.num_programs(1)
    rows = u_ref.shape[0]

    @pl.when(c == 0)
    def _():
        hre_ref[...] = h0_ref[0, 0]
        him_ref[...] = h0_ref[0, 1]

    ub = u_ref[...].astype(BF16)
    ct = 2 * LANES
    for j in range(S5_STATE // ct):
        first_channel = j * ct * S5_GROUP // N_S5
        ks = slice(first_channel // ct * ct, first_channel // ct * ct + ct)
        js = slice(ct * j, ct * (j + 1))
        bure_ref[:, js] = jnp.dot(ub[:, ks], wbre_ref[0, ks, js], preferred_element_type=F32)
        buim_ref[:, js] = jnp.dot(ub[:, ks], wbim_ref[0, ks, js], preferred_element_type=F32)

    fwd = d == 0
    if nb >= SUBLANES:
        steps = rows // nb
        for lc in range(S5_STATE // lane_w):
            ls = slice(lc * lane_w, (lc + 1) * lane_w)
            ar = jnp.broadcast_to(abre_ref[0, :, ls], (nb, lane_w))
            ai = jnp.broadcast_to(abim_ref[0, :, ls], (nb, lane_w))

            def body(s, carry, ls=ls, ar=ar, ai=ai):
                hr, hi = carry
                t = jnp.where(fwd, s, steps - 1 - s)
                r0 = pl.multiple_of(t * nb, nb)
                br = bure_ref[pl.ds(r0, nb), ls]
                bi = buim_ref[pl.ds(r0, nb), ls]
                nr = ar * hr - ai * hi + br
                ni = ar * hi + ai * hr + bi
                bure_ref[pl.ds(r0, nb), ls] = nr
                buim_ref[pl.ds(r0, nb), ls] = ni
                return nr, ni

            hr, hi = lax.fori_loop(0, steps, body, (hre_ref[:, ls], him_ref[:, ls]))
            hre_ref[:, ls] = hr
            him_ref[:, ls] = hi
    else:
        per = SUBLANES // nb
        groups = rows // SUBLANES
        row_id = lax.broadcasted_iota(jnp.int32, (SUBLANES, lane_w), 0) // nb
        shift = jnp.where(fwd, nb, SUBLANES - nb)
        for lc in range(S5_STATE // lane_w):
            ls = slice(lc * lane_w, (lc + 1) * lane_w)
            ar = jnp.broadcast_to(abre_ref[0, :, ls], (SUBLANES, lane_w))
            ai = jnp.broadcast_to(abim_ref[0, :, ls], (SUBLANES, lane_w))
            h0r = hre_ref[:, ls]
            h0i = him_ref[:, ls]

            def body(gidx, carry, ls=ls, ar=ar, ai=ai):
                tr, ti = carry
                g = jnp.where(fwd, gidx, groups - 1 - gidx)
                r0 = pl.multiple_of(g * SUBLANES, SUBLANES)
                br = bure_ref[pl.ds(r0, SUBLANES), ls]
                bi = buim_ref[pl.ds(r0, SUBLANES), ls]
                for k in range(per):
                    pr = pltpu.roll(tr, shift, 0)
                    pi = pltpu.roll(ti, shift, 0)
                    nr = ar * pr - ai * pi + br
                    ni = ar * pi + ai * pr + bi
                    tgt = jnp.where(fwd, k, per - 1 - k)
                    sel = row_id == tgt
                    tr = jnp.where(sel, nr, tr)
                    ti = jnp.where(sel, ni, ti)
                bure_ref[pl.ds(r0, SUBLANES), ls] = tr
                buim_ref[pl.ds(r0, SUBLANES), ls] = ti
                return tr, ti

            tr, ti = lax.fori_loop(0, groups, body, (h0r, h0i))
            hre_ref[:, ls] = tr
            him_ref[:, ls] = ti

    for q in range(D_S5 // ct):
        qs = slice(ct * q, ct * (q + 1))
        ss = slice(q * ct * N_S5 // S5_GROUP, (q + 1) * ct * N_S5 // S5_GROUP)
        si = slice(S5_STATE + ss.start, S5_STATE + ss.stop)
        y_ref[0, :, qs] = (jnp.dot(bure_ref[:, ss].astype(BF16), wc_ref[0, ss, qs], preferred_element_type=F32)
                           + jnp.dot(buim_ref[:, ss].astype(BF16), wc_ref[0, si, qs],
                                     preferred_element_type=F32))

    @pl.when(c == nc - 1)
    def _():
        hfin_ref[0, 0] = hre_ref[...]
        hfin_ref[0, 1] = him_ref[...]


def _s5_scan(u, wb_re, wb_im, ab_re, ab_im, wc, h0, nb, row0, rows, chunk_rows=1024):
    if nb < SUBLANES:
        h0 = jnp.tile(h0, (1, 1, SUBLANES // nb, 1))
    srows = max(nb, SUBLANES)
    cr = min(chunk_rows, rows)
    nc = rows // cr
    c0 = row0 // cr
    s = S5_STATE
    lane_w = 256 if nb >= SUBLANES else 512
    chunk = lambda d, c: c + d * (nc - 1 - 2 * c)
    y, hfin = pl.pallas_call(
        functools.partial(_s5_scan_kernel, nb=nb, lane_w=lane_w),
        grid=(2, nc),
        in_specs=[pl.BlockSpec((cr, D_S5), lambda d, c: (c0 + chunk(d, c), 0)),
                  pl.BlockSpec((1, D_S5, s), lambda d, c: (d, 0, 0)),
                  pl.BlockSpec((1, D_S5, s), lambda d, c: (d, 0, 0)),
                  pl.BlockSpec((1, 1, s), lambda d, c: (d, 0, 0)),
                  pl.BlockSpec((1, 1, s), lambda d, c: (d, 0, 0)),
                  pl.BlockSpec((1, 2 * s, D_S5), lambda d, c: (d, 0, 0)),
                  pl.BlockSpec((1, 2, srows, s), lambda d, c: (d, 0, 0, 0))],
        out_specs=[pl.BlockSpec((1, cr, D_S5), lambda d, c: (d, chunk(d, c), 0)),
                   pl.BlockSpec((1, 2, srows, s), lambda d, c: (d, 0, 0, 0))],
        out_shape=[jax.ShapeDtypeStruct((2, rows, D_S5), F32),
                   jax.ShapeDtypeStruct((2, 2, srows, s), F32)],
        scratch_shapes=[pltpu.VMEM((cr, s), F32), pltpu.VMEM((cr, s), F32),
                        pltpu.VMEM((srows, s), F32), pltpu.VMEM((srows, s), F32)],
        compiler_params=_params(("arbitrary", "arbitrary")),
        name="s5_scan",
    )(u, wb_re, wb_im, ab_re, ab_im, wc, h0)
    if nb < SUBLANES:
        hfin = jnp.stack([hfin[0, :, srows - nb:], hfin[1, :, :nb]])
    return y, hfin


def _rwkv_prep_kernel(prev_ref, cur_ref, next_ref, mu_ref, ones_ref, g2_ref, kk_ref, ka_ref, rk_ref,
                      w0_ref, w2_ref, a0_ref, a2_ref,
                      r_ref, v_ref, nkk_ref, bonus_ref, g_ref, w_ref, kd_ref, b_ref, *, nb, seq, grid_shift,
                      scan_layout):
    tm = cur_ref.shape[0]

    def put(ref, x, d=None):
        if not scan_layout:
            if d is None:
                ref[...] = x
            else:
                ref[d] = x
            return
        tiles = _block_transpose(_head_tiles(x), nb)
        for step, tile in enumerate(tiles):
            ref[0 if d is None else d, step] = tile

    i = pl.program_id(0)
    cur = cur_ref[...]
    prv = prev_ref[...]
    nxt = next_ref[...]
    row = lax.broadcasted_iota(jnp.int32, (tm, RWKV_COLS), 0)
    lane = lax.broadcasted_iota(jnp.int32, (tm, RWKV_COLS), 1)
    t = (i * tm + row) // nb

    halo = prv.shape[0]

    def rows_before(s):
        if s == tm:
            return prv
        if s == halo:
            return jnp.concatenate([prv, cur[:tm - s]], axis=0)
        return jnp.where(row < s, pltpu.roll(prv, s, 0), pltpu.roll(cur, s, 0))

    def rows_after(s):
        if s == tm:
            return nxt
        if s == halo:
            return jnp.concatenate([cur[s:], nxt], axis=0)
        return jnp.where(row >= tm - s, pltpu.roll(nxt, tm - s, 0), pltpu.roll(cur, tm - s, 0))

    if grid_shift:
        tw = t % GRID_W
        left = jnp.where(tw == 0, 0.0, rows_before(nb))
        right = jnp.where(tw == GRID_W - 1, 0.0, rows_after(nb))
        up = jnp.where(t < GRID_W, 0.0, rows_before(nb * GRID_W))
        down = jnp.where(t >= seq - GRID_W, 0.0, rows_after(nb * GRID_W))
        m4 = lane % 4
        sh = jnp.where(m4 == 0, left, jnp.where(m4 == 1, right, jnp.where(m4 == 2, up, down)))
    else:
        before = jnp.where(t == 0, 0.0, rows_before(nb))
        after = jnp.where(t == seq - 1, 0.0, rows_after(nb))
        sh = jnp.where(lane % 2 == 0, before, after)

    z = cur + (sh - cur) * mu_ref[...]
    r = z[:, 0:D_RWKV]
    k = z[:, D_RWKV:2 * D_RWKV]
    v = z[:, 2 * D_RWKV:3 * D_RWKV]
    xw = z[:, 3 * D_RWKV:3 * D_RWKV + LORA_W]
    xa = z[:, 3 * D_RWKV + LORA_W:3 * D_RWKV + LORA_W + LORA_A]
    xg = z[:, 3 * D_RWKV + LORA_W + LORA_A:]
    ones = ones_ref[...]

    g_ref[...] = _dot(_sigmoid(xg), g2_ref[...])
    kk = k * kk_ref[...]
    nrm = jnp.sqrt(_dot_split(kk * kk, ones))
    kk = kk / jnp.maximum(nrm, 1e-12)
    put(r_ref, r)
    put(v_ref, v)
    put(nkk_ref, -kk)
    bonus_ref[...] = _dot_split(r * k * rk_ref[...], ones) * v
    txw = jnp.tanh(xw)
    for d in range(2):
        zw = -(w0_ref[d] + _dot(txw, w2_ref[d]))
        softplus = jnp.maximum(zw, 0.0) + jnp.log(1.0 + jnp.exp(-jnp.abs(zw)))
        w_log = -softplus - 0.5
        put(w_ref, jnp.exp(-jnp.exp(w_log)), d)
        a = _sigmoid(a0_ref[d] + _dot(xa, a2_ref[d]))
        put(kd_ref, k * (1.0 + (a - 1.0) * ka_ref[...]), d)
        put(b_ref, kk * a, d)


def _rwkv_prep(zr, p, nb, seq, grid_shift, row0, tm=128, scan_layout=False):
    rows = seq * nb
    nt = rows // tm
    t0 = row0 // tm
    if grid_shift:
        assert tm == nb * GRID_W, "one tile must be one grid row of the latent grid"
    else:
        assert tm % nb == 0 and tm >= nb
    if scan_layout:
        assert tm == H_RWKV * nb and nb * H_RWKV == LANES, "one tile = 8 steps, heads x batch fill the lanes"
    kern = functools.partial(_rwkv_prep_kernel, nb=nb, seq=seq, grid_shift=grid_shift, scan_layout=scan_layout)
    c = D_RWKV
    vec = lambda n: pl.BlockSpec((1, n), lambda i: (0, 0))
    mat = lambda a, b: pl.BlockSpec((a, b), lambda i: (0, 0))
    row_out = pl.BlockSpec((tm, c), lambda i: (i, 0))
    dir_out = pl.BlockSpec((2, tm, c), lambda i: (0, i, 0))
    head_ones = jnp.kron(jnp.eye(H_RWKV, dtype=F32), jnp.ones((RWKV_HEAD, RWKV_HEAD), F32)).astype(BF16)
    halo = tm if grid_shift else nb
    assert halo % SUBLANES == 0
    hpt = tm // halo
    h0 = t0 * hpt
    in_specs = [pl.BlockSpec((halo, RWKV_COLS), lambda i: (h0 + jnp.maximum(i * hpt - 1, 0), 0)),
                pl.BlockSpec((tm, RWKV_COLS), lambda i: (t0 + i, 0)),
                pl.BlockSpec((halo, RWKV_COLS), lambda i: (h0 + jnp.minimum((i + 1) * hpt, nt * hpt - 1), 0)),
                vec(RWKV_COLS), mat(c, c), mat(LORA_G, c), vec(c), vec(c), vec(c),
                pl.BlockSpec((2, 1, c), lambda i: (0, 0, 0)),
                pl.BlockSpec((2, LORA_W, c), lambda i: (0, 0, 0)),
                pl.BlockSpec((2, 1, c), lambda i: (0, 0, 0)),
                pl.BlockSpec((2, LORA_A, c), lambda i: (0, 0, 0))]
    args = [zr, zr, zr, p['rw_mu'].reshape(1, -1), head_ones, p['rw_g2'], p['rw_k_k'].reshape(1, c),
            p['rw_k_a'].reshape(1, c), p['rw_r_k'].reshape(1, c), p['rw_w0'].reshape(2, 1, c), p['rw_w2'],
            p['rw_a0'].reshape(2, 1, c), p['rw_a2']]
    part = jax.ShapeDtypeStruct((rows, c), F32)
    if scan_layout:
        steps = tm // nb
        one = pl.BlockSpec((1, steps, RWKV_HEAD, LANES), lambda i: (0, i, 0, 0))
        two = pl.BlockSpec((2, steps, RWKV_HEAD, LANES), lambda i: (0, i, 0, 0))
        out_specs = [one] * 3 + [row_out] * 2 + [two] * 3
        out_shape = ([jax.ShapeDtypeStruct((1, seq, RWKV_HEAD, LANES), F32)] * 3 + [part] * 2
                     + [jax.ShapeDtypeStruct((2, seq, RWKV_HEAD, LANES), F32)] * 3)
    else:
        out_specs = [row_out] * 5 + [dir_out] * 3
        out_shape = [part] * 5 + [jax.ShapeDtypeStruct((2, rows, c), F32)] * 3
    return pl.pallas_call(
        kern,
        grid=(nt,),
        in_specs=in_specs,
        out_specs=out_specs,
        out_shape=out_shape,
        compiler_params=_params(("arbitrary",)),
        name="rwkv_prep",
    )(*args)


def _rwkv_scan_kernel(a_ref, w_ref, b_ref, k_ref, r_ref, v_ref, s0_ref, y_ref, sfin_ref, s_ref, *,
                      nacc, row_blk, folded):
    d = pl.program_id(0)
    c = pl.program_id(1)
    nc = pl.num_programs(1)
    tc = a_ref.shape[1]
    nj = RWKV_HEAD
    j_unroll = 32 if row_blk >= 32 else nj
    rev = (c >= nc // 2) if folded else (d == 1)

    @pl.when(c == 0)
    def _():
        s_ref[...] = s0_ref[0]

    def step(s, carry):
        t = jnp.where(rev, tc - 1 - s, s)
        zero = jnp.zeros((row_blk, a_ref.shape[3]), F32)
        for rb in range(s_ref.shape[1] // row_blk):
            rows = pl.ds(rb * row_blk, row_blk)

            def pass1(jo, acc):
                acc = list(acc)
                for ji in range(j_unroll):
                    j = jo * j_unroll + ji
                    acc[ji % nacc] = acc[ji % nacc] + s_ref[j, rows, :] * a_ref[0, t, pl.ds(j, 1), :]
                return tuple(acc)

            sa = functools.reduce(lambda x, y: x + y, lax.fori_loop(0, nj // j_unroll, pass1, (zero,) * nacc))
            vv = v_ref[0, t, rows, :]

            def pass2(jo, acc):
                acc = list(acc)
                for ji in range(j_unroll):
                    j = jo * j_unroll + ji
                    s_new = (s_ref[j, rows, :] * w_ref[0, t, pl.ds(j, 1), :]
                             + sa * b_ref[0, t, pl.ds(j, 1), :]
                             + vv * k_ref[0, t, pl.ds(j, 1), :])
                    s_ref[j, rows, :] = s_new
                    acc[ji % nacc] = acc[ji % nacc] + s_new * r_ref[0, t, pl.ds(j, 1), :]
                return tuple(acc)

            y_ref[0, t, rows, :] = functools.reduce(
                lambda x, y: x + y, lax.fori_loop(0, nj // j_unroll, pass2, (zero,) * nacc))
        return carry

    lax.fori_loop(0, tc, step, 0)

    @pl.when(c == nc - 1)
    def _():
        sfin_ref[0] = s_ref[...]


def _rwkv_scan(a, w, b, k, r, v, s0, tc, folded):
    _, tp, _, nl = w.shape
    iv = v.shape[2]
    row_blk = min(iv, 32)
    nacc = 4 if row_blk <= 16 else 2
    nc_half = tp // tc
    if folded:
        nd, nc = 1, 2 * nc_half
        tmap = lambda d, c: (c // nc_half, jnp.where(c < nc_half, c, nc - 1 - c), 0, 0)
        shared = per_dir = tmap
    else:
        nd, nc = 2, nc_half
        chunk = lambda d, c: c + d * (nc - 1 - 2 * c)
        shared = lambda d, c: (0, chunk(d, c), 0, 0)
        per_dir = lambda d, c: (d, chunk(d, c), 0, 0)
    key = lambda m: pl.BlockSpec((1, tc, RWKV_HEAD, nl), m)
    val = lambda m: pl.BlockSpec((1, tc, iv, nl), m)
    sspec = pl.BlockSpec((1, RWKV_HEAD, iv, nl), lambda d, c: (d, 0, 0, 0))
    return pl.pallas_call(
        functools.partial(_rwkv_scan_kernel, nacc=nacc, row_blk=row_blk, folded=folded),
        grid=(nd, nc),
        in_specs=[key(shared), key(per_dir), key(per_dir), key(per_dir), key(shared), val(shared), sspec],
        out_specs=[val(per_dir), sspec],
        out_shape=[jax.ShapeDtypeStruct((2, tp, iv, nl), F32),
                   jax.ShapeDtypeStruct((nd, RWKV_HEAD, iv, nl), F32)],
        scratch_shapes=[pltpu.VMEM((RWKV_HEAD, iv, nl), F32)],
        compiler_params=_params(("arbitrary", "arbitrary")),
        name="rwkv_scan",
    )(a, w, b, k, r, v, s0)


SMP_REP = 4
SMP_ROWS = RWKV_HEAD // SMP_REP


def _block_transpose(tiles, bs):
    tiles = list(tiles)
    lane = lax.broadcasted_iota(jnp.int32, tiles[0].shape, 1)
    for kbit in range(3):
        sft = bs << kbit
        bit = (lane // sft) % 2 == 1
        for r in range(8):
            if (r >> kbit) & 1:
                continue
            r2 = r | (1 << kbit)
            lo, hi = tiles[r], tiles[r2]
            tiles[r] = jnp.where(bit, pltpu.roll(hi, sft, 1), lo)
            tiles[r2] = jnp.where(bit, hi, pltpu.roll(lo, LANES - sft, 1))
    return tiles


def _head_tiles(x):
    xt = x.T
    return [xt[h * RWKV_HEAD:(h + 1) * RWKV_HEAD, :] for h in range(H_RWKV)]


def _from_head_tiles(tiles):
    return jnp.concatenate(tiles, axis=0).T


def _relayout_ctx_kernel(r_ref, v_ref, nkk_ref, w_ref, kd_ref, b_ref, a_o, r_o, v_o, w_o, b_o, k_o, *, nb):
    steps = a_o.shape[1]
    for src, dst in ((nkk_ref, a_o), (r_ref, r_o), (v_ref, v_o)):
        tiles = _block_transpose(_head_tiles(src[...]), nb)
        for t in range(steps):
            dst[0, t] = tiles[t]
    for src, dst in ((w_ref, w_o), (b_ref, b_o), (kd_ref, k_o)):
        for d in range(2):
            tiles = _block_transpose(_head_tiles(src[d]), nb)
            for t in range(steps):
                dst[d, t] = tiles[t]


def _relayout_ctx(prep, nb, seq):
    r, v, nkk, _, _, w, kd, bb = prep
    steps = LANES // nb
    assert steps == H_RWKV and nb * H_RWKV == LANES
    tm = steps * nb
    nt = seq // steps
    c = D_RWKV
    nat = pl.BlockSpec((tm, c), lambda i: (i, 0))
    nat2 = pl.BlockSpec((2, tm, c), lambda i: (0, i, 0))
    out1 = pl.BlockSpec((1, steps, RWKV_HEAD, LANES), lambda i: (0, i, 0, 0))
    out2 = pl.BlockSpec((2, steps, RWKV_HEAD, LANES), lambda i: (0, i, 0, 0))
    s1 = jax.ShapeDtypeStruct((1, seq, RWKV_HEAD, LANES), F32)
    s2 = jax.ShapeDtypeStruct((2, seq, RWKV_HEAD, LANES), F32)
    a, r_, v_, w_, b_, k_ = pl.pallas_call(
        functools.partial(_relayout_ctx_kernel, nb=nb),
        grid=(nt,),
        in_specs=[nat, nat, nat, nat2, nat2, nat2],
        out_specs=[out1, out1, out1, out2, out2, out2],
        out_shape=[s1, s1, s1, s2, s2, s2],
        compiler_params=_params(("arbitrary",)),
        name="rwkv_relayout_ctx",
    )(r, v, nkk, w, kd, bb)
    return a, w_, b_, k_, r_, v_


def _unmix_ctx_kernel(y_ref, o_ref, *, nb):
    steps = y_ref.shape[1]
    tiles = [y_ref[0, t] + y_ref[1, t] for t in range(steps)]
    o_ref[...] = _from_head_tiles(_block_transpose(tiles, nb))


def _unmix_ctx(y, nb, seq):
    steps = LANES // nb
    tm = steps * nb
    return pl.pallas_call(
        functools.partial(_unmix_ctx_kernel, nb=nb),
        grid=(seq // steps,),
        in_specs=[pl.BlockSpec((2, steps, RWKV_HEAD, LANES), lambda i: (0, i, 0, 0))],
        out_specs=pl.BlockSpec((tm, D_RWKV), lambda i: (i, 0)),
        out_shape=jax.ShapeDtypeStruct((seq * nb, D_RWKV), F32),
        compiler_params=_params(("arbitrary",)),
        name="rwkv_unmix_ctx",
    )(y)


def _relayout_smp_kernel(ra, rb, va, vb, na, nb_, wa, wb, ka, kb, ba, bb, sel_ref,
                         a_o, r_o, v_o, w_o, b_o, k_o):
    steps = a_o.shape[1]
    lane_v = lax.broadcasted_iota(jnp.int32, (SMP_ROWS, LANES), 1)

    def conv(xa, xb):
        tiles = _block_transpose(_head_tiles(jnp.concatenate([xa, xb], axis=0)), 2)
        parts = []
        for x in tiles:
            hi = x.astype(BF16)
            r1 = x - hi.astype(F32)
            mid = r1.astype(BF16)
            lo = (r1 - mid.astype(F32)).astype(BF16)
            parts.append((hi, mid, lo))
        return parts

    def expand(o0, o1, half):
        order0 = range(8) if half == 0 else range(7, -1, -1)
        order1 = range(7, -1, -1) if half == 0 else range(8)
        cols = []
        for term in range(3):
            cols.append(jnp.concatenate([o0[s][term] for s in order0], axis=0))
            cols.append(jnp.concatenate([o1[s][term] for s in order1], axis=0))
        lhs = jnp.concatenate(cols, axis=1)
        out = [None] * steps
        for pair in range(steps // 16):
            res = jnp.dot(lhs, sel_ref[half, pair], preferred_element_type=F32)
            for q in range(2):
                for s_lo in range(8):
                    out[8 * (2 * pair + q) + s_lo] = res[64 * s_lo:64 * (s_lo + 1), LANES * q:LANES * (q + 1)]
        return out

    def value_rows(full):
        blk = lane_v // (LANES // SMP_REP)
        rows = [full[k * SMP_ROWS:(k + 1) * SMP_ROWS] for k in range(SMP_REP)]
        return jnp.where(blk == 0, rows[0], jnp.where(blk == 1, rows[1], jnp.where(blk == 2, rows[2], rows[3])))

    for srca, srcb, dst in ((na, nb_, a_o), (ra, rb, r_o), (va, vb, v_o)):
        o = conv(srca[...], srcb[...])
        for half in range(2):
            tiles = expand(o, o, half)
            for s in range(steps):
                dst[half, s] = value_rows(tiles[s]) if dst is v_o else tiles[s]
    for srca, srcb, dst in ((wa, wb, w_o), (ba, bb, b_o), (ka, kb, k_o)):
        o0 = conv(srca[0], srcb[0])
        o1 = conv(srca[1], srcb[1])
        for half in range(2):
            tiles = expand(o0, o1, half)
            for s in range(steps):
                dst[half, s] = tiles[s]


def _smp_selection():
    sel = np.zeros((2, 2, 6, LANES, 2 * LANES), np.float32)
    for half in range(2):
        for pair in range(2):
            for q in range(2):
                s_hi = 2 * pair + q
                grp_a = 0 * 64 + 16 * s_hi
                grp_b = 1 * 64 + 16 * (3 - s_hi)
                src = (grp_a, grp_b) if half == 0 else (grp_b, grp_a)
                for d in range(2):
                    for blk in range(SMP_REP):
                        for hb in range(16):
                            col = q * LANES + blk * 32 + d * 16 + hb
                            for term in range(3):
                                sel[half, pair, 2 * term + d, src[d] + hb, col] = 1.0
    return jnp.asarray(sel.reshape(2, 2, 6 * LANES, 2 * LANES), BF16)


def _relayout_smp(prep, nb, seq, row0):
    r, v, nkk, _, _, w, kd, bb = prep
    assert 2 * nb * H_RWKV * SMP_REP == LANES
    steps = 32
    tm = steps * nb
    nt = seq // steps
    half_t = seq // 2
    c = D_RWKV
    t0 = row0 // tm
    blk_a = pl.BlockSpec((tm, c), lambda g: (t0 + g, 0))
    blk_b = pl.BlockSpec((tm, c), lambda g: (t0 + nt - 1 - g, 0))
    blk2_a = pl.BlockSpec((2, tm, c), lambda g: (0, t0 + g, 0))
    blk2_b = pl.BlockSpec((2, tm, c), lambda g: (0, t0 + nt - 1 - g, 0))
    okey = pl.BlockSpec((2, steps, RWKV_HEAD, LANES), lambda g: (0, g, 0, 0))
    oval = pl.BlockSpec((2, steps, SMP_ROWS, LANES), lambda g: (0, g, 0, 0))
    skey = jax.ShapeDtypeStruct((2, half_t, RWKV_HEAD, LANES), F32)
    sval = jax.ShapeDtypeStruct((2, half_t, SMP_ROWS, LANES), F32)
    a, r_, v_, w_, b_, k_ = pl.pallas_call(
        _relayout_smp_kernel,
        grid=(nt // 2,),
        in_specs=[blk_a, blk_b, blk_a, blk_b, blk_a, blk_b, blk2_a, blk2_b, blk2_a, blk2_b, blk2_a, blk2_b,
                  pl.BlockSpec((2, 2, 6 * LANES, 2 * LANES), lambda g: (0, 0, 0, 0))],
        out_specs=[okey, okey, oval, okey, okey, okey],
        out_shape=[skey, skey, sval, skey, skey, skey],
        compiler_params=_params(("arbitrary",)),
        name="rwkv_relayout_smp",
    )(r, r, v, v, nkk, nkk, w, w, kd, kd, bb, bb, _smp_selection())
    return a, w_, b_, k_, r_, v_


def _unmix_smp_kernel(y_ref, o_ref, *, half_tiles):
    q = pl.program_id(0)
    steps = y_ref.shape[1]
    low = q < half_tiles
    h0 = jnp.where(low, 0, 1)
    lane = lax.broadcasted_iota(jnp.int32, (SMP_ROWS, LANES), 1)
    tiles = []
    for s_lo in range(8):
        blocks = []
        for blk in range(SMP_REP):
            acc = jnp.zeros((SMP_ROWS, LANES), F32)
            for s_hi in range(steps // 8):
                s = 8 * s_hi + s_lo
                u = jnp.where(low, s, steps - 1 - s)
                d0 = y_ref[h0, u]
                d1 = y_ref[1 - h0, u]
                sh0 = (16 * s_hi - 32 * blk) % LANES
                sh1 = (16 * s_hi - 32 * blk - 16) % LANES
                t0 = d0 if sh0 == 0 else pltpu.roll(d0, sh0, 1)
                t1 = d1 if sh1 == 0 else pltpu.roll(d1, sh1, 1)
                acc = jnp.where(lane // 16 == s_hi, t0 + t1, acc)
            blocks.append(acc)
        tiles.append(jnp.concatenate(blocks, axis=0))
    o_ref[...] = _from_head_tiles(_block_transpose(tiles, 2))


def _unmix_smp(y, nb, seq):
    steps = 64
    tm = steps * nb
    assert tm == LANES
    nq = seq // steps
    half_tiles = nq // 2
    return pl.pallas_call(
        functools.partial(_unmix_smp_kernel, half_tiles=half_tiles),
        grid=(nq,),
        in_specs=[pl.BlockSpec((2, steps, SMP_ROWS, LANES),
                               lambda q: (0, jnp.where(q < half_tiles, q, nq - 1 - q), 0, 0))],
        out_specs=pl.BlockSpec((tm, D_RWKV), lambda q: (q, 0)),
        out_shape=jax.ShapeDtypeStruct((seq * nb, D_RWKV), F32),
        compiler_params=_params(("arbitrary",)),
        name="rwkv_unmix_smp",
    )(y)


def _rwkv_state_smp(state_l, nb):
    s = state_l.reshape(nb, 2, H_RWKV, SMP_REP, SMP_ROWS, RWKV_HEAD)
    s = s.transpose(5, 4, 3, 1, 2, 0)
    return s.reshape(1, RWKV_HEAD, SMP_ROWS, LANES)


def _rwkv_state_out_ctx(sfin, nb):
    s = sfin.reshape(2, RWKV_HEAD, RWKV_HEAD, H_RWKV, nb)
    return s.transpose(4, 0, 3, 2, 1)


def _mix_out_kernel(x_ref, pat_ref, u_ref, gs_ref, gr_ref,
                    y5c_ref, y5s_ref, yrwc_ref, yrws_ref, bonc_ref, bons_ref, gc_ref, gs2_ref,
                    d_ref, wglu_ref, ones_ref, lnxw_ref, lnxb_ref, wups_ref, wupr_ref, wout_ref,
                    ln1g_ref, ln1b_ref, o_ref, *, ctx_tiles):
    is_ctx = pl.program_id(0) < ctx_tiles
    pick = lambda a, b: jnp.where(is_ctx, a, b)
    y5 = u_ref[...] * d_ref[...] + pick(y5c_ref[0] + y5c_ref[1], y5s_ref[0] + y5s_ref[1])
    y5 = jax.nn.gelu(y5)
    y5 = y5 * _sigmoid(_dot(y5, wglu_ref[...]))
    ones = ones_ref[...]
    yr = pick(yrwc_ref[...], yrws_ref[...])
    inv_n = 1.0 / RWKV_HEAD
    mu = _dot_split(yr, ones) * inv_n
    yc = yr - mu
    var = _dot_split(yc * yc, ones) * inv_n
    yr = yc * lax.rsqrt(var + GN_EPS) * lnxw_ref[...] + lnxb_ref[...]
    yr = (yr + pick(bonc_ref[...], bons_ref[...])) * pick(gc_ref[...], gs2_ref[...])
    merged = (_sigmoid(gs_ref[...]) * _dot(y5, wups_ref[...])
              + _sigmoid(gr_ref[...]) * _dot(yr, wupr_ref[...]))
    mix = _dot(merged, wout_ref[...])
    x3 = _modulate(mix, jnp.zeros_like(pat_ref[0, 2]), pat_ref[0, 2] - 1.0)
    o_ref[...] = _layer_norm(ALPHA * x_ref[...] + x3, ln1g_ref[...], ln1b_ref[...])


def _mix_out(x, pat, u, gs, gr, y5, yrw, bonus, g, p, rows_per_trunk, tm=512):
    rows = x.shape[0]
    tm = min(tm, rows_per_trunk)
    d = D_MODEL
    c = D_RWKV
    tpt = rows_per_trunk // tm
    rowblk = lambda n: pl.BlockSpec((tm, n), lambda i: (i, 0))
    ctx_i = lambda i: jnp.minimum(i, tpt - 1)
    smp_i = lambda i: jnp.maximum(i - tpt, 0)
    pair = lambda spec_of: [spec_of(ctx_i), spec_of(smp_i)]
    row_t = lambda f: pl.BlockSpec((tm, c), lambda i: (f(i), 0))
    dir_t = lambda f: pl.BlockSpec((2, tm, c), lambda i: (0, f(i), 0))
    vec = lambda n: pl.BlockSpec((1, n), lambda i: (0, 0))
    mat = lambda a, b: pl.BlockSpec((a, b), lambda i: (0, 0))
    head_ones = jnp.kron(jnp.eye(H_RWKV, dtype=F32), jnp.ones((RWKV_HEAD, RWKV_HEAD), F32)).astype(BF16)
    return pl.pallas_call(
        functools.partial(_mix_out_kernel, ctx_tiles=tpt),
        grid=(rows // tm,),
        in_specs=[rowblk(d),
                  pl.BlockSpec((1, 6, SUBLANES, d), lambda i: (i // tpt, 0, 0, 0)),
                  rowblk(c), rowblk(d), rowblk(d)]
                 + pair(dir_t) + pair(row_t) + pair(row_t) + pair(row_t)
                 + [vec(c), mat(c, c), mat(c, c), vec(c), vec(c), mat(c, d), mat(c, d), mat(d, d),
                    vec(d), vec(d)],
        out_specs=rowblk(d),
        out_shape=jax.ShapeDtypeStruct((rows, d), F32),
        compiler_params=_params(("arbitrary",)),
        name="mix_out",
    )(x, pat, u, gs, gr, *y5, *yrw, *bonus, *g,
      p['s5_d'].reshape(1, c), p['s5_w_glu'].astype(BF16), head_ones,
      p['rw_lnx_w'].reshape(1, c), p['rw_lnx_b'].reshape(1, c),
      p['w_up_s5'].astype(BF16), p['w_up_rwkv'].astype(BF16), p['w_out'].astype(BF16),
      p['ln1_g'].reshape(1, d), p['ln1_b'].reshape(1, d))


def _first_max(x, lane, valid):
    xm = jnp.where(valid, x, -jnp.inf)
    m = jnp.max(xm, -1, keepdims=True)
    idx = jnp.min(jnp.where(xm == m, lane, float(N_EXPERTS)), -1, keepdims=True)
    return m, idx


def _moe_kernel(x_ref, pat_ref, wr_ref, br_ref, wg_ref, wu_ref, wd_ref, ln2g_ref, ln2b_ref, o_ref,
                hp_ref, cp_ref, acc_ref, pos_ref, bounds_ref):
    e = pl.program_id(1)
    ne = pl.num_programs(1)
    tm = x_ref.shape[0]

    @pl.when(e == 0)
    def _():
        h = _modulate(x_ref[...], pat_ref[0, 3], pat_ref[0, 4])
        wr = wr_ref[...]
        w_hi = wr.astype(BF16)
        w_lo = (wr - w_hi.astype(F32)).astype(BF16)
        h_hi = h.astype(BF16)
        h_lo = (h - h_hi.astype(F32)).astype(BF16)
        logits = (jnp.dot(h_hi, w_hi, preferred_element_type=F32)
                  + jnp.dot(h_hi, w_lo, preferred_element_type=F32)
                  + jnp.dot(h_lo, w_hi, preferred_element_type=F32)) + br_ref[...]
        logits = logits - jnp.max(logits, -1, keepdims=True)
        ex = jnp.exp(logits)
        probs = ex / jnp.sum(ex, -1, keepdims=True)
        lane_i = lax.broadcasted_iota(jnp.int32, (tm, N_EXPERTS), 1)
        lane = lane_i.astype(F32)
        grp = (lane_i // EXPERTS_PER_GROUP).astype(F32)
        best_score = jnp.full((tm, 1), -jnp.inf, F32)
        best_grp = jnp.zeros((tm, 1), F32)
        for gi in range(N_GROUPS):
            in_g = grp == float(gi)
            m1, i1 = _first_max(probs, lane, in_g)
            m2, _ = _first_max(probs, lane, in_g & (lane != i1))
            score = m1 + m2
            better = score > best_score
            best_score = jnp.where(better, score, best_score)
            best_grp = jnp.where(better, float(gi), best_grp)
        in_best = grp == best_grp
        m1, i1 = _first_max(probs, lane, in_best)
        m2, i2 = _first_max(probs, lane, in_best & (lane != i1))
        tot = m1 + m2
        comb = jnp.where(lane == i1, m1 / tot, 0.0) + jnp.where(lane == i2, m2 / tot, 0.0)

        glane = lax.broadcasted_iota(jnp.int32, (tm, LANES), 1).astype(F32)
        onehot = (glane == best_grp).astype(BF16)
        r_id = lax.broadcasted_iota(jnp.int32, (tm, tm), 0)
        c_id = lax.broadcasted_iota(jnp.int32, (tm, tm), 1)
        prefix = jnp.dot((r_id >= c_id).astype(BF16), onehot, preferred_element_type=F32)
        counts = prefix[tm - 1:tm, :]
        lane1 = lax.broadcasted_iota(jnp.int32, (1, LANES), 1)
        start = jnp.zeros((1, LANES), F32)
        run = jnp.zeros((1, 1), F32)
        bounds_ref[0] = 0
        for gi in range(N_GROUPS):
            start = start + jnp.where(lane1 == gi, run, 0.0)
            run = run + jnp.sum(jnp.where(lane1 == gi, counts, 0.0), -1, keepdims=True)
            bounds_ref[gi + 1] = run[0, 0].astype(jnp.int32)
        pos = jnp.sum(onehot.astype(F32) * (start + prefix - 1.0), -1, keepdims=True)
        pos_row = jnp.transpose(jnp.broadcast_to(pos, (tm, LANES)))[0:1, :]
        perm = (r_id.astype(F32) == pos_row).astype(BF16)
        hp_ref[...] = jnp.dot(perm, h.astype(BF16), preferred_element_type=F32).astype(BF16)
        cp_ref[...] = _permute_rows(perm, comb)
        pos_ref[...] = pos
        acc_ref[...] = jnp.zeros_like(acc_ref)

    wg = wg_ref[0, 0].astype(BF16)
    wu = wu_ref[0, 0].astype(BF16)
    wd = wd_ref[0, 0].astype(BF16)
    sub = min(tm, MOE_SUB_ROWS)
    grp_first = bounds_ref[e // EXPERTS_PER_GROUP]
    grp_end = bounds_ref[e // EXPERTS_PER_GROUP + 1]

    for rb in range(tm // sub):
        @pl.when((grp_first < (rb + 1) * sub) & (grp_end > rb * sub))
        def _(rb=rb):
            rs = pl.ds(rb * sub, sub)
            hb = hp_ref[rs, :]
            lane = lax.broadcasted_iota(jnp.int32, (sub, N_EXPERTS), 1)
            ce = jnp.sum(jnp.where(lane == e, cp_ref[rs, :], 0.0), -1, keepdims=True)
            hid = (_silu(jnp.dot(hb, wg, preferred_element_type=F32))
                   * jnp.dot(hb, wu, preferred_element_type=F32))
            acc_ref[rs, :] += jnp.dot((hid * ce).astype(BF16), wd, preferred_element_type=F32)

    @pl.when(e == ne - 1)
    def _():
        c_lane = lax.broadcasted_iota(jnp.int32, (tm, tm), 1).astype(F32)
        unperm = (c_lane == pos_ref[...]).astype(BF16)
        ffn = _permute_rows(unperm, acc_ref[...])
        ffn = _modulate(ffn, jnp.zeros_like(pat_ref[0, 5]), pat_ref[0, 5] - 1.0)
        o_ref[...] = _layer_norm(ALPHA * x_ref[...] + ffn, ln2g_ref[...], ln2b_ref[...])


MOE_SUB_ROWS = 256


def _moe(x, pat, w_router, b_router, wg, wu, wd, layer, ln2_g, ln2_b, rows_per_trunk, tm=1024):
    rows = x.shape[0]
    tm = min(tm, rows_per_trunk)
    d = D_MODEL
    tpt = rows_per_trunk // tm
    rowblk = pl.BlockSpec((tm, d), lambda i, e: (i, 0))
    vec = lambda n: pl.BlockSpec((1, n), lambda i, e: (0, 0))
    return pl.pallas_call(
        _moe_kernel,
        grid=(rows // tm, N_EXPERTS),
        in_specs=[rowblk,
                  pl.BlockSpec((1, 6, SUBLANES, d), lambda i, e: (i // tpt, 0, 0, 0)),
                  pl.BlockSpec((d, N_EXPERTS), lambda i, e: (0, 0)), vec(N_EXPERTS),
                  pl.BlockSpec((1, 1, d, D_EXPERT), lambda i, e: (layer, e, 0, 0)),
                  pl.BlockSpec((1, 1, d, D_EXPERT), lambda i, e: (layer, e, 0, 0)),
                  pl.BlockSpec((1, 1, D_EXPERT, d), lambda i, e: (layer, e, 0, 0)),
                  vec(d), vec(d)],
        out_specs=rowblk,
        out_shape=jax.ShapeDtypeStruct((rows, d), F32),
        scratch_shapes=[pltpu.VMEM((tm, d), BF16), pltpu.VMEM((tm, N_EXPERTS), F32),
                        pltpu.VMEM((tm, d), F32), pltpu.VMEM((tm, 1), F32),
                        pltpu.SMEM((SUBLANES,), jnp.int32)],
        compiler_params=_params(("arbitrary", "arbitrary")),
        name="moe",
    )(x, pat, w_router, b_router.reshape(1, N_EXPERTS), wg, wu, wd,
      ln2_g.reshape(1, d), ln2_b.reshape(1, d))


def kernel(x_prompt, x_sample, state_s5, state_rwkv, c, c_ctx, w_ada, b_ada, w_in, s5_a_re, s5_a_im, s5_log_dt, s5_b_re, s5_b_im, s5_c_re, s5_c_im, s5_d, s5_w_glu, rw_mu, rw_w0, rw_w2, rw_a0, rw_a2, rw_g2, rw_k_k, rw_k_a, rw_r_k, rw_lnx_w, rw_lnx_b, w_up_s5, w_up_rwkv, w_out, ln1_g, ln1_b, ln2_g, ln2_b, w_router, b_router, w_exp_gate, w_exp_up, w_exp_down):
    nbc, tc_len, d = x_prompt.shape
    nbs, ts_len, _ = x_sample.shape
    nl = w_ada.shape[0]
    rc = nbc * tc_len
    rs = nbs * ts_len
    assert rc == rs, "both trunks are processed as equal halves of one row-major token matrix"
    assert SUBLANES % nbs == 0 and nbc % SUBLANES == 0

    x = _to_time_major(x_prompt, x_sample)

    cond = jnp.concatenate([c_ctx[None], c], axis=0)
    cond8 = jnp.zeros((SUBLANES, d), F32).at[:cond.shape[0]].set(cond)
    mod = _ada_mod(cond8, w_ada, b_ada).reshape(nl, SUBLANES, 6, d)
    ctx_rows = jnp.zeros((SUBLANES,), jnp.int32)
    smp_rows = 1 + jnp.arange(SUBLANES, dtype=jnp.int32) % nbs
    pat_idx = jnp.stack([ctx_rows, smp_rows])
    pats = mod[:, pat_idx]
    pats = pats.transpose(0, 1, 3, 2, 4)

    m = nl * 2
    ab_re, ab_im, wb_re, wb_im, wc = _s5_weights(
        s5_a_re.reshape(m, G_S5, N_S5), s5_a_im.reshape(m, G_S5, N_S5), s5_log_dt.reshape(m, G_S5, 1),
        s5_b_re.reshape(m, G_S5, N_S5, S5_GROUP).transpose(0, 1, 3, 2),
        s5_b_im.reshape(m, G_S5, N_S5, S5_GROUP).transpose(0, 1, 3, 2),
        s5_c_re.reshape(m, G_S5, S5_GROUP, N_S5).transpose(0, 1, 3, 2).reshape(m, S5_STATE, S5_GROUP),
        s5_c_im.reshape(m, G_S5, S5_GROUP, N_S5).transpose(0, 1, 3, 2).reshape(m, S5_STATE, S5_GROUP))
    ab_re = ab_re.reshape(nl, 2, 1, S5_STATE)
    ab_im = ab_im.reshape(nl, 2, 1, S5_STATE)
    wb_re = wb_re.reshape(nl, 2, D_S5, S5_STATE)
    wb_im = wb_im.reshape(nl, 2, D_S5, S5_STATE)
    wc = wc.reshape(nl, 2, 2 * S5_STATE, D_S5)

    w_in_bf16 = w_in.astype(BF16)
    zero_s5 = jnp.zeros((2, 2, nbc, S5_STATE), F32)
    zero_rw = jnp.zeros((2, RWKV_HEAD, RWKV_HEAD, nbc * H_RWKV), F32)
    s5_out, rw_out = [], []
    for l in range(nl):
        p = dict(s5_d=s5_d[l], s5_w_glu=s5_w_glu[l], rw_mu=rw_mu[l], rw_w0=rw_w0[l], rw_w2=rw_w2[l],
                 rw_a0=rw_a0[l], rw_a2=rw_a2[l], rw_g2=rw_g2[l], rw_k_k=rw_k_k[l], rw_k_a=rw_k_a[l],
                 rw_r_k=rw_r_k[l], rw_lnx_w=rw_lnx_w[l], rw_lnx_b=rw_lnx_b[l], w_up_s5=w_up_s5[l],
                 w_up_rwkv=w_up_rwkv[l], w_out=w_out[l], ln1_g=ln1_g[l], ln1_b=ln1_b[l])
        pat = pats[l]
        u, zr, gs, gr = _inproj(x, pat, w_in_bf16, l, rc)

        s5w = (wb_re[l], wb_im[l], ab_re[l], ab_im[l], wc[l])
        y5c, hfin = _s5_scan(u, *s5w, zero_s5, nbc, 0, rc)
        h0s = state_s5[:, l].reshape(nbs, 2, 2, S5_STATE).transpose(1, 2, 0, 3)
        y5s, _ = _s5_scan(u, *s5w, h0s, nbs, rc, rs)
        s5_out.append(hfin.transpose(2, 0, 1, 3).reshape(nbc, 2, 2, G_S5, N_S5))

        prep_c = _rwkv_prep(zr, p, nbc, tc_len, False, 0, scan_layout=True)
        prep_s = _rwkv_prep(zr, p, nbs, ts_len, True, rc)
        r_c, v_c, a_c, _, _, w_c, k_c, b_c = prep_c
        yc, sfin = _rwkv_scan(a_c, w_c, b_c, k_c, r_c, v_c, zero_rw, tc=min(64, tc_len), folded=False)
        ys, _ = _rwkv_scan(*_relayout_smp(prep_s, nbs, ts_len, 0), _rwkv_state_smp(state_rwkv[:, l], nbs),
                           tc=min(64, ts_len // 2), folded=True)
        yrw = (_unmix_ctx(yc, nbc, tc_len), _unmix_smp(ys, nbs, ts_len))
        rw_out.append(_rwkv_state_out_ctx(sfin, nbc))

        x = _mix_out(x, pat, u, gs, gr, (y5c, y5s), yrw, (prep_c[3], prep_s[3]), (prep_c[4], prep_s[4]), p, rc)
        x = _moe(x, pat, w_router, b_router, w_exp_gate, w_exp_up, w_exp_down, l, ln2_g[l], ln2_b[l], rc)

    y_prompt = _from_time_major(x, nbc, tc_len, 0)
    y_sample = _from_time_major(x, nbs, ts_len, rc)
    return (y_prompt, y_sample, jnp.stack(s5_out, 1), jnp.stack(rw_out, 1))
```
